```python
import math
import jax
import jax.numpy as jnp
from jax import lax
import numpy as np


D_MODEL = 4096
BATCH = 2
SEQ = 8192
DEPTH = 4

MIX_WIDTH = D_MODEL
EPS = 1e-6
ROPE_THETA = 500000.0
Q_BLOCK = 128
S5_WIDTH = MIX_WIDTH // 2
S5_GROUP = 16
S5_GROUPS = S5_WIDTH // S5_GROUP
S5_STATE = 64
S5_DT_MIN = 0.001
S5_DT_MAX = 0.1
S5_LAMBDA_RE_MAX = -1e-4
MLA_NOPE = 128
MLA_ROPE = 64
MLA_V = 128
MLA_HEADS = (MIX_WIDTH - S5_WIDTH) // MLA_V
MLA_Q_RANK = 3 * D_MODEL // 16
MLA_KV_RANK = D_MODEL // 8
DSA_HEAD_DIM = 128
DSA_HEADS = MIX_WIDTH // DSA_HEAD_DIM
DSA_KV_HEADS = 4
DSA_Q_RANK = 3 * D_MODEL // 16
IDX_HEADS = D_MODEL // 128
IDX_DIM = 128
IDX_TOPK_MAX = 256
ROT_DIM = DSA_HEAD_DIM // 4
D_FF = 4 * D_MODEL
EVEN_IN = S5_WIDTH + MLA_Q_RANK + MLA_KV_RANK + MLA_ROPE
ODD_IN = DSA_Q_RANK + 2 * DSA_KV_HEADS * DSA_HEAD_DIM + IDX_DIM + IDX_HEADS
N_EVEN = (DEPTH + 1) // 2
N_ODD = DEPTH // 2

kernel_name = 'hybrid_s5_mla_dsa_trunk'


def rms_norm(x, g):
    xf = x.astype(jnp.float32)
    xf = xf * lax.rsqrt(jnp.mean(xf * xf, axis=-1, keepdims=True) + EPS)
    return (xf * g.astype(jnp.float32)).astype(x.dtype)


def layer_norm(x, g, b):
    xf = x.astype(jnp.float32)
    xc = xf - jnp.mean(xf, axis=-1, keepdims=True)
    var = jnp.mean(xc * xc, axis=-1, keepdims=True)
    return (xc * lax.rsqrt(var + EPS) * g.astype(jnp.float32) + b.astype(jnp.float32)).astype(x.dtype)


def rope_tables(positions, dim):
    inv_freq = ROPE_THETA ** (-jnp.arange(0, dim, 2, dtype=jnp.float32) / dim)
    ang = positions.astype(jnp.float32)[..., None] * inv_freq
    return jnp.cos(ang), jnp.sin(ang)


def apply_rope(x, cos, sin):
    half = x.shape[-1] // 2
    c = cos.astype(x.dtype)
    s = sin.astype(x.dtype)
    x1 = x[..., :half]
    x2 = x[..., half:]
    return jnp.concatenate([x1 * c - x2 * s, x2 * c + x1 * s], axis=-1)


def partial_rope(x, cos, sin):
    return jnp.concatenate([apply_rope(x[..., :ROT_DIM], cos, sin), x[..., ROT_DIM:]], axis=-1)


def to_blocks(a):
    b, s = a.shape[:2]
    return jnp.moveaxis(a.reshape((b, s // Q_BLOCK, Q_BLOCK) + a.shape[2:]), 1, 0)


def from_blocks(a):
    a = jnp.moveaxis(a, 0, 1)
    return a.reshape((a.shape[0], a.shape[1] * a.shape[2]) + a.shape[3:])


def s5_combine(e_i, e_j):
    a_ir, a_ii, b_ir, b_ii = e_i
    a_jr, a_ji, b_jr, b_ji = e_j
    return (a_jr * a_ir - a_ji * a_ii,
            a_jr * a_ii + a_ji * a_ir,
            a_jr * b_ir - a_ji * b_ii + b_jr,
            a_jr * b_ii + a_ji * b_ir + b_ji)


def s5_branch(u, lam_re, lam_im, log_step, b_re, b_im, c_re, c_im, d_skip, glu_a, glu_b):
    bsz, s, _ = u.shape
    f32 = jnp.float32
    uf = u.astype(f32).reshape(bsz, s, S5_GROUPS, S5_GROUP)
    lr = jnp.minimum(lam_re.astype(f32), S5_LAMBDA_RE_MAX)
    li = lam_im.astype(f32)
    step = jnp.exp(log_step.astype(f32))[:, None]
    mag = jnp.exp(lr * step)
    ab_re = mag * jnp.cos(li * step)
    ab_im = mag * jnp.sin(li * step)
    den = lr * lr + li * li
    f_re = ((ab_re - 1.0) * lr + ab_im * li) / den
    f_im = (ab_im * lr - (ab_re - 1.0) * li) / den
    br = b_re.astype(f32)
    bi = b_im.astype(f32)
    bb_re = f_re[..., None] * br - f_im[..., None] * bi
    bb_im = f_re[..., None] * bi + f_im[..., None] * br
    bu_re = jnp.einsum('bsgc,gnc->bsgn', uf, bb_re)
    bu_im = jnp.einsum('bsgc,gnc->bsgn', uf, bb_im)
    a_re = jnp.broadcast_to(ab_re, (1, s) + ab_re.shape)
    a_im = jnp.broadcast_to(ab_im, (1, s) + ab_im.shape)
    _, _, x_re, x_im = lax.associative_scan(s5_combine, (a_re, a_im, bu_re, bu_im), axis=1)
    y = (jnp.einsum('bsgn,gcn->bsgc', x_re, c_re.astype(f32))
         - jnp.einsum('bsgn,gcn->bsgc', x_im, c_im.astype(f32))
         + d_skip.astype(f32).reshape(S5_GROUPS, S5_GROUP) * uf)
    y = jax.nn.gelu(y).reshape(bsz, s, S5_WIDTH).astype(u.dtype)
    return (y @ glu_a) * jax.nn.sigmoid(y @ glu_b)


def mla_attention(q_nope, q_rope, k_nope, k_rope, v):
    bsz, s = q_nope.shape[:2]
    scale = (MLA_NOPE + MLA_ROPE) ** -0.5
    k_pos = jnp.arange(s)

    def block(args):
        qn, qr, qp = args
        sc = jnp.einsum('bqhd,bkhd->bhqk', qn, k_nope) + jnp.einsum('bqhd,bkd->bhqk', qr, k_rope)
        sc = jnp.where(k_pos[None, :] <= qp[:, None], sc.astype(jnp.float32) * scale, -jnp.inf)
        p = jax.nn.softmax(sc, axis=-1).astype(v.dtype)
        return jnp.einsum('bhqk,bkhd->bqhd', p, v)

    out = lax.map(block, (to_blocks(q_nope), to_blocks(q_rope), k_pos.reshape(-1, Q_BLOCK)))
    return from_blocks(out).reshape(bsz, s, MLA_HEADS * MLA_V)


def even_mixer(h, cos, sin, w_in, lam_re, lam_im, log_step, b_re, b_im, c_re, c_im, d_skip,
               glu_a, glu_b, q_norm_g, w_q_up, kv_norm_g, w_kv_up, w_out):
    bsz, s, _ = h.shape
    cuts = [S5_WIDTH, S5_WIDTH + MLA_Q_RANK, S5_WIDTH + MLA_Q_RANK + MLA_KV_RANK]
    u, q_lat, kv_lat, k_rope = jnp.split(h @ w_in, cuts, axis=-1)
    s5_out = s5_branch(u, lam_re, lam_im, log_step, b_re, b_im, c_re, c_im, d_skip, glu_a, glu_b)
    ch, sh = cos[:, :, None, :], sin[:, :, None, :]
    q = (rms_norm(q_lat, q_norm_g) @ w_q_up).reshape(bsz, s, MLA_HEADS, MLA_NOPE + MLA_ROPE)
    q_nope = q[..., :MLA_NOPE]
    q_rope = apply_rope(q[..., MLA_NOPE:], ch, sh)
    kv = (rms_norm(kv_lat, kv_norm_g) @ w_kv_up).reshape(bsz, s, MLA_HEADS, MLA_NOPE + MLA_V)
    k_nope = kv[..., :MLA_NOPE]
    v = kv[..., MLA_NOPE:]
    k_rope = apply_rope(k_rope, cos, sin)
    mla_out = mla_attention(q_nope, q_rope, k_nope, k_rope, v)
    return jnp.concatenate([s5_out, mla_out], axis=-1) @ w_out


def dsa_attention(q, k, v, q_idx, k_idx, w_idx, top_k):
    bsz, s = q.shape[:2]
    scale = DSA_HEAD_DIM ** -0.5
    rep = DSA_HEADS // DSA_KV_HEADS
    k_pos = jnp.arange(s)
    gather = jax.vmap(lambda a, i: jnp.take(a, i, axis=0))

    def block(args):
        qb, qib, wb, qp = args
        logits = jnp.einsum('bqhd,bsd->bqhs', qib, k_idx)
        score = jnp.einsum('bqhs,bqh->bqs', jax.nn.relu(logits), wb).astype(jnp.float32)
        score = jnp.where(k_pos[None, None, :] <= qp[None, :, None], score, -jnp.inf)
        _, sel = lax.top_k(score, top_k)
        valid = sel <= qp[None, :, None]
        k_sel = gather(k, sel)
        v_sel = gather(v, sel)
        qg = qb.reshape(qb.shape[0], Q_BLOCK, DSA_KV_HEADS, rep, DSA_HEAD_DIM)
        sc = jnp.einsum('bqgrd,bqkgd->bqgrk', qg, k_sel).astype(jnp.float32) * scale
        sc = jnp.where(valid[:, :, None, None, :], sc, -jnp.inf)
        p = jax.nn.softmax(sc, axis=-1).astype(v.dtype)
        o = jnp.einsum('bqgrk,bqkgd->bqgrd', p, v_sel)
        return o.reshape(qb.shape[0], Q_BLOCK, DSA_HEADS * DSA_HEAD_DIM)

    out = lax.map(block, (to_blocks(q), to_blocks(q_idx), to_blocks(w_idx), k_pos.reshape(-1, Q_BLOCK)))
    return from_blocks(out)


def odd_mixer(h, cos, sin, w_in, q_norm_g, w_q_up, w_idx_q, k_ln_g, k_ln_b, w_out):
    bsz, s, _ = h.shape
    kv_w = 2 * DSA_KV_HEADS * DSA_HEAD_DIM
    cuts = [DSA_Q_RANK, DSA_Q_RANK + kv_w, DSA_Q_RANK + kv_w + IDX_DIM]
    q_lat, kv, k_idx, w_idx = jnp.split(h @ w_in, cuts, axis=-1)
    ch, sh = cos[:, :, None, :], sin[:, :, None, :]
    q_lat = rms_norm(q_lat, q_norm_g)
    q = partial_rope((q_lat @ w_q_up).reshape(bsz, s, DSA_HEADS, DSA_HEAD_DIM), ch, sh)
    kv = kv.reshape(bsz, s, 2, DSA_KV_HEADS, DSA_HEAD_DIM)
    k = partial_rope(kv[:, :, 0], ch, sh)
    v = kv[:, :, 1]
    q_idx = partial_rope((q_lat @ w_idx_q).reshape(bsz, s, IDX_HEADS, IDX_DIM), ch, sh)
    k_idx = partial_rope(layer_norm(k_idx, k_ln_g, k_ln_b), cos, sin)
    w_idx = w_idx * (IDX_HEADS ** -0.5 * IDX_DIM ** -0.5)
    top_k = min(IDX_TOPK_MAX, s // 4)
    return dsa_attention(q, k, v, q_idx, k_idx, w_idx, top_k) @ w_out


def sq_relu_mlp(h, w_up, w_down):
    a = jax.nn.relu(h @ w_up)
    return (a * a) @ w_down


def setup_inputs(seed: int = 0) -> dict:
    key = jax.random.key(seed)
    keys = iter(jax.random.split(key, 48))
    f32 = jnp.float32

    def normal(shape, scale):
        return jax.random.normal(next(keys), shape, f32) * scale

    def gain(shape):
        return 1.0 + normal(shape, 0.01)

    ne, no = N_EVEN, N_ODD
    g, n, p = S5_GROUPS, S5_STATE, S5_GROUP
    offset = jax.random.randint(next(keys), (BATCH, 1), 0, 4096, dtype=jnp.int32)
    return {
        'x': normal((BATCH, SEQ, D_MODEL), 1.0),
        'positions': offset + jnp.arange(SEQ, dtype=jnp.int32)[None, :],
        'norm_mix_g': gain((DEPTH, D_MODEL)),
        'norm_mlp_g': gain((DEPTH, D_MODEL)),
        'final_norm_g': gain((D_MODEL,)),
        'even_w_in': normal((ne, D_MODEL, EVEN_IN), D_MODEL ** -0.5),
        's5_lam_re': -0.5 + normal((ne, g, n), 0.01),
        's5_lam_im': math.pi * jnp.arange(n, dtype=f32) + normal((ne, g, n), 0.01),
        's5_log_step': jax.random.uniform(next(keys), (ne, g), f32, math.log(S5_DT_MIN), math.log(S5_DT_MAX)),
        's5_b_re': normal((ne, g, n, p), (2 * p) ** -0.5),
        's5_b_im': normal((ne, g, n, p), (2 * p) ** -0.5),
        's5_c_re': normal((ne, g, p, n), 0.5),
        's5_c_im': normal((ne, g, p, n), 0.5),
        's5_d': normal((ne, S5_WIDTH), 1.0),
        's5_glu_a': normal((ne, S5_WIDTH, S5_WIDTH), S5_WIDTH ** -0.5),
        's5_glu_b': normal((ne, S5_WIDTH, S5_WIDTH), S5_WIDTH ** -0.5),
        'mla_q_norm_g': gain((ne, MLA_Q_RANK)),
        'mla_w_q_up': normal((ne, MLA_Q_RANK, MLA_HEADS * (MLA_NOPE + MLA_ROPE)), MLA_Q_RANK ** -0.5),
        'mla_kv_norm_g': gain((ne, MLA_KV_RANK)),
        'mla_w_kv_up': normal((ne, MLA_KV_RANK, MLA_HEADS * (MLA_NOPE + MLA_V)), MLA_KV_RANK ** -0.5),
        'even_w_out': normal((ne, MIX_WIDTH, D_MODEL), MIX_WIDTH ** -0.5),
        'odd_w_in': normal((no, D_MODEL, ODD_IN), D_MODEL ** -0.5),
        'dsa_q_norm_g': gain((no, DSA_Q_RANK)),
        'dsa_w_q_up': normal((no, DSA_Q_RANK, DSA_HEADS * DSA_HEAD_DIM), DSA_Q_RANK ** -0.5),
        'idx_w_q': normal((no, DSA_Q_RANK, IDX_HEADS * IDX_DIM), DSA_Q_RANK ** -0.5),
        'idx_k_ln_g': gain((no, IDX_DIM)),
        'idx_k_ln_b': normal((no, IDX_DIM), 0.01),
        'odd_w_out': normal((no, DSA_HEADS * DSA_HEAD_DIM, D_MODEL), (DSA_HEADS * DSA_HEAD_DIM) ** -0.5),
        'mlp_w_up': normal((DEPTH, D_MODEL, D_FF), D_MODEL ** -0.5),
        'mlp_w_down': normal((DEPTH, D_FF, D_MODEL), D_FF ** -0.5),
    }


def reference(x, positions, norm_mix_g, norm_mlp_g, final_norm_g,
              even_w_in, s5_lam_re, s5_lam_im, s5_log_step, s5_b_re, s5_b_im, s5_c_re, s5_c_im,
              s5_d, s5_glu_a, s5_glu_b, mla_q_norm_g, mla_w_q_up, mla_kv_norm_g, mla_w_kv_up,
              even_w_out, odd_w_in, dsa_q_norm_g, dsa_w_q_up, idx_w_q, idx_k_ln_g, idx_k_ln_b,
              odd_w_out, mlp_w_up, mlp_w_down):
    cos_m, sin_m = rope_tables(positions, MLA_ROPE)
    cos_p, sin_p = rope_tables(positions, ROT_DIM)
    for layer in range(DEPTH):
        i = layer // 2
        h = rms_norm(x, norm_mix_g[layer])
        if layer % 2 == 0:
            x = x + even_mixer(h, cos_m, sin_m, even_w_in[i], s5_lam_re[i], s5_lam_im[i],
                               s5_log_step[i], s5_b_re[i], s5_b_im[i], s5_c_re[i], s5_c_im[i],
                               s5_d[i], s5_glu_a[i], s5_glu_b[i], mla_q_norm_g[i], mla_w_q_up[i],
                               mla_kv_norm_g[i], mla_w_kv_up[i], even_w_out[i])
        else:
            x = x + odd_mixer(h, cos_p, sin_p, odd_w_in[i], dsa_q_norm_g[i], dsa_w_q_up[i],
                              idx_w_q[i], idx_k_ln_g[i], idx_k_ln_b[i], odd_w_out[i])
        h = rms_norm(x, norm_mlp_g[layer])
        x = x + sq_relu_mlp(h, mlp_w_up[layer], mlp_w_down[layer])
    return rms_norm(x, final_norm_g)
```

```python
import functools
import math

import jax
import jax.numpy as jnp
from jax import lax
from jax.experimental import pallas as pl
from jax.experimental.pallas import tpu as pltpu

F32 = jnp.float32
BF16 = jnp.bfloat16

EPS = 1e-6
ROPE_THETA = 500000.0
S5_GROUP = 16
S5_LAMBDA_RE_MAX = -1e-4
MLA_NOPE = 128
MLA_ROPE = 64
MLA_V = 128
DSA_HEAD_DIM = 128
DSA_KV_HEADS = 4
IDX_DIM = 128
IDX_TOPK_MAX = 256
ROT_DIM = DSA_HEAD_DIM // 4

LANES = 128
SUBLANES = 8
VMEM_LIMIT_BYTES = 56 * 1024 * 1024
MASK_VALUE = -1e30


def _params(semantics):
    return pltpu.CompilerParams(dimension_semantics=semantics, vmem_limit_bytes=VMEM_LIMIT_BYTES)


def _tile(dim, pref, align=LANES):
    if dim <= pref:
        return dim
    t = (pref // align) * align
    while t >= align:
        if dim % t == 0:
            return t
        t -= align
    return dim


def _rmsnorm_kernel(x_ref, g_ref, o_ref):
    x = x_ref[...].astype(F32)
    r = lax.rsqrt(jnp.mean(x * x, axis=-1, keepdims=True) + EPS)
    o_ref[...] = (x * r * g_ref[...]).astype(o_ref.dtype)


def rmsnorm(x, g, out_dtype):
    m, d = x.shape
    tm = _tile(m, 256, SUBLANES)
    return pl.pallas_call(
        _rmsnorm_kernel,
        grid=(m // tm,),
        in_specs=[pl.BlockSpec((tm, d), lambda i: (i, 0)),
                  pl.BlockSpec((1, d), lambda i: (0, 0))],
        out_specs=pl.BlockSpec((tm, d), lambda i: (i, 0)),
        out_shape=jax.ShapeDtypeStruct((m, d), out_dtype),
        compiler_params=_params(("parallel",)),
        name="rmsnorm",
    )(x, g.reshape(1, d).astype(F32))


def _mm_epilogue(acc, kind, res_ref):
    if kind == "relu2":
        a = jnp.maximum(acc, 0.0)
        return a * a
    if kind == "residual":
        return res_ref[...].astype(F32) + acc
    return acc


def _mm_kernel_single(*refs, kind):
    if kind == "residual":
        a_ref, w_ref, res_ref, o_ref = refs
    else:
        a_ref, w_ref, o_ref = refs
        res_ref = None
    acc = jnp.dot(a_ref[...].astype(BF16), w_ref[...], preferred_element_type=F32)
    o_ref[...] = _mm_epilogue(acc, kind, res_ref).astype(o_ref.dtype)


def _mm_kernel_ksplit(*refs, kind, nk):
    if kind == "residual":
        a_ref, w_ref, res_ref, o_ref, acc_ref = refs
    else:
        a_ref, w_ref, o_ref, acc_ref = refs
        res_ref = None
    k = pl.program_id(2)

    @pl.when(k == 0)
    def _():
        acc_ref[...] = jnp.zeros_like(acc_ref)

    acc_ref[...] += jnp.dot(a_ref[...].astype(BF16), w_ref[...], preferred_element_type=F32)

    @pl.when(k == nk - 1)
    def _():
        o_ref[...] = _mm_epilogue(acc_ref[...], kind, res_ref).astype(o_ref.dtype)


def matmul(a, w, out_dtype, kind="none", res=None):
    m, k = a.shape
    _, n = w.shape
    tm = _tile(m, 1024, SUBLANES)
    if k <= 4096:
        tn = _tile(n, 512)
        in_specs = [pl.BlockSpec((tm, k), lambda i, j: (i, 0)),
                    pl.BlockSpec((k, tn), lambda i, j: (0, j))]
        args = [a, w]
        if kind == "residual":
            in_specs.append(pl.BlockSpec((tm, tn), lambda i, j: (i, j)))
            args.append(res)
        return pl.pallas_call(
            functools.partial(_mm_kernel_single, kind=kind),
            grid=(m // tm, n // tn),
            in_specs=in_specs,
            out_specs=pl.BlockSpec((tm, tn), lambda i, j: (i, j)),
            out_shape=jax.ShapeDtypeStruct((m, n), out_dtype),
            compiler_params=_params(("parallel", "parallel")),
            name="matmul",
        )(*args)
    tn = _tile(n, 1024)
    tk = _tile(k, 2048)
    nk = k // tk
    in_specs = [pl.BlockSpec((tm, tk), lambda i, j, kk: (i, kk)),
                pl.BlockSpec((tk, tn), lambda i, j, kk: (kk, j))]
    args = [a, w]
    if kind == "residual":
        in_specs.append(pl.BlockSpec((tm, tn), lambda i, j, kk: (i, j)))
        args.append(res)
    return pl.pallas_call(
        functools.partial(_mm_kernel_ksplit, kind=kind, nk=nk),
        grid=(m // tm, n // tn, nk),
        in_specs=in_specs,
        out_specs=pl.BlockSpec((tm, tn), lambda i, j, kk: (i, j)),
        out_shape=jax.ShapeDtypeStruct((m, n), out_dtype),
        scratch_shapes=[pltpu.VMEM((tm, tn), F32)],
        compiler_params=_params(("parallel", "parallel", "arbitrary")),
        name="matmul_ksplit",
    )(*args)


def _glu_kernel(a_ref, wa_ref, wb_ref, o_ref):
    a = a_ref[...].astype(BF16)
    ya = jnp.dot(a, wa_ref[...], preferred_element_type=F32)
    yb = jnp.dot(a, wb_ref[...], preferred_element_type=F32)
    o_ref[...] = (ya * jax.nn.sigmoid(yb)).astype(o_ref.dtype)


def glu_matmul(a, wa, wb, out_dtype):
    m, k = a.shape
    _, n = wa.shape
    tm = _tile(m, 1024, SUBLANES)
    tn = _tile(n, 512)
    return pl.pallas_call(
        _glu_kernel,
        grid=(m // tm, n // tn),
        in_specs=[pl.BlockSpec((tm, k), lambda i, j: (i, 0)),
                  pl.BlockSpec((k, tn), lambda i, j: (0, j)),
                  pl.BlockSpec((k, tn), lambda i, j: (0, j))],
        out_specs=pl.BlockSpec((tm, tn), lambda i, j: (i, j)),
        out_shape=jax.ShapeDtypeStruct((m, n), out_dtype),
        compiler_params=_params(("parallel", "parallel")),
        name="glu_matmul",
    )(a, wa, wb)


def _rope_tile(x, cos_t, sin_t, half):
    lane = lax.broadcasted_iota(jnp.int32, x.shape, 1)
    first = (lane % (2 * half)) < half
    partner = jnp.where(first, pltpu.roll(x, LANES - half, 1), pltpu.roll(x, half, 1))
    return x * cos_t + partner * sin_t


def _rope_tables(positions, dim, pad_to):
    inv_freq = ROPE_THETA ** (-jnp.arange(0, dim, 2, dtype=F32) / dim)
    ang = positions.astype(F32).reshape(-1)[:, None] * inv_freq
    c, s = jnp.cos(ang), jnp.sin(ang)
    reps = pad_to // dim
    cos_t = jnp.tile(jnp.concatenate([c, c], axis=-1), (1, reps))
    sin_t = jnp.tile(jnp.concatenate([-s, s], axis=-1), (1, reps))
    t = c.shape[0]
    cos_t = jnp.concatenate([cos_t, jnp.ones((t, LANES - pad_to), F32)], axis=-1)
    sin_t = jnp.concatenate([sin_t, jnp.zeros((t, LANES - pad_to), F32)], axis=-1)
    return cos_t, sin_t


def _flash_kernel(*refs, tq, tkb, rep, dv, has_bias):
    if has_bias:
        q_ref, k_ref, v_ref, b_ref, o_ref, m_ref, l_ref, acc_ref = refs
    else:
        q_ref, k_ref, v_ref, o_ref, m_ref, l_ref, acc_ref = refs
        b_ref = None
    i = pl.program_id(2)
    rows = rep * tq

    m_ref[...] = jnp.full(m_ref.shape, MASK_VALUE, F32)
    l_ref[...] = jnp.zeros(l_ref.shape, F32)
    acc_ref[...] = jnp.zeros(acc_ref.shape, F32)

    def step(j, masked):
        start = pl.multiple_of(j * tkb, tkb)
        kb = k_ref[0, 0, pl.ds(start, tkb), :]
        vb = v_ref[0, 0, pl.ds(start, tkb), :]
        s = lax.dot_general(q_ref[0, 0, 0], kb, (((1,), (1,)), ((), ())),
                            preferred_element_type=F32)
        if has_bias:
            bias = b_ref[0, :, pl.ds(start, tkb)].astype(F32)
            s = (s.reshape(rep, tq, tkb) + bias[None]).reshape(rows, tkb)
        if masked:
            qpos = i * tq + lax.broadcasted_iota(jnp.int32, (tq, tkb), 0)
            kpos = start + lax.broadcasted_iota(jnp.int32, (tq, tkb), 1)
            s = jnp.where((kpos <= qpos)[None], s.reshape(rep, tq, tkb), MASK_VALUE).reshape(rows, tkb)
        m_prev = m_ref[...]
        m_new = jnp.maximum(m_prev, jnp.max(s, axis=1, keepdims=True))
        p = jnp.exp(s - m_new)
        alpha = jnp.exp(m_prev - m_new)
        l_ref[...] = alpha * l_ref[...] + jnp.sum(p, axis=1, keepdims=True)
        acc_ref[...] = alpha * acc_ref[...] + jnp.dot(p.astype(BF16), vb, preferred_element_type=F32)
        m_ref[...] = m_new

    q_end = (i + 1) * tq
    if has_bias:
        n_blocks = (q_end + tkb - 1) // tkb
        lax.fori_loop(0, n_blocks, lambda j, c: (step(j, False), c)[1], 0)
    else:
        n_full = (i * tq) // tkb
        n_blocks = (q_end + tkb - 1) // tkb
        lax.fori_loop(0, n_full, lambda j, c: (step(j, False), c)[1], 0)
        lax.fori_loop(n_full, n_blocks, lambda j, c: (step(j, True), c)[1], 0)

    out = acc_ref[...] / l_ref[...]
    for r in range(rep):
        o_ref[0, :, r * dv:(r + 1) * dv] = out[r * tq:(r + 1) * tq].astype(o_ref.dtype)


def flash_attention(q, k, v, bias, *, tq, rep, out_dtype):
    b, hk, nq, rows, dq = q.shape
    s = k.shape[2]
    dv = v.shape[3]
    tkb = _tile(s, 512)
    has_bias = bias is not None
    in_specs = [pl.BlockSpec((1, 1, 1, rows, dq), lambda bb, h, i: (bb, h, i, 0, 0)),
                pl.BlockSpec((1, 1, s, dq), lambda bb, h, i: (bb, h, 0, 0)),
                pl.BlockSpec((1, 1, s, dv), lambda bb, h, i: (bb, h, 0, 0))]
    args = [q, k, v]
    if has_bias:
        in_specs.append(pl.BlockSpec((1, tq, s), lambda bb, h, i: (bb, i, 0)))
        args.append(bias)
    return pl.pallas_call(
        functools.partial(_flash_kernel, tq=tq, tkb=tkb, rep=rep, dv=dv, has_bias=has_bias),
        grid=(b, hk, nq),
        in_specs=in_specs,
        out_specs=pl.BlockSpec((1, tq, rep * dv), lambda bb, h, i: (bb, i, h)),
        out_shape=jax.ShapeDtypeStruct((b, s, hk * rep * dv), out_dtype),
        scratch_shapes=[pltpu.VMEM((rows, 1), F32), pltpu.VMEM((rows, 1), F32),
                        pltpu.VMEM((rows, dv), F32)],
        compiler_params=_params(("parallel", "parallel", "parallel")),
        name="flash_attention",
    )(*args)


S5_NSEG = 2 * SUBLANES
S5_CBLK = LANES
S5_LT = 16


def _s5_kernel(u_ref, bbd_ref, cbd_ref, a_ref, ap_ref, d_ref, y_ref,
               ug_ref, bu_ref, xs_ref, init_ref, *, seg_len, lt):
    nseg = S5_NSEG
    ns = a_ref.shape[-1]
    a_re = jnp.broadcast_to(a_ref[0, 0:1, :], (nseg, ns))
    a_im = jnp.broadcast_to(a_ref[0, 1:2, :], (nseg, ns))
    ntiles = seg_len // lt

    def load_inputs(t):
        for i in range(lt):
            ug_ref[i * nseg:(i + 1) * nseg, :] = u_ref[0, pl.ds(t * lt + i, nseg, stride=seg_len), :]
        bu_ref[...] = jnp.dot(ug_ref[...].astype(BF16), bbd_ref[0], preferred_element_type=F32)

    def scan_tile(x_re, x_im, store):
        for i in range(lt):
            r0 = i * nseg
            b_re = bu_ref[r0:r0 + nseg, 0:ns]
            b_im = bu_ref[r0:r0 + nseg, ns:2 * ns]
            n_re = a_re * x_re - a_im * x_im + b_re
            n_im = a_re * x_im + a_im * x_re + b_im
            x_re, x_im = n_re, n_im
            if store:
                xs_ref[r0:r0 + nseg, 0:ns] = x_re
                xs_ref[r0:r0 + nseg, ns:2 * ns] = x_im
        return x_re, x_im

    def pass1(t, carry):
        load_inputs(t)
        return scan_tile(carry[0], carry[1], False)

    zeros = jnp.zeros((nseg, ns), F32)
    e_re, e_im = lax.fori_loop(0, ntiles, pass1, (zeros, zeros))

    ap_re = ap_ref[0, 0:1, :]
    ap_im = ap_ref[0, 1:2, :]
    t_re = jnp.zeros((1, ns), F32)
    t_im = jnp.zeros((1, ns), F32)
    init_ref[0:1, :] = jnp.zeros((1, 2 * ns), F32)
    for s in range(1, nseg):
        p_re, p_im = e_re[s - 1:s], e_im[s - 1:s]
        t_re, t_im = (ap_re * t_re - ap_im * t_im + p_re,
                      ap_re * t_im + ap_im * t_re + p_im)
        init_ref[s:s + 1, 0:ns] = t_re
        init_ref[s:s + 1, ns:2 * ns] = t_im

    def pass2(t, carry):
        load_inputs(t)
        x_re, x_im = scan_tile(carry[0], carry[1], True)
        y = jnp.dot(xs_ref[...].astype(BF16), cbd_ref[0], preferred_element_type=F32)
        y = jax.nn.gelu(y + d_ref[0] * ug_ref[...])
        for i in range(lt):
            y_ref[0, pl.ds(t * lt + i, nseg, stride=seg_len), :] = y[i * nseg:(i + 1) * nseg]
        return x_re, x_im

    lax.fori_loop(0, ntiles, pass2, (init_ref[:, 0:ns], init_ref[:, ns:2 * ns]))


def _s5_discretise(lam_re, lam_im, log_step, b_re, b_im, c_re, c_im, seg_len):
    g, n = lam_re.shape
    p = S5_GROUP
    lr = jnp.minimum(lam_re.astype(F32), S5_LAMBDA_RE_MAX)
    li = lam_im.astype(F32)
    step = jnp.exp(log_step.astype(F32))[:, None]
    mag = jnp.exp(lr * step)
    ab_re = mag * jnp.cos(li * step)
    ab_im = mag * jnp.sin(li * step)
    den = lr * lr + li * li
    f_re = ((ab_re - 1.0) * lr + ab_im * li) / den
    f_im = (ab_im * lr - (ab_re - 1.0) * li) / den
    br = b_re.astype(F32)
    bi = b_im.astype(F32)
    bb_re = f_re[..., None] * br - f_im[..., None] * bi
    bb_im = f_re[..., None] * bi + f_im[..., None] * br
    pw_re, pw_im = jnp.ones_like(ab_re), jnp.zeros_like(ab_im)
    sq_re, sq_im = ab_re, ab_im
    e = seg_len
    while e:
        if e & 1:
            pw_re, pw_im = pw_re * sq_re - pw_im * sq_im, pw_re * sq_im + pw_im * sq_re
        sq_re, sq_im = sq_re * sq_re - sq_im * sq_im, 2.0 * sq_re * sq_im
        e >>= 1
    gb = S5_CBLK // p
    nb = g // gb
    eye = jnp.eye(gb, dtype=F32)

    def blockdiag_in(bb):
        x = bb.reshape(nb, gb, n, p)
        return jnp.einsum("bgnp,gh->bgphn", x, eye).reshape(nb, gb * p, gb * n)

    def blockdiag_out(c):
        x = c.astype(F32).reshape(nb, gb, p, n)
        return jnp.einsum("bgpn,gh->bgnhp", x, eye).reshape(nb, gb * n, gb * p)

    bbd = jnp.concatenate([blockdiag_in(bb_re), blockdiag_in(bb_im)], axis=2).astype(BF16)
    cbd = jnp.concatenate([blockdiag_out(c_re), -blockdiag_out(c_im)], axis=1).astype(BF16)
    a = jnp.stack([ab_re.reshape(nb, gb * n), ab_im.reshape(nb, gb * n)], axis=1)
    ap = jnp.stack([pw_re.reshape(nb, gb * n), pw_im.reshape(nb, gb * n)], axis=1)
    return bbd, cbd, a, ap


def s5_scan(u, lam_re, lam_im, log_step, b_re, b_im, c_re, c_im, d_skip):
    b, s, w = u.shape
    seg_len = s // S5_NSEG
    lt = min(S5_LT, seg_len)
    bbd, cbd, a, ap = _s5_discretise(lam_re, lam_im, log_step, b_re, b_im, c_re, c_im, seg_len)
    nb, cw, ns2 = bbd.shape
    ns = ns2 // 2
    rows = lt * S5_NSEG
    return pl.pallas_call(
        functools.partial(_s5_kernel, seg_len=seg_len, lt=lt),
        grid=(b, nb),
        in_specs=[pl.BlockSpec((1, s, cw), lambda bb, j: (bb, 0, j)),
                  pl.BlockSpec((1, cw, ns2), lambda bb, j: (j, 0, 0)),
                  pl.BlockSpec((1, ns2, cw), lambda bb, j: (j, 0, 0)),
                  pl.BlockSpec((1, 2, ns), lambda bb, j: (j, 0, 0)),
                  pl.BlockSpec((1, 2, ns), lambda bb, j: (j, 0, 0)),
                  pl.BlockSpec((1, 1, cw), lambda bb, j: (j, 0, 0))],
        out_specs=pl.BlockSpec((1, s, cw), lambda bb, j: (bb, 0, j)),
        out_shape=jax.ShapeDtypeStruct((b, s, w), F32),
        scratch_shapes=[pltpu.VMEM((rows, cw), F32), pltpu.VMEM((rows, ns2), F32),
                        pltpu.VMEM((rows, ns2), F32), pltpu.VMEM((S5_NSEG, ns2), F32)],
        compiler_params=_params(("parallel", "parallel")),
        name="s5_scan",
    )(u, bbd, cbd, a, ap, d_skip.astype(F32).reshape(nb, 1, cw))


def _even_lat_kernel(lat_ref, gq_ref, gkv_ref, cos_ref, sin_ref, qn_ref, kvn_ref, kr_ref, *, q_rank, kv_rank):
    lat = lat_ref[...]
    q_lat = lat[:, 0:q_rank]
    kv_lat = lat[:, q_rank:q_rank + kv_rank]
    kr = lat[:, q_rank + kv_rank:q_rank + kv_rank + LANES]
    rq = lax.rsqrt(jnp.mean(q_lat * q_lat, axis=-1, keepdims=True) + EPS)
    qn_ref[...] = (q_lat * rq * gq_ref[...]).astype(qn_ref.dtype)
    rkv = lax.rsqrt(jnp.mean(kv_lat * kv_lat, axis=-1, keepdims=True) + EPS)
    kvn_ref[...] = (kv_lat * rkv * gkv_ref[...]).astype(kvn_ref.dtype)
    kr = _rope_tile(kr, cos_ref[...], sin_ref[...], MLA_ROPE // 2)
    lane = lax.broadcasted_iota(jnp.int32, kr.shape, 1)
    lo = jnp.where(lane < MLA_ROPE, kr, 0.0)
    hi = pltpu.roll(lo, MLA_ROPE, 1)
    kr_ref[:, 0:LANES] = lo.astype(kr_ref.dtype)
    kr_ref[:, LANES:2 * LANES] = hi.astype(kr_ref.dtype)


def even_lat_prep(lat, gq, gkv, cos_t, sin_t):
    m, wl = lat.shape
    q_rank, kv_rank = gq.shape[0], gkv.shape[0]
    tm = _tile(m, 512, SUBLANES)
    return pl.pallas_call(
        functools.partial(_even_lat_kernel, q_rank=q_rank, kv_rank=kv_rank),
        grid=(m // tm,),
        in_specs=[pl.BlockSpec((tm, wl), lambda i: (i, 0)),
                  pl.BlockSpec((1, q_rank), lambda i: (0, 0)),
                  pl.BlockSpec((1, kv_rank), lambda i: (0, 0)),
                  pl.BlockSpec((tm, LANES), lambda i: (i, 0)),
                  pl.BlockSpec((tm, LANES), lambda i: (i, 0))],
        out_specs=[pl.BlockSpec((tm, q_rank), lambda i: (i, 0)),
                   pl.BlockSpec((tm, kv_rank), lambda i: (i, 0)),
                   pl.BlockSpec((tm, 2 * LANES), lambda i: (i, 0))],
        out_shape=[jax.ShapeDtypeStruct((m, q_rank), BF16),
                   jax.ShapeDtypeStruct((m, kv_rank), BF16),
                   jax.ShapeDtypeStruct((m, 2 * LANES), BF16)],
        compiler_params=_params(("parallel",)),
        name="even_lat_prep",
    )(lat, gq.reshape(1, -1).astype(F32), gkv.reshape(1, -1).astype(F32), cos_t, sin_t)


def _mla_pack_kernel(q_ref, kv_ref, kr_ref, cos_ref, sin_ref, qf_ref, kf_ref, vf_ref, *, heads, scale):
    cos_t, sin_t = cos_ref[...], sin_ref[...]
    nope_w = heads * MLA_NOPE
    lane = lax.broadcasted_iota(jnp.int32, (q_ref.shape[1], LANES), 1)
    for pair in range(heads // 2):
        rope = _rope_tile(q_ref[0, :, nope_w + pair * LANES:nope_w + (pair + 1) * LANES],
                          cos_t, sin_t, MLA_ROPE // 2)
        for sub in range(2):
            h = 2 * pair + sub
            keep = (lane < MLA_ROPE) if sub == 0 else (lane >= MLA_ROPE)
            qf_ref[0, h, :, 0:LANES] = (q_ref[0, :, h * MLA_NOPE:(h + 1) * MLA_NOPE] * scale).astype(qf_ref.dtype)
            qf_ref[0, h, :, LANES:2 * LANES] = (jnp.where(keep, rope, 0.0) * scale).astype(qf_ref.dtype)
            kf_ref[0, h, :, 0:LANES] = kv_ref[0, :, h * 2 * LANES:h * 2 * LANES + LANES]
            kf_ref[0, h, :, LANES:2 * LANES] = kr_ref[0, :, sub * LANES:(sub + 1) * LANES]
            vf_ref[0, h] = kv_ref[0, :, h * 2 * LANES + LANES:(h + 1) * 2 * LANES]


def mla_pack(q, kv, kr2, cos_t, sin_t, heads):
    b, s, _ = q.shape
    tm = _tile(s, 256, SUBLANES)
    scale = (MLA_NOPE + MLA_ROPE) ** -0.5
    return pl.pallas_call(
        functools.partial(_mla_pack_kernel, heads=heads, scale=scale),
        grid=(b, s // tm),
        in_specs=[pl.BlockSpec((1, tm, q.shape[2]), lambda bb, i: (bb, i, 0)),
                  pl.BlockSpec((1, tm, kv.shape[2]), lambda bb, i: (bb, i, 0)),
                  pl.BlockSpec((1, tm, 2 * LANES), lambda bb, i: (bb, i, 0)),
                  pl.BlockSpec((tm, LANES), lambda bb, i, nt=s // tm: (bb * nt + i, 0)),
                  pl.BlockSpec((tm, LANES), lambda bb, i, nt=s // tm: (bb * nt + i, 0))],
        out_specs=[pl.BlockSpec((1, heads, tm, 2 * LANES), lambda bb, i: (bb, 0, i, 0)),
                   pl.BlockSpec((1, heads, tm, 2 * LANES), lambda bb, i: (bb, 0, i, 0)),
                   pl.BlockSpec((1, heads, tm, MLA_V), lambda bb, i: (bb, 0, i, 0))],
        out_shape=[jax.ShapeDtypeStruct((b, heads, s, 2 * LANES), BF16),
                   jax.ShapeDtypeStruct((b, heads, s, 2 * LANES), BF16),
                   jax.ShapeDtypeStruct((b, heads, s, MLA_V), BF16)],
        compiler_params=_params(("parallel", "parallel")),
        name="mla_pack",
    )(q, kv, kr2, cos_t, sin_t)


def even_mixer(x2, h, b, s, cos_m, sin_m, w_in, lam_re, lam_im, log_step, b_re, b_im, c_re, c_im,
               d_skip, glu_a, glu_b, q_norm_g, w_q_up, kv_norm_g, w_kv_up, w_out):
    t, d = h.shape
    s5_w = d_skip.shape[0]
    q_rank, kv_rank = q_norm_g.shape[0], kv_norm_g.shape[0]
    heads = w_kv_up.shape[1] // (MLA_NOPE + MLA_V)
    lat_w = q_rank + kv_rank + MLA_ROPE
    lat_pad = -(-(q_rank + kv_rank + LANES) // (2 * LANES)) * (2 * LANES)

    w_u = w_in[:, :s5_w].astype(BF16)
    w_lat = jnp.pad(w_in[:, s5_w:], ((0, 0), (0, lat_pad - lat_w))).astype(BF16)
    u = matmul(h, w_u, F32)
    lat = matmul(h, w_lat, F32)

    y = s5_scan(u.reshape(b, s, s5_w), lam_re, lam_im, log_step, b_re, b_im, c_re, c_im, d_skip)
    s5_out = glu_matmul(y.reshape(t, s5_w), glu_a.astype(BF16), glu_b.astype(BF16), BF16)

    qn, kvn, kr2 = even_lat_prep(lat, q_norm_g, kv_norm_g, cos_m, sin_m)
    wq = w_q_up.reshape(q_rank, heads, MLA_NOPE + MLA_ROPE)
    wq = jnp.concatenate([wq[:, :, :MLA_NOPE].reshape(q_rank, heads * MLA_NOPE),
                          wq[:, :, MLA_NOPE:].reshape(q_rank, heads * MLA_ROPE)], axis=1).astype(BF16)
    q = matmul(qn, wq, F32)
    kv = matmul(kvn, w_kv_up.astype(BF16), BF16)
    qf, kf, vf = mla_pack(q.reshape(b, s, -1), kv.reshape(b, s, -1), kr2.reshape(b, s, -1), cos_m, sin_m, heads)
    tq = _tile(s, 512)
    mla_out = flash_attention(qf.reshape(b, heads, s // tq, tq, 2 * LANES), kf, vf, None,
                              tq=tq, rep=1, out_dtype=BF16)
    mixed = jnp.concatenate([s5_out, mla_out.reshape(t, -1)], axis=-1)
    return matmul(mixed, w_out.astype(BF16), F32, kind="residual", res=x2)


def _odd_lat_kernel(z_ref, gq_ref, lng_ref, lnb_ref, cos_ref, sin_ref,
                    qn_ref, k_ref, v_ref, kidx_ref, w_ref, *, q_rank, w_scale):
    cos_t, sin_t = cos_ref[...], sin_ref[...]
    half = ROT_DIM // 2
    z = z_ref[0]
    q_lat = z[:, 0:q_rank]
    rq = lax.rsqrt(jnp.mean(q_lat * q_lat, axis=-1, keepdims=True) + EPS)
    qn_ref[0] = (q_lat * rq * gq_ref[...]).astype(qn_ref.dtype)
    kvh = DSA_KV_HEADS
    d = DSA_HEAD_DIM
    for g in range(kvh):
        kh = z[:, q_rank + g * d:q_rank + (g + 1) * d]
        k_ref[0, g] = _rope_tile(kh, cos_t, sin_t, half).astype(k_ref.dtype)
        v_ref[0, g] = z[:, q_rank + (kvh + g) * d:q_rank + (kvh + g + 1) * d].astype(v_ref.dtype)
    off = q_rank + 2 * kvh * d
    ki = z[:, off:off + IDX_DIM]
    kc = ki - jnp.mean(ki, axis=-1, keepdims=True)
    var = jnp.mean(kc * kc, axis=-1, keepdims=True)
    ki = kc * lax.rsqrt(var + EPS) * lng_ref[...] + lnb_ref[...]
    kidx_ref[0] = _rope_tile(ki, cos_t, sin_t, half).astype(kidx_ref.dtype)
    w_ref[0] = z[:, off + IDX_DIM:off + IDX_DIM + LANES] * w_scale


def odd_lat_prep(z, gq, ln_g, ln_b, cos_t, sin_t, idx_heads):
    b, s, zw = z.shape
    q_rank = gq.shape[0]
    tm = _tile(s, 512, SUBLANES)
    nt = s // tm
    w_scale = idx_heads ** -0.5 * IDX_DIM ** -0.5
    kvh, d = DSA_KV_HEADS, DSA_HEAD_DIM
    return pl.pallas_call(
        functools.partial(_odd_lat_kernel, q_rank=q_rank, w_scale=w_scale),
        grid=(b, nt),
        in_specs=[pl.BlockSpec((1, tm, zw), lambda bb, i: (bb, i, 0)),
                  pl.BlockSpec((1, q_rank), lambda bb, i: (0, 0)),
                  pl.BlockSpec((1, IDX_DIM), lambda bb, i: (0, 0)),
                  pl.BlockSpec((1, IDX_DIM), lambda bb, i: (0, 0)),
                  pl.BlockSpec((tm, LANES), lambda bb, i: (bb * nt + i, 0)),
                  pl.BlockSpec((tm, LANES), lambda bb, i: (bb * nt + i, 0))],
        out_specs=[pl.BlockSpec((1, tm, q_rank), lambda bb, i: (bb, i, 0)),
                   pl.BlockSpec((1, kvh, tm, d), lambda bb, i: (bb, 0, i, 0)),
                   pl.BlockSpec((1, kvh, tm, d), lambda bb, i: (bb, 0, i, 0)),
                   pl.BlockSpec((1, tm, IDX_DIM), lambda bb, i: (bb, i, 0)),
                   pl.BlockSpec((1, tm, LANES), lambda bb, i: (bb, i, 0))],
        out_shape=[jax.ShapeDtypeStruct((b, s, q_rank), BF16),
                   jax.ShapeDtypeStruct((b, kvh, s, d), BF16),
                   jax.ShapeDtypeStruct((b, kvh, s, d), BF16),
                   jax.ShapeDtypeStruct((b, s, IDX_DIM), BF16),
                   jax.ShapeDtypeStruct((b, s, LANES), F32)],
        compiler_params=_params(("parallel", "parallel")),
        name="odd_lat_prep",
    )(z, gq.reshape(1, -1).astype(F32), ln_g.reshape(1, -1).astype(F32),
      ln_b.reshape(1, -1).astype(F32), cos_t, sin_t)


def _dsa_q_kernel(qq_ref, cos_ref, sin_ref, q_ref, qi_ref, *, heads, idx_heads, rep, tq, scale):
    cos_t, sin_t = cos_ref[...], sin_ref[...]
    half = ROT_DIM // 2
    d = DSA_HEAD_DIM
    for h in range(heads):
        g, r = divmod(h, rep)
        xh = _rope_tile(qq_ref[0, :, h * d:(h + 1) * d], cos_t, sin_t, half)
        q_ref[0, g, 0, r * tq:(r + 1) * tq, :] = (xh * scale).astype(q_ref.dtype)
    off = heads * d
    for h in range(idx_heads):
        xh = _rope_tile(qq_ref[0, :, off + h * IDX_DIM:off + (h + 1) * IDX_DIM], cos_t, sin_t, half)
        qi_ref[0, :, h * IDX_DIM:(h + 1) * IDX_DIM] = xh.astype(qi_ref.dtype)


def dsa_q_prep(qq, cos_t, sin_t, heads, idx_heads, tq):
    b, s, wq = qq.shape
    nt = s // tq
    rep = heads // DSA_KV_HEADS
    d = DSA_HEAD_DIM
    scale = DSA_HEAD_DIM ** -0.5
    return pl.pallas_call(
        functools.partial(_dsa_q_kernel, heads=heads, idx_heads=idx_heads, rep=rep, tq=tq, scale=scale),
        grid=(b, nt),
        in_specs=[pl.BlockSpec((1, tq, wq), lambda bb, i: (bb, i, 0)),
                  pl.BlockSpec((tq, LANES), lambda bb, i: (bb * nt + i, 0)),
                  pl.BlockSpec((tq, LANES), lambda bb, i: (bb * nt + i, 0))],
        out_specs=[pl.BlockSpec((1, DSA_KV_HEADS, 1, rep * tq, d), lambda bb, i: (bb, 0, i, 0, 0)),
                   pl.BlockSpec((1, tq, idx_heads * IDX_DIM), lambda bb, i: (bb, i, 0))],
        out_shape=[jax.ShapeDtypeStruct((b, DSA_KV_HEADS, nt, rep * tq, d), BF16),
                   jax.ShapeDtypeStruct((b, s, idx_heads * IDX_DIM), BF16)],
        compiler_params=_params(("parallel", "parallel")),
        name="dsa_q_prep",
    )(qq, cos_t, sin_t)


def _ukey_to_float(u):
    bits = jnp.where(u < 0, u & jnp.int32(0x7FFFFFFF), ~u)
    return lax.bitcast_convert_type(bits, F32)


def _indexer_kernel(qi_ref, w_ref, kidx_ref, bias_ref, score_ref, *, tq, tkb, idx_heads, top_k):
    i = pl.program_id(1)
    s_len = kidx_ref.shape[1]
    n_blocks = ((i + 1) * tq + tkb - 1) // tkb
    neg_inf = jnp.float32(-jnp.inf)

    def score_block(j, c):
        start = pl.multiple_of(j * tkb, tkb)
        kb = kidx_ref[0, pl.ds(start, tkb), :]
        acc = jnp.zeros((tq, tkb), F32)
        for h in range(idx_heads):
            logits = lax.dot_general(qi_ref[0, :, h * IDX_DIM:(h + 1) * IDX_DIM], kb,
                                     (((1,), (1,)), ((), ())), preferred_element_type=F32)
            acc = acc + jnp.maximum(logits, 0.0) * w_ref[0, :, h:h + 1]
        qpos = i * tq + lax.broadcasted_iota(jnp.int32, (tq, tkb), 0)
        kpos = start + lax.broadcasted_iota(jnp.int32, (tq, tkb), 1)
        score_ref[:, pl.ds(start, tkb)] = jnp.where(kpos <= qpos, acc, neg_inf)
        return c

    lax.fori_loop(0, n_blocks, score_block, 0)

    def count_ge(thr):
        def body(j, cnt):
            start = pl.multiple_of(j * tkb, tkb)
            blk = score_ref[:, pl.ds(start, tkb)]
            return cnt + jnp.sum(jnp.where(blk >= thr, 1.0, 0.0), axis=1, keepdims=True)
        return lax.fori_loop(0, n_blocks, body, jnp.zeros((tq, 1), F32))

    def bit_step(bi, u):
        bit = jnp.int32(1) << (31 - bi)
        cand = u | bit
        thr = _ukey_to_float(cand)
        thr = jnp.where(thr != thr, jnp.where(cand < 0, thr, neg_inf), thr)
        return jnp.where(count_ge(thr) >= top_k, cand, u)

    u = lax.fori_loop(0, 32, bit_step, jnp.zeros((tq, 1), jnp.int32))
    thr = _ukey_to_float(u)

    def write_block(j, c):
        start = pl.multiple_of(j * tkb, tkb)
        blk = score_ref[:, pl.ds(start, tkb)]
        qpos = i * tq + lax.broadcasted_iota(jnp.int32, (tq, tkb), 0)
        kpos = start + lax.broadcasted_iota(jnp.int32, (tq, tkb), 1)
        keep = (blk >= thr) & (kpos <= qpos)
        bias_ref[0, :, pl.ds(start, tkb)] = jnp.where(keep, 0.0, MASK_VALUE).astype(bias_ref.dtype)
        return c

    lax.fori_loop(0, n_blocks, write_block, 0)

    def fill_block(j, c):
        start = pl.multiple_of(j * tkb, tkb)
        bias_ref[0, :, pl.ds(start, tkb)] = jnp.full((tq, tkb), MASK_VALUE, bias_ref.dtype)
        return c

    lax.fori_loop(n_blocks, s_len // tkb, fill_block, 0)


def dsa_indexer(qi, w, kidx, idx_heads, top_k, tq):
    b, s, _ = qi.shape
    tkb = _tile(s, 512)
    assert tkb >= top_k or tkb == s
    return pl.pallas_call(
        functools.partial(_indexer_kernel, tq=tq, tkb=tkb, idx_heads=idx_heads, top_k=top_k),
        grid=(b, s // tq),
        in_specs=[pl.BlockSpec((1, tq, idx_heads * IDX_DIM), lambda bb, i: (bb, i, 0)),
                  pl.BlockSpec((1, tq, LANES), lambda bb, i: (bb, i, 0)),
                  pl.BlockSpec((1, s, IDX_DIM), lambda bb, i: (bb, 0, 0))],
        out_specs=pl.BlockSpec((1, tq, s), lambda bb, i: (bb, i, 0)),
        out_shape=jax.ShapeDtypeStruct((b, s, s), BF16),
        scratch_shapes=[pltpu.VMEM((tq, s), F32)],
        compiler_params=_params(("parallel", "parallel")),
        name="dsa_indexer",
    )(qi, w, kidx)


def odd_mixer(x2, h, b, s, cos_p, sin_p, w_in, q_norm_g, w_q_up, w_idx_q, k_ln_g, k_ln_b, w_out):
    t, d = h.shape
    q_rank = q_norm_g.shape[0]
    heads = w_q_up.shape[1] // DSA_HEAD_DIM
    idx_heads = w_idx_q.shape[1] // IDX_DIM
    in_w = w_in.shape[1]
    z_w = -(-(in_w - idx_heads + LANES) // (2 * LANES)) * (2 * LANES)
    w_z = jnp.pad(w_in, ((0, 0), (0, z_w - in_w))).astype(BF16)
    z = matmul(h, w_z, F32)
    qn, k, v, kidx, w = odd_lat_prep(z.reshape(b, s, z_w), q_norm_g, k_ln_g, k_ln_b, cos_p, sin_p, idx_heads)
    w_qq = jnp.concatenate([w_q_up, w_idx_q], axis=1).astype(BF16)
    qq = matmul(qn.reshape(t, q_rank), w_qq, F32)
    tq = _tile(s, 128)
    q, qi = dsa_q_prep(qq.reshape(b, s, -1), cos_p, sin_p, heads, idx_heads, tq)
    top_k = min(IDX_TOPK_MAX, s // 4)
    bias = dsa_indexer(qi, w, kidx, idx_heads, top_k, tq)
    o = flash_attention(q, k, v, bias, tq=tq, rep=heads // DSA_KV_HEADS, out_dtype=BF16)
    return matmul(o.reshape(t, -1), w_out.astype(BF16), F32, kind="residual", res=x2)


def sq_relu_mlp(x2, h, w_up, w_down):
    a = matmul(h, w_up.astype(BF16), BF16, kind="relu2")
    return matmul(a, w_down.astype(BF16), F32, kind="residual", res=x2)


def kernel(x, positions, norm_mix_g, norm_mlp_g, final_norm_g, even_w_in, s5_lam_re, s5_lam_im, s5_log_step, s5_b_re, s5_b_im, s5_c_re, s5_c_im, s5_d, s5_glu_a, s5_glu_b, mla_q_norm_g, mla_w_q_up, mla_kv_norm_g, mla_w_kv_up, even_w_out, odd_w_in, dsa_q_norm_g, dsa_w_q_up, idx_w_q, idx_k_ln_g, idx_k_ln_b, odd_w_out, mlp_w_up, mlp_w_down):
    b, s, d = x.shape
    depth = norm_mix_g.shape[0]
    cos_m, sin_m = _rope_tables(positions, MLA_ROPE, LANES)
    cos_p, sin_p = _rope_tables(positions, ROT_DIM, ROT_DIM)
    x2 = x.reshape(b * s, d)
    for layer in range(depth):
        i = layer // 2
        h = rmsnorm(x2, norm_mix_g[layer], BF16)
        if layer % 2 == 0:
            x2 = even_mixer(x2, h, b, s, cos_m, sin_m, even_w_in[i], s5_lam_re[i], s5_lam_im[i],
                            s5_log_step[i], s5_b_re[i], s5_b_im[i], s5_c_re[i], s5_c_im[i], s5_d[i],
                            s5_glu_a[i], s5_glu_b[i], mla_q_norm_g[i], mla_w_q_up[i],
                            mla_kv_norm_g[i], mla_w_kv_up[i], even_w_out[i])
        else:
            x2 = odd_mixer(x2, h, b, s, cos_p, sin_p, odd_w_in[i], dsa_q_norm_g[i], dsa_w_q_up[i],
                           idx_w_q[i], idx_k_ln_g[i], idx_k_ln_b[i], odd_w_out[i])
        h = rmsnorm(x2, norm_mlp_g[layer], BF16)
        x2 = sq_relu_mlp(x2, h, mlp_w_up[layer], mlp_w_down[layer])
    return rmsnorm(x2, final_norm_g, x.dtype).reshape(b, s, d)
```

```python
import functools
import math

import jax
import jax.numpy as jnp
from jax import lax
from jax.experimental import pallas as pl
from jax.experimental.pallas import tpu as pltpu

F32 = jnp.float32
BF16 = jnp.bfloat16

EPS = 1e-6
ROPE_THETA = 500000.0
S5_GROUP = 16
S5_LAMBDA_RE_MAX = -1e-4
MLA_NOPE = 128
MLA_ROPE = 64
MLA_V = 128
DSA_HEAD_DIM = 128
DSA_KV_HEADS = 4
IDX_DIM = 128
IDX_TOPK_MAX = 256
ROT_DIM = DSA_HEAD_DIM // 4

LANES = 128
SUBLANES = 8
VMEM_LIMIT_BYTES = 56 * 1024 * 1024
MASK_VALUE = -1e30


def _params(semantics):
    return pltpu.CompilerParams(dimension_semantics=semantics, vmem_limit_bytes=VMEM_LIMIT_BYTES)


def _tile(dim, pref, align=LANES):
    if dim <= pref:
        return dim
    t = (pref // align) * align
    while t >= align:
        if dim % t == 0:
            return t
        t -= align
    return dim


def _rmsnorm_kernel(x_ref, g_ref, o_ref):
    x = x_ref[...].astype(F32)
    r = lax.rsqrt(jnp.mean(x * x, axis=-1, keepdims=True) + EPS)
    o_ref[...] = (x * r * g_ref[...]).astype(o_ref.dtype)


def rmsnorm(x, g, out_dtype):
    m, d = x.shape
    tm = _tile(m, 256, SUBLANES)
    return pl.pallas_call(
        _rmsnorm_kernel,
        grid=(m // tm,),
        in_specs=[pl.BlockSpec((tm, d), lambda i: (i, 0)),
                  pl.BlockSpec((1, d), lambda i: (0, 0))],
        out_specs=pl.BlockSpec((tm, d), lambda i: (i, 0)),
        out_shape=jax.ShapeDtypeStruct((m, d), out_dtype),
        compiler_params=_params(("parallel",)),
        name="rmsnorm",
    )(x, g.reshape(1, d).astype(F32))


def _mm_epilogue(acc, kind, res_ref):
    if kind == "relu2":
        a = jnp.maximum(acc, 0.0)
        return a * a
    if kind == "residual":
        return res_ref[...].astype(F32) + acc
    return acc


def _mm_kernel_single(*refs, kind):
    if kind == "residual":
        a_ref, w_ref, res_ref, o_ref = refs
    else:
        a_ref, w_ref, o_ref = refs
        res_ref = None
    acc = jnp.dot(a_ref[...].astype(BF16), w_ref[...].astype(BF16), preferred_element_type=F32)
    o_ref[...] = _mm_epilogue(acc, kind, res_ref).astype(o_ref.dtype)


def _mm_kernel_ksplit(*refs, kind, nk):
    if kind == "residual":
        a_ref, w_ref, res_ref, o_ref, acc_ref = refs
    else:
        a_ref, w_ref, o_ref, acc_ref = refs
        res_ref = None
    k = pl.program_id(2)

    @pl.when(k == 0)
    def _():
        acc_ref[...] = jnp.zeros_like(acc_ref)

    acc_ref[...] += jnp.dot(a_ref[...].astype(BF16), w_ref[...].astype(BF16), preferred_element_type=F32)

    @pl.when(k == nk - 1)
    def _():
        o_ref[...] = _mm_epilogue(acc_ref[...], kind, res_ref).astype(o_ref.dtype)


def matmul(a, w, out_dtype, kind="none", res=None):
    m, k = a.shape
    _, n = w.shape
    tm = _tile(m, 1024, SUBLANES)
    if k <= 4096:
        tn = _tile(n, 512)
        in_specs = [pl.BlockSpec((tm, k), lambda i, j: (i, 0)),
                    pl.BlockSpec((k, tn), lambda i, j: (0, j))]
        args = [a, w]
        if kind == "residual":
            in_specs.append(pl.BlockSpec((tm, tn), lambda i, j: (i, j)))
            args.append(res)
        return pl.pallas_call(
            functools.partial(_mm_kernel_single, kind=kind),
            grid=(m // tm, n // tn),
            in_specs=in_specs,
            out_specs=pl.BlockSpec((tm, tn), lambda i, j: (i, j)),
            out_shape=jax.ShapeDtypeStruct((m, n), out_dtype),
            compiler_params=_params(("parallel", "parallel")),
            name="matmul",
        )(*args)
    tn = _tile(n, 1024)
    tk = _tile(k, 2048)
    nk = k // tk
    in_specs = [pl.BlockSpec((tm, tk), lambda i, j, kk: (i, kk)),
                pl.BlockSpec((tk, tn), lambda i, j, kk: (kk, j))]
    args = [a, w]
    if kind == "residual":
        in_specs.append(pl.BlockSpec((tm, tn), lambda i, j, kk: (i, j)))
        args.append(res)
    return pl.pallas_call(
        functools.partial(_mm_kernel_ksplit, kind=kind, nk=nk),
        grid=(m // tm, n // tn, nk),
        in_specs=in_specs,
        out_specs=pl.BlockSpec((tm, tn), lambda i, j, kk: (i, j)),
        out_shape=jax.ShapeDtypeStruct((m, n), out_dtype),
        scratch_shapes=[pltpu.VMEM((tm, tn), F32)],
        compiler_params=_params(("parallel", "parallel", "arbitrary")),
        name="matmul_ksplit",
    )(*args)


def _glu_kernel(a_ref, wa_ref, wb_ref, o_ref):
    a = a_ref[...].astype(BF16)
    ya = jnp.dot(a, wa_ref[...], preferred_element_type=F32)
    yb = jnp.dot(a, wb_ref[...], preferred_element_type=F32)
    o_ref[...] = (ya * jax.nn.sigmoid(yb)).astype(o_ref.dtype)


def glu_matmul(a, wa, wb, out_dtype):
    m, k = a.shape
    _, n = wa.shape
    tm = _tile(m, 1024, SUBLANES)
    tn = _tile(n, 512)
    return pl.pallas_call(
        _glu_kernel,
        grid=(m // tm, n // tn),
        in_specs=[pl.BlockSpec((tm, k), lambda i, j: (i, 0)),
                  pl.BlockSpec((k, tn), lambda i, j: (0, j)),
                  pl.BlockSpec((k, tn), lambda i, j: (0, j))],
        out_specs=pl.BlockSpec((tm, tn), lambda i, j: (i, j)),
        out_shape=jax.ShapeDtypeStruct((m, n), out_dtype),
        compiler_params=_params(("parallel", "parallel")),
        name="glu_matmul",
    )(a, wa, wb)


def _rope_tile(x, cos_t, sin_t, half):
    lane = lax.broadcasted_iota(jnp.int32, x.shape, 1)
    first = (lane % (2 * half)) < half
    partner = jnp.where(first, pltpu.roll(x, LANES - half, 1), pltpu.roll(x, half, 1))
    return x * cos_t + partner * sin_t


def _rope_tables(positions, dim, pad_to):
    inv_freq = ROPE_THETA ** (-jnp.arange(0, dim, 2, dtype=F32) / dim)
    ang = positions.astype(F32).reshape(-1)[:, None] * inv_freq
    c, s = jnp.cos(ang), jnp.sin(ang)
    reps = pad_to // dim
    cos_t = jnp.tile(jnp.concatenate([c, c], axis=-1), (1, reps))
    sin_t = jnp.tile(jnp.concatenate([-s, s], axis=-1), (1, reps))
    t = c.shape[0]
    cos_t = jnp.concatenate([cos_t, jnp.ones((t, LANES - pad_to), F32)], axis=-1)
    sin_t = jnp.concatenate([sin_t, jnp.zeros((t, LANES - pad_to), F32)], axis=-1)
    return cos_t, sin_t


V_ONES_ROWS = 16
FLASH_TKB = 512
LOG2E = math.log2(math.e)


def _flash_kernel(*refs, tq, tkb, rep, dv, has_bias):
    if has_bias:
        q_ref, k_ref, vt_ref, b_ref, o_ref, m_ref, acc_ref, sa_ref, sb_ref = refs
    else:
        q_ref, k_ref, vt_ref, o_ref, m_ref, acc_ref, sa_ref, sb_ref = refs
        b_ref = None
    i = pl.program_id(2)
    rows = rep * tq

    m_ref[...] = jnp.full(m_ref.shape, MASK_VALUE, F32)
    acc_ref[...] = jnp.zeros(acc_ref.shape, F32)
    q = q_ref[0, 0, 0]

    def qk(j, dst_ref):
        start = pl.multiple_of(j * tkb, tkb)
        kb = k_ref[0, 0, pl.ds(start, tkb), :]
        dst_ref[...] = lax.dot_general(kb, q, (((1,), (1,)), ((), ())),
                                       preferred_element_type=F32)

    def softmax_pv(j, src_ref, masked):
        start = pl.multiple_of(j * tkb, tkb)
        vtb = vt_ref[0, 0, :, pl.ds(start, tkb)]
        s = src_ref[...]
        if has_bias:
            bias = b_ref[0, pl.ds(start, tkb), :].astype(F32)
            s = s + jnp.tile(bias, (1, rep))
        if masked:
            kpos = start + lax.broadcasted_iota(jnp.int32, (tkb, rows), 0)
            qlane = lax.broadcasted_iota(jnp.int32, (tkb, rows), 1)
            qpos = i * tq + (qlane % tq if rep > 1 else qlane)
            s = jnp.where(kpos <= qpos, s, MASK_VALUE)
        m_prev = m_ref[...]
        m_new = jnp.maximum(m_prev, jnp.max(s, axis=0, keepdims=True))
        p = jnp.exp2(s - m_new)
        alpha = jnp.exp2(m_prev - m_new)
        acc_ref[...] = alpha * acc_ref[...] + jnp.dot(vtb, p.astype(BF16), preferred_element_type=F32)
        m_ref[...] = m_new

    qk(0, sa_ref)
    if has_bias:
        n_blocks = ((i + 1) * tq + tkb - 1) // tkb

        def pair(jj, c):
            j0 = 2 * jj
            qk(j0 + 1, sb_ref)
            softmax_pv(j0, sa_ref, False)
            qk(jnp.minimum(j0 + 2, n_blocks - 1), sa_ref)
            softmax_pv(j0 + 1, sb_ref, False)
            return c

        lax.fori_loop(0, n_blocks // 2, pair, 0)

        @pl.when(n_blocks % 2 == 1)
        def _():
            softmax_pv(n_blocks - 1, sa_ref, False)
    else:
        def pair(jj, c):
            j0 = 2 * jj
            qk(j0 + 1, sb_ref)
            softmax_pv(j0, sa_ref, False)
            qk(j0 + 2, sa_ref)
            softmax_pv(j0 + 1, sb_ref, False)
            return c

        lax.fori_loop(0, i, pair, 0)
        qk(2 * i + 1, sb_ref)
        softmax_pv(2 * i, sa_ref, True)
        softmax_pv(2 * i + 1, sb_ref, True)

    acc = acc_ref[...]
    out = (acc[0:dv] * (1.0 / acc[dv:dv + 1])).T
    for r in range(rep):
        o_ref[0, :, r * dv:(r + 1) * dv] = out[r * tq:(r + 1) * tq].astype(o_ref.dtype)


def flash_attention(q, k, vt, bias, *, tq, rep, out_dtype):
    b, hk, nq, rows, dq = q.shape
    s = k.shape[2]
    dve = vt.shape[2]
    dv = dve - V_ONES_ROWS
    tkb = _tile(s, FLASH_TKB)
    has_bias = bias is not None
    assert has_bias or tq == 2 * tkb
    in_specs = [pl.BlockSpec((1, 1, 1, rows, dq), lambda bb, h, i: (bb, h, i, 0, 0)),
                pl.BlockSpec((1, 1, s, dq), lambda bb, h, i: (bb, h, 0, 0)),
                pl.BlockSpec((1, 1, dve, s), lambda bb, h, i: (bb, h, 0, 0))]
    args = [q, k, vt]
    if has_bias:
        in_specs.append(pl.BlockSpec((1, s, tq), lambda bb, h, i: (bb, 0, i)))
        args.append(bias)
    return pl.pallas_call(
        functools.partial(_flash_kernel, tq=tq, tkb=tkb, rep=rep, dv=dv, has_bias=has_bias),
        grid=(b, hk, nq),
        in_specs=in_specs,
        out_specs=pl.BlockSpec((1, tq, rep * dv), lambda bb, h, i: (bb, i, h)),
        out_shape=jax.ShapeDtypeStruct((b, s, hk * rep * dv), out_dtype),
        scratch_shapes=[pltpu.VMEM((1, rows), F32), pltpu.VMEM((dve, rows), F32),
                        pltpu.VMEM((tkb, rows), F32), pltpu.VMEM((tkb, rows), F32)],
        compiler_params=_params(("parallel", "parallel", "parallel")),
        name="flash_attention",
    )(*args)


S5_NSEG = 2 * SUBLANES
S5_CBLK = LANES
S5_LT = 16


def _s5_kernel(u_ref, bbd_ref, cbd_ref, a_ref, ap_ref, d_ref, y_ref,
               ug_ref, bu_ref, xs_ref, init_ref, *, seg_len, lt):
    nseg = S5_NSEG
    ns = a_ref.shape[-1]
    a_re = jnp.broadcast_to(a_ref[0, 0:1, :], (nseg, ns))
    a_im = jnp.broadcast_to(a_ref[0, 1:2, :], (nseg, ns))
    ntiles = seg_len // lt

    def load_inputs(t):
        for i in range(lt):
            ug_ref[i * nseg:(i + 1) * nseg, :] = u_ref[0, pl.ds(t * lt + i, nseg, stride=seg_len), :]
        bu_ref[...] = jnp.dot(ug_ref[...].astype(BF16), bbd_ref[0], preferred_element_type=F32)

    def scan_tile(x_re, x_im, store):
        for i in range(lt):
            r0 = i * nseg
            b_re = bu_ref[r0:r0 + nseg, 0:ns]
            b_im = bu_ref[r0:r0 + nseg, ns:2 * ns]
            n_re = a_re * x_re - a_im * x_im + b_re
            n_im = a_re * x_im + a_im * x_re + b_im
            x_re, x_im = n_re, n_im
            if store:
                xs_ref[r0:r0 + nseg, 0:ns] = x_re
                xs_ref[r0:r0 + nseg, ns:2 * ns] = x_im
        return x_re, x_im

    def pass1(t, carry):
        load_inputs(t)
        return scan_tile(carry[0], carry[1], False)

    zeros = jnp.zeros((nseg, ns), F32)
    e_re, e_im = lax.fori_loop(0, ntiles, pass1, (zeros, zeros))

    ap_re = ap_ref[0, 0:1, :]
    ap_im = ap_ref[0, 1:2, :]
    t_re = jnp.zeros((1, ns), F32)
    t_im = jnp.zeros((1, ns), F32)
    init_ref[0:1, :] = jnp.zeros((1, 2 * ns), F32)
    for s in range(1, nseg):
        p_re, p_im = e_re[s - 1:s], e_im[s - 1:s]
        t_re, t_im = (ap_re * t_re - ap_im * t_im + p_re,
                      ap_re * t_im + ap_im * t_re + p_im)
        init_ref[s:s + 1, 0:ns] = t_re
        init_ref[s:s + 1, ns:2 * ns] = t_im

    def pass2(t, carry):
        load_inputs(t)
        x_re, x_im = scan_tile(carry[0], carry[1], True)
        y = jnp.dot(xs_ref[...].astype(BF16), cbd_ref[0], preferred_element_type=F32)
        y = jax.nn.gelu(y + d_ref[0] * ug_ref[...])
        for i in range(lt):
            y_ref[0, pl.ds(t * lt + i, nseg, stride=seg_len), :] = y[i * nseg:(i + 1) * nseg]
        return x_re, x_im

    lax.fori_loop(0, ntiles, pass2, (init_ref[:, 0:ns], init_ref[:, ns:2 * ns]))


def _s5_discretise(lam_re, lam_im, log_step, b_re, b_im, c_re, c_im, seg_len):
    g, n = lam_re.shape
    p = S5_GROUP
    lr = jnp.minimum(lam_re.astype(F32), S5_LAMBDA_RE_MAX)
    li = lam_im.astype(F32)
    step = jnp.exp(log_step.astype(F32))[:, None]
    mag = jnp.exp(lr * step)
    ab_re = mag * jnp.cos(li * step)
    ab_im = mag * jnp.sin(li * step)
    den = lr * lr + li * li
    f_re = ((ab_re - 1.0) * lr + ab_im * li) / den
    f_im = (ab_im * lr - (ab_re - 1.0) * li) / den
    br = b_re.astype(F32)
    bi = b_im.astype(F32)
    bb_re = f_re[..., None] * br - f_im[..., None] * bi
    bb_im = f_re[..., None] * bi + f_im[..., None] * br
    pw_re, pw_im = jnp.ones_like(ab_re), jnp.zeros_like(ab_im)
    sq_re, sq_im = ab_re, ab_im
    e = seg_len
    while e:
        if e & 1:
            pw_re, pw_im = pw_re * sq_re - pw_im * sq_im, pw_re * sq_im + pw_im * sq_re
        sq_re, sq_im = sq_re * sq_re - sq_im * sq_im, 2.0 * sq_re * sq_im
        e >>= 1
    gb = S5_CBLK // p
    nb = g // gb
    eye = jnp.eye(gb, dtype=F32)

    def blockdiag_in(bb):
        x = bb.reshape(nb, gb, n, p)
        return jnp.einsum("bgnp,gh->bgphn", x, eye).reshape(nb, gb * p, gb * n)

    def blockdiag_out(c):
        x = c.astype(F32).reshape(nb, gb, p, n)
        return jnp.einsum("bgpn,gh->bgnhp", x, eye).reshape(nb, gb * n, gb * p)

    bbd = jnp.concatenate([blockdiag_in(bb_re), blockdiag_in(bb_im)], axis=2).astype(BF16)
    cbd = jnp.concatenate([blockdiag_out(c_re), -blockdiag_out(c_im)], axis=1).astype(BF16)
    a = jnp.stack([ab_re.reshape(nb, gb * n), ab_im.reshape(nb, gb * n)], axis=1)
    ap = jnp.stack([pw_re.reshape(nb, gb * n), pw_im.reshape(nb, gb * n)], axis=1)
    return bbd, cbd, a, ap


def s5_scan(u, lam_re, lam_im, log_step, b_re, b_im, c_re, c_im, d_skip):
    b, s, w = u.shape
    seg_len = s // S5_NSEG
    lt = min(S5_LT, seg_len)
    bbd, cbd, a, ap = _s5_discretise(lam_re, lam_im, log_step, b_re, b_im, c_re, c_im, seg_len)
    nb, cw, ns2 = bbd.shape
    ns = ns2 // 2
    rows = lt * S5_NSEG
    return pl.pallas_call(
        functools.partial(_s5_kernel, seg_len=seg_len, lt=lt),
        grid=(b, nb),
        in_specs=[pl.BlockSpec((1, s, cw), lambda bb, j: (bb, 0, j)),
                  pl.BlockSpec((1, cw, ns2), lambda bb, j: (j, 0, 0)),
                  pl.BlockSpec((1, ns2, cw), lambda bb, j: (j, 0, 0)),
                  pl.BlockSpec((1, 2, ns), lambda bb, j: (j, 0, 0)),
                  pl.BlockSpec((1, 2, ns), lambda bb, j: (j, 0, 0)),
                  pl.BlockSpec((1, 1, cw), lambda bb, j: (j, 0, 0))],
        out_specs=pl.BlockSpec((1, s, cw), lambda bb, j: (bb, 0, j)),
        out_shape=jax.ShapeDtypeStruct((b, s, w), F32),
        scratch_shapes=[pltpu.VMEM((rows, cw), F32), pltpu.VMEM((rows, ns2), F32),
                        pltpu.VMEM((rows, ns2), F32), pltpu.VMEM((S5_NSEG, ns2), F32)],
        compiler_params=_params(("parallel", "parallel")),
        name="s5_scan",
    )(u, bbd, cbd, a, ap, d_skip.astype(F32).reshape(nb, 1, cw))


def _even_lat_kernel(lat_ref, gq_ref, gkv_ref, cos_ref, sin_ref, qn_ref, kvn_ref, kr_ref, *, q_rank, kv_rank):
    lat = lat_ref[...]
    q_lat = lat[:, 0:q_rank]
    kv_lat = lat[:, q_rank:q_rank + kv_rank]
    kr = lat[:, q_rank + kv_rank:q_rank + kv_rank + LANES]
    rq = lax.rsqrt(jnp.mean(q_lat * q_lat, axis=-1, keepdims=True) + EPS)
    qn_ref[...] = (q_lat * rq * gq_ref[...]).astype(qn_ref.dtype)
    rkv = lax.rsqrt(jnp.mean(kv_lat * kv_lat, axis=-1, keepdims=True) + EPS)
    kvn_ref[...] = (kv_lat * rkv * gkv_ref[...]).astype(kvn_ref.dtype)
    kr = _rope_tile(kr, cos_ref[...], sin_ref[...], MLA_ROPE // 2)
    lane = lax.broadcasted_iota(jnp.int32, kr.shape, 1)
    lo = jnp.where(lane < MLA_ROPE, kr, 0.0)
    hi = pltpu.roll(lo, MLA_ROPE, 1)
    kr_ref[:, 0:LANES] = lo.astype(kr_ref.dtype)
    kr_ref[:, LANES:2 * LANES] = hi.astype(kr_ref.dtype)


def even_lat_prep(lat, gq, gkv, cos_t, sin_t):
    m, wl = lat.shape
    q_rank, kv_rank = gq.shape[0], gkv.shape[0]
    tm = _tile(m, 512, SUBLANES)
    return pl.pallas_call(
        functools.partial(_even_lat_kernel, q_rank=q_rank, kv_rank=kv_rank),
        grid=(m // tm,),
        in_specs=[pl.BlockSpec((tm, wl), lambda i: (i, 0)),
                  pl.BlockSpec((1, q_rank), lambda i: (0, 0)),
                  pl.BlockSpec((1, kv_rank), lambda i: (0, 0)),
                  pl.BlockSpec((tm, LANES), lambda i: (i, 0)),
                  pl.BlockSpec((tm, LANES), lambda i: (i, 0))],
        out_specs=[pl.BlockSpec((tm, q_rank), lambda i: (i, 0)),
                   pl.BlockSpec((tm, kv_rank), lambda i: (i, 0)),
                   pl.BlockSpec((tm, 2 * LANES), lambda i: (i, 0))],
        out_shape=[jax.ShapeDtypeStruct((m, q_rank), BF16),
                   jax.ShapeDtypeStruct((m, kv_rank), BF16),
                   jax.ShapeDtypeStruct((m, 2 * LANES), BF16)],
        compiler_params=_params(("parallel",)),
        name="even_lat_prep",
    )(lat, gq.reshape(1, -1).astype(F32), gkv.reshape(1, -1).astype(F32), cos_t, sin_t)


def _mla_pack_kernel(q_ref, kv_ref, kr_ref, cos_ref, sin_ref, qf_ref, kf_ref, vf_ref, *, heads, scale):
    cos_t, sin_t = cos_ref[...], sin_ref[...]
    nope_w = heads * MLA_NOPE
    lane = lax.broadcasted_iota(jnp.int32, (q_ref.shape[1], LANES), 1)
    for pair in range(heads // 2):
        rope = _rope_tile(q_ref[0, :, nope_w + pair * LANES:nope_w + (pair + 1) * LANES],
                          cos_t, sin_t, MLA_ROPE // 2)
        for sub in range(2):
            h = 2 * pair + sub
            keep = (lane < MLA_ROPE) if sub == 0 else (lane >= MLA_ROPE)
            qf_ref[0, h, :, 0:LANES] = (q_ref[0, :, h * MLA_NOPE:(h + 1) * MLA_NOPE] * scale).astype(qf_ref.dtype)
            qf_ref[0, h, :, LANES:2 * LANES] = (jnp.where(keep, rope, 0.0) * scale).astype(qf_ref.dtype)
            kf_ref[0, h, :, 0:LANES] = kv_ref[0, :, h * 2 * LANES:h * 2 * LANES + LANES]
            kf_ref[0, h, :, LANES:2 * LANES] = kr_ref[0, :, sub * LANES:(sub + 1) * LANES]
            v_h = kv_ref[0, :, h * 2 * LANES + LANES:(h + 1) * 2 * LANES].astype(F32)
            vf_ref[0, h, 0:MLA_V, :] = v_h.T.astype(vf_ref.dtype)
            vf_ref[0, h, MLA_V:MLA_V + V_ONES_ROWS, :] = jnp.ones((V_ONES_ROWS, v_h.shape[0]), vf_ref.dtype)


def mla_pack(q, kv, kr2, cos_t, sin_t, heads):
    b, s, _ = q.shape
    tm = _tile(s, 256, SUBLANES)
    scale = (MLA_NOPE + MLA_ROPE) ** -0.5 * LOG2E
    return pl.pallas_call(
        functools.partial(_mla_pack_kernel, heads=heads, scale=scale),
        grid=(b, s // tm),
        in_specs=[pl.BlockSpec((1, tm, q.shape[2]), lambda bb, i: (bb, i, 0)),
                  pl.BlockSpec((1, tm, kv.shape[2]), lambda bb, i: (bb, i, 0)),
                  pl.BlockSpec((1, tm, 2 * LANES), lambda bb, i: (bb, i, 0)),
                  pl.BlockSpec((tm, LANES), lambda bb, i, nt=s // tm: (bb * nt + i, 0)),
                  pl.BlockSpec((tm, LANES), lambda bb, i, nt=s // tm: (bb * nt + i, 0))],
        out_specs=[pl.BlockSpec((1, heads, tm, 2 * LANES), lambda bb, i: (bb, 0, i, 0)),
                   pl.BlockSpec((1, heads, tm, 2 * LANES), lambda bb, i: (bb, 0, i, 0)),
                   pl.BlockSpec((1, heads, MLA_V + V_ONES_ROWS, tm), lambda bb, i: (bb, 0, 0, i))],
        out_shape=[jax.ShapeDtypeStruct((b, heads, s, 2 * LANES), BF16),
                   jax.ShapeDtypeStruct((b, heads, s, 2 * LANES), BF16),
                   jax.ShapeDtypeStruct((b, heads, MLA_V + V_ONES_ROWS, s), BF16)],
        compiler_params=_params(("parallel", "parallel")),
        name="mla_pack",
    )(q, kv, kr2, cos_t, sin_t)


def even_mixer(x2, h, b, s, cos_m, sin_m, w_in, lam_re, lam_im, log_step, b_re, b_im, c_re, c_im,
               d_skip, glu_a, glu_b, q_norm_g, w_q_up, kv_norm_g, w_kv_up, w_out):
    t, d = h.shape
    s5_w = d_skip.shape[0]
    q_rank, kv_rank = q_norm_g.shape[0], kv_norm_g.shape[0]
    heads = w_kv_up.shape[1] // (MLA_NOPE + MLA_V)
    lat_w = q_rank + kv_rank + MLA_ROPE
    lat_pad = -(-(q_rank + kv_rank + LANES) // (2 * LANES)) * (2 * LANES)

    w_u = w_in[:, :s5_w].astype(BF16)
    w_lat = jnp.pad(w_in[:, s5_w:], ((0, 0), (0, lat_pad - lat_w))).astype(BF16)
    u = matmul(h, w_u, F32)
    lat = matmul(h, w_lat, F32)

    y = s5_scan(u.reshape(b, s, s5_w), lam_re, lam_im, log_step, b_re, b_im, c_re, c_im, d_skip)
    s5_out = glu_matmul(y.reshape(t, s5_w), glu_a.astype(BF16), glu_b.astype(BF16), BF16)

    qn, kvn, kr2 = even_lat_prep(lat, q_norm_g, kv_norm_g, cos_m, sin_m)
    wq = w_q_up.reshape(q_rank, heads, MLA_NOPE + MLA_ROPE)
    wq = jnp.concatenate([wq[:, :, :MLA_NOPE].reshape(q_rank, heads * MLA_NOPE),
                          wq[:, :, MLA_NOPE:].reshape(q_rank, heads * MLA_ROPE)], axis=1).astype(BF16)
    q = matmul(qn, wq, F32)
    kv = matmul(kvn, w_kv_up.astype(BF16), BF16)
    qf, kf, vf = mla_pack(q.reshape(b, s, -1), kv.reshape(b, s, -1), kr2.reshape(b, s, -1), cos_m, sin_m, heads)
    tq = 2 * _tile(s, FLASH_TKB)
    mla_out = flash_attention(qf.reshape(b, heads, s // tq, tq, 2 * LANES), kf, vf, None,
                              tq=tq, rep=1, out_dtype=BF16)
    mixed = jnp.concatenate([s5_out, mla_out.reshape(t, -1)], axis=-1)
    return matmul(mixed, w_out.astype(BF16), F32, kind="residual", res=x2)


def _odd_lat_kernel(z_ref, gq_ref, lng_ref, lnb_ref, cos_ref, sin_ref,
                    qn_ref, k_ref, v_ref, kidx_ref, w_ref, *, q_rank, w_scale):
    cos_t, sin_t = cos_ref[...], sin_ref[...]
    half = ROT_DIM // 2
    z = z_ref[0]
    q_lat = z[:, 0:q_rank]
    rq = lax.rsqrt(jnp.mean(q_lat * q_lat, axis=-1, keepdims=True) + EPS)
    qn_ref[0] = (q_lat * rq * gq_ref[...]).astype(qn_ref.dtype)
    kvh = DSA_KV_HEADS
    d = DSA_HEAD_DIM
    for g in range(kvh):
        kh = z[:, q_rank + g * d:q_rank + (g + 1) * d]
        k_ref[0, g] = _rope_tile(kh, cos_t, sin_t, half).astype(k_ref.dtype)
        v_ref[0, g, 0:d, :] = z[:, q_rank + (kvh + g) * d:q_rank + (kvh + g + 1) * d].T.astype(v_ref.dtype)
        v_ref[0, g, d:d + V_ONES_ROWS, :] = jnp.ones((V_ONES_ROWS, z.shape[0]), v_ref.dtype)
    off = q_rank + 2 * kvh * d
    ki = z[:, off:off + IDX_DIM]
    kc = ki - jnp.mean(ki, axis=-1, keepdims=True)
    var = jnp.mean(kc * kc, axis=-1, keepdims=True)
    ki = kc * lax.rsqrt(var + EPS) * lng_ref[...] + lnb_ref[...]
    kidx_ref[0] = _rope_tile(ki, cos_t, sin_t, half).astype(kidx_ref.dtype)
    w_ref[0] = (z[:, off + IDX_DIM:off + IDX_DIM + LANES] * w_scale).T


def odd_lat_prep(z, gq, ln_g, ln_b, cos_t, sin_t, idx_heads):
    b, s, zw = z.shape
    q_rank = gq.shape[0]
    tm = _tile(s, 512, SUBLANES)
    nt = s // tm
    w_scale = idx_heads ** -0.5 * IDX_DIM ** -0.5
    kvh, d = DSA_KV_HEADS, DSA_HEAD_DIM
    return pl.pallas_call(
        functools.partial(_odd_lat_kernel, q_rank=q_rank, w_scale=w_scale),
        grid=(b, nt),
        in_specs=[pl.BlockSpec((1, tm, zw), lambda bb, i: (bb, i, 0)),
                  pl.BlockSpec((1, q_rank), lambda bb, i: (0, 0)),
                  pl.BlockSpec((1, IDX_DIM), lambda bb, i: (0, 0)),
                  pl.BlockSpec((1, IDX_DIM), lambda bb, i: (0, 0)),
                  pl.BlockSpec((tm, LANES), lambda bb, i: (bb * nt + i, 0)),
                  pl.BlockSpec((tm, LANES), lambda bb, i: (bb * nt + i, 0))],
        out_specs=[pl.BlockSpec((1, tm, q_rank), lambda bb, i: (bb, i, 0)),
                   pl.BlockSpec((1, kvh, tm, d), lambda bb, i: (bb, 0, i, 0)),
                   pl.BlockSpec((1, kvh, d + V_ONES_ROWS, tm), lambda bb, i: (bb, 0, 0, i)),
                   pl.BlockSpec((1, tm, IDX_DIM), lambda bb, i: (bb, i, 0)),
                   pl.BlockSpec((1, LANES, tm), lambda bb, i: (bb, 0, i))],
        out_shape=[jax.ShapeDtypeStruct((b, s, q_rank), BF16),
                   jax.ShapeDtypeStruct((b, kvh, s, d), BF16),
                   jax.ShapeDtypeStruct((b, kvh, d + V_ONES_ROWS, s), BF16),
                   jax.ShapeDtypeStruct((b, s, IDX_DIM), BF16),
                   jax.ShapeDtypeStruct((b, LANES, s), F32)],
        compiler_params=_params(("parallel", "parallel")),
        name="odd_lat_prep",
    )(z, gq.reshape(1, -1).astype(F32), ln_g.reshape(1, -1).astype(F32),
      ln_b.reshape(1, -1).astype(F32), cos_t, sin_t)


def _dsa_q_kernel(qq_ref, cos_ref, sin_ref, q_ref, qi_ref, *, heads, idx_heads, rep, tq, scale):
    cos_t, sin_t = cos_ref[...], sin_ref[...]
    half = ROT_DIM // 2
    d = DSA_HEAD_DIM
    for h in range(heads):
        g, r = divmod(h, rep)
        xh = _rope_tile(qq_ref[0, :, h * d:(h + 1) * d], cos_t, sin_t, half)
        q_ref[0, g, 0, r * tq:(r + 1) * tq, :] = (xh * scale).astype(q_ref.dtype)
    off = heads * d
    for h in range(idx_heads):
        xh = _rope_tile(qq_ref[0, :, off + h * IDX_DIM:off + (h + 1) * IDX_DIM], cos_t, sin_t, half)
        qi_ref[0, :, h * IDX_DIM:(h + 1) * IDX_DIM] = xh.astype(qi_ref.dtype)


def dsa_q_prep(qq, cos_t, sin_t, heads, idx_heads, tq):
    b, s, wq = qq.shape
    nt = s // tq
    rep = heads // DSA_KV_HEADS
    d = DSA_HEAD_DIM
    scale = DSA_HEAD_DIM ** -0.5 * LOG2E
    return pl.pallas_call(
        functools.partial(_dsa_q_kernel, heads=heads, idx_heads=idx_heads, rep=rep, tq=tq, scale=scale),
        grid=(b, nt),
        in_specs=[pl.BlockSpec((1, tq, wq), lambda bb, i: (bb, i, 0)),
                  pl.BlockSpec((tq, LANES), lambda bb, i: (bb * nt + i, 0)),
                  pl.BlockSpec((tq, LANES), lambda bb, i: (bb * nt + i, 0))],
        out_specs=[pl.BlockSpec((1, DSA_KV_HEADS, 1, rep * tq, d), lambda bb, i: (bb, 0, i, 0, 0)),
                   pl.BlockSpec((1, tq, idx_heads * IDX_DIM), lambda bb, i: (bb, i, 0))],
        out_shape=[jax.ShapeDtypeStruct((b, DSA_KV_HEADS, nt, rep * tq, d), BF16),
                   jax.ShapeDtypeStruct((b, s, idx_heads * IDX_DIM), BF16)],
        compiler_params=_params(("parallel", "parallel")),
        name="dsa_q_prep",
    )(qq, cos_t, sin_t)


def _ukey_to_float(u):
    bits = jnp.where(u < 0, u & jnp.int32(0x7FFFFFFF), ~u)
    return lax.bitcast_convert_type(bits, F32)


def _threshold_of_key(u):
    thr = _ukey_to_float(u)
    return jnp.where((thr != thr) & (u >= 0), jnp.float32(-jnp.inf), thr)


def _indexer_kernel(qi_ref, wt_ref, kidx_ref, bias_ref, score_ref, *, tq, tkb, idx_heads, top_k):
    i = pl.program_id(1)
    s_len = kidx_ref.shape[1]
    n_blocks = ((i + 1) * tq + tkb - 1) // tkb
    neg_inf = jnp.float32(-jnp.inf)
    qpos = i * tq + lax.broadcasted_iota(jnp.int32, (1, tq), 1)

    def causal(start):
        kpos = start + lax.broadcasted_iota(jnp.int32, (tkb, tq), 0)
        return kpos <= qpos

    def score_block(j, c):
        start = pl.multiple_of(j * tkb, tkb)
        kb = kidx_ref[0, pl.ds(start, tkb), :]
        acc = jnp.zeros((tkb, tq), F32)
        for h in range(idx_heads):
            logits = lax.dot_general(kb, qi_ref[0, :, h * IDX_DIM:(h + 1) * IDX_DIM],
                                     (((1,), (1,)), ((), ())), preferred_element_type=F32)
            acc = acc + jnp.maximum(logits, 0.0) * wt_ref[0, h:h + 1, :]
        score_ref[pl.ds(start, tkb), :] = jnp.where(causal(start), acc, neg_inf)
        return c

    lax.fori_loop(0, n_blocks, score_block, 0)

    def count_ge(thr):
        def body(j, cnt):
            start = pl.multiple_of(j * tkb, tkb)
            ind = jnp.where(score_ref[pl.ds(start, tkb), :] >= thr, 1.0, 0.0)
            part = jnp.sum(ind.reshape(tkb // (8 * SUBLANES), 8, SUBLANES, tq), axis=1)
            return cnt + jnp.sum(part, axis=0)
        cnt = lax.fori_loop(0, n_blocks, body, jnp.zeros((SUBLANES, tq), F32))
        return jnp.sum(cnt, axis=0, keepdims=True)

    k_f = jnp.float32(top_k)
    few = qpos < top_k

    def search_cond(st):
        bi, _, cnt_u = st
        pending = jnp.where(few | (cnt_u == k_f), 0.0, 1.0)
        return (bi < 32) & (jnp.max(pending) > 0.0)

    def search_step(st):
        bi, u, cnt_u = st
        cand = u | (jnp.int32(1) << (31 - bi))
        cnt = count_ge(_threshold_of_key(cand))
        take = cnt >= k_f
        return bi + 1, jnp.where(take, cand, u), jnp.where(take, cnt, cnt_u)

    total = (n_blocks * tkb).astype(F32)
    _, u, _ = lax.while_loop(search_cond, search_step,
                             (jnp.int32(0), jnp.zeros((1, tq), jnp.int32), jnp.full((1, tq), total, F32)))
    thr = jnp.where(few, neg_inf, _threshold_of_key(u))

    def write_block(j, c):
        start = pl.multiple_of(j * tkb, tkb)
        keep = (score_ref[pl.ds(start, tkb), :] >= thr) & causal(start)
        bias_ref[0, pl.ds(start, tkb), :] = jnp.where(keep, 0.0, MASK_VALUE).astype(bias_ref.dtype)
        return c

    lax.fori_loop(0, n_blocks, write_block, 0)

    def fill_block(j, c):
        start = pl.multiple_of(j * tkb, tkb)
        bias_ref[0, pl.ds(start, tkb), :] = jnp.full((tkb, tq), MASK_VALUE, bias_ref.dtype)
        return c

    lax.fori_loop(n_blocks, s_len // tkb, fill_block, 0)


def dsa_indexer(qi, wt, kidx, idx_heads, top_k):
    b, s, _ = qi.shape
    tq = _tile(s, 256)
    tkb = _tile(s, 512)
    assert tkb >= top_k or tkb == s
    return pl.pallas_call(
        functools.partial(_indexer_kernel, tq=tq, tkb=tkb, idx_heads=idx_heads, top_k=top_k),
        grid=(b, s // tq),
        in_specs=[pl.BlockSpec((1, tq, idx_heads * IDX_DIM), lambda bb, i: (bb, i, 0)),
                  pl.BlockSpec((1, LANES, tq), lambda bb, i: (bb, 0, i)),
                  pl.BlockSpec((1, s, IDX_DIM), lambda bb, i: (bb, 0, 0))],
        out_specs=pl.BlockSpec((1, s, tq), lambda bb, i: (bb, 0, i)),
        out_shape=jax.ShapeDtypeStruct((b, s, s), BF16),
        scratch_shapes=[pltpu.VMEM((s, tq), F32)],
        compiler_params=_params(("parallel", "parallel")),
        name="dsa_indexer",
    )(qi, wt, kidx)


def odd_mixer(x2, h, b, s, cos_p, sin_p, w_in, q_norm_g, w_q_up, w_idx_q, k_ln_g, k_ln_b, w_out):
    t, d = h.shape
    q_rank = q_norm_g.shape[0]
    heads = w_q_up.shape[1] // DSA_HEAD_DIM
    idx_heads = w_idx_q.shape[1] // IDX_DIM
    in_w = w_in.shape[1]
    z_w = -(-(in_w - idx_heads + LANES) // (2 * LANES)) * (2 * LANES)
    w_z = jnp.pad(w_in, ((0, 0), (0, z_w - in_w))).astype(BF16)
    z = matmul(h, w_z, F32)
    qn, k, vt, kidx, wt = odd_lat_prep(z.reshape(b, s, z_w), q_norm_g, k_ln_g, k_ln_b, cos_p, sin_p, idx_heads)
    w_qq = jnp.concatenate([w_q_up, w_idx_q], axis=1).astype(BF16)
    qq = matmul(qn.reshape(t, q_rank), w_qq, F32)
    tq = _tile(s, 128)
    q, qi = dsa_q_prep(qq.reshape(b, s, -1), cos_p, sin_p, heads, idx_heads, tq)
    top_k = min(IDX_TOPK_MAX, s // 4)
    bias = dsa_indexer(qi, wt, kidx, idx_heads, top_k)
    o = flash_attention(q, k, vt, bias, tq=tq, rep=heads // DSA_KV_HEADS, out_dtype=BF16)
    return matmul(o.reshape(t, -1), w_out.astype(BF16), F32, kind="residual", res=x2)


def sq_relu_mlp(x2, h, w_up, w_down):
    a = matmul(h, w_up, BF16, kind="relu2")
    return matmul(a, w_down, F32, kind="residual", res=x2)


def kernel(x, positions, norm_mix_g, norm_mlp_g, final_norm_g, even_w_in, s5_lam_re, s5_lam_im, s5_log_step, s5_b_re, s5_b_im, s5_c_re, s5_c_im, s5_d, s5_glu_a, s5_glu_b, mla_q_norm_g, mla_w_q_up, mla_kv_norm_g, mla_w_kv_up, even_w_out, odd_w_in, dsa_q_norm_g, dsa_w_q_up, idx_w_q, idx_k_ln_g, idx_k_ln_b, odd_w_out, mlp_w_up, mlp_w_down):
    b, s, d = x.shape
    depth = norm_mix_g.shape[0]
    cos_m, sin_m = _rope_tables(positions, MLA_ROPE, LANES)
    cos_p, sin_p = _rope_tables(positions, ROT_DIM, ROT_DIM)
    x2 = x.reshape(b * s, d)
    for layer in range(depth):
        i = layer // 2
        h = rmsnorm(x2, norm_mix_g[layer], BF16)
        if layer % 2 == 0:
            x2 = even_mixer(x2, h, b, s, cos_m, sin_m, even_w_in[i], s5_lam_re[i], s5_lam_im[i],
                            s5_log_step[i], s5_b_re[i], s5_b_im[i], s5_c_re[i], s5_c_im[i], s5_d[i],
                            s5_glu_a[i], s5_glu_b[i], mla_q_norm_g[i], mla_w_q_up[i],
                            mla_kv_norm_g[i], mla_w_kv_up[i], even_w_out[i])
        else:
            x2 = odd_mixer(x2, h, b, s, cos_p, sin_p, odd_w_in[i], dsa_q_norm_g[i], dsa_w_q_up[i],
                           idx_w_q[i], idx_k_ln_g[i], idx_k_ln_b[i], odd_w_out[i])
        h = rmsnorm(x2, norm_mlp_g[layer], BF16)
        x2 = sq_relu_mlp(x2, h, mlp_w_up[layer], mlp_w_down[layer])
    return rmsnorm(x2, final_norm_g, x.dtype).reshape(b, s, d)
```

```python
import functools
import math

import jax
import jax.numpy as jnp
from jax import lax
from jax.experimental import pallas as pl
from jax.experimental.pallas import tpu as pltpu

F32 = jnp.float32
BF16 = jnp.bfloat16

EPS = 1e-6
ROPE_THETA = 500000.0
S5_GROUP = 16
S5_LAMBDA_RE_MAX = -1e-4
MLA_NOPE = 128
MLA_ROPE = 64
MLA_V = 128
DSA_HEAD_DIM = 128
DSA_KV_HEADS = 4
IDX_DIM = 128
IDX_TOPK_MAX = 256
ROT_DIM = DSA_HEAD_DIM // 4

LANES = 128
SUBLANES = 8
VMEM_LIMIT_BYTES = 56 * 1024 * 1024
MASK_VALUE = -1e30


def _params(semantics):
    return pltpu.CompilerParams(dimension_semantics=semantics, vmem_limit_bytes=VMEM_LIMIT_BYTES)


def _tile(dim, pref, align=LANES):
    if dim <= pref:
        return dim
    t = (pref // align) * align
    while t >= align:
        if dim % t == 0:
            return t
        t -= align
    return dim


def _rmsnorm_kernel(x_ref, g_ref, o_ref):
    x = x_ref[...].astype(F32)
    r = lax.rsqrt(jnp.mean(x * x, axis=-1, keepdims=True) + EPS)
    o_ref[...] = (x * r * g_ref[...]).astype(o_ref.dtype)


def rmsnorm(x, g, out_dtype):
    m, d = x.shape
    tm = _tile(m, 256, SUBLANES)
    return pl.pallas_call(
        _rmsnorm_kernel,
        grid=(m // tm,),
        in_specs=[pl.BlockSpec((tm, d), lambda i: (i, 0)),
                  pl.BlockSpec((1, d), lambda i: (0, 0))],
        out_specs=pl.BlockSpec((tm, d), lambda i: (i, 0)),
        out_shape=jax.ShapeDtypeStruct((m, d), out_dtype),
        compiler_params=_params(("parallel",)),
        name="rmsnorm",
    )(x, g.reshape(1, d).astype(F32))


def _mm_epilogue(acc, kind, res_ref):
    if kind == "relu2":
        a = jnp.maximum(acc, 0.0)
        return a * a
    if kind == "residual":
        return res_ref[...].astype(F32) + acc
    return acc


def _mm_kernel_single(*refs, kind):
    if kind == "residual":
        a_ref, w_ref, res_ref, o_ref = refs
    else:
        a_ref, w_ref, o_ref = refs
        res_ref = None
    acc = jnp.dot(a_ref[...].astype(BF16), w_ref[...].astype(BF16), preferred_element_type=F32)
    o_ref[...] = _mm_epilogue(acc, kind, res_ref).astype(o_ref.dtype)


def _mm_kernel_ksplit(*refs, kind, nk):
    if kind == "residual":
        a_ref, w_ref, res_ref, o_ref, acc_ref = refs
    else:
        a_ref, w_ref, o_ref, acc_ref = refs
        res_ref = None
    k = pl.program_id(2)

    @pl.when(k == 0)
    def _():
        acc_ref[...] = jnp.zeros_like(acc_ref)

    acc_ref[...] += jnp.dot(a_ref[...].astype(BF16), w_ref[...].astype(BF16), preferred_element_type=F32)

    @pl.when(k == nk - 1)
    def _():
        o_ref[...] = _mm_epilogue(acc_ref[...], kind, res_ref).astype(o_ref.dtype)


def matmul(a, w, out_dtype, kind="none", res=None, layer=None):
    m, k = a.shape
    n = w.shape[-1]
    tm = _tile(m, 1024, SUBLANES)

    def w_spec(tk_, tn_, index):
        if layer is None:
            return pl.BlockSpec((tk_, tn_), index)
        return pl.BlockSpec((None, tk_, tn_), lambda *g: (layer,) + index(*g))

    if k <= 4096:
        tn = _tile(n, 512)
        in_specs = [pl.BlockSpec((tm, k), lambda i, j: (i, 0)),
                    w_spec(k, tn, lambda i, j: (0, j))]
        args = [a, w]
        if kind == "residual":
            in_specs.append(pl.BlockSpec((tm, tn), lambda i, j: (i, j)))
            args.append(res)
        return pl.pallas_call(
            functools.partial(_mm_kernel_single, kind=kind),
            grid=(m // tm, n // tn),
            in_specs=in_specs,
            out_specs=pl.BlockSpec((tm, tn), lambda i, j: (i, j)),
            out_shape=jax.ShapeDtypeStruct((m, n), out_dtype),
            compiler_params=_params(("parallel", "parallel")),
            name="matmul",
        )(*args)
    tn = _tile(n, 1024)
    tk = _tile(k, 2048)
    nk = k // tk
    in_specs = [pl.BlockSpec((tm, tk), lambda i, j, kk: (i, kk)),
                w_spec(tk, tn, lambda i, j, kk: (kk, j))]
    args = [a, w]
    if kind == "residual":
        in_specs.append(pl.BlockSpec((tm, tn), lambda i, j, kk: (i, j)))
        args.append(res)
    return pl.pallas_call(
        functools.partial(_mm_kernel_ksplit, kind=kind, nk=nk),
        grid=(m // tm, n // tn, nk),
        in_specs=in_specs,
        out_specs=pl.BlockSpec((tm, tn), lambda i, j, kk: (i, j)),
        out_shape=jax.ShapeDtypeStruct((m, n), out_dtype),
        scratch_shapes=[pltpu.VMEM((tm, tn), F32)],
        compiler_params=_params(("parallel", "parallel", "arbitrary")),
        name="matmul_ksplit",
    )(*args)


def _glu_kernel(a_ref, wa_ref, wb_ref, o_ref):
    a = a_ref[...].astype(BF16)
    ya = jnp.dot(a, wa_ref[...], preferred_element_type=F32)
    yb = jnp.dot(a, wb_ref[...], preferred_element_type=F32)
    o_ref[...] = (ya * jax.nn.sigmoid(yb)).astype(o_ref.dtype)


def glu_matmul(a, wa, wb, out_dtype):
    m, k = a.shape
    _, n = wa.shape
    tm = _tile(m, 1024, SUBLANES)
    tn = _tile(n, 512)
    return pl.pallas_call(
        _glu_kernel,
        grid=(m // tm, n // tn),
        in_specs=[pl.BlockSpec((tm, k), lambda i, j: (i, 0)),
                  pl.BlockSpec((k, tn), lambda i, j: (0, j)),
                  pl.BlockSpec((k, tn), lambda i, j: (0, j))],
        out_specs=pl.BlockSpec((tm, tn), lambda i, j: (i, j)),
        out_shape=jax.ShapeDtypeStruct((m, n), out_dtype),
        compiler_params=_params(("parallel", "parallel")),
        name="glu_matmul",
    )(a, wa, wb)


def _rope_tile(x, cos_t, sin_t, half):
    lane = lax.broadcasted_iota(jnp.int32, x.shape, 1)
    first = (lane % (2 * half)) < half
    partner = jnp.where(first, pltpu.roll(x, LANES - half, 1), pltpu.roll(x, half, 1))
    return x * cos_t + partner * sin_t


def _rope_tables(positions, dim, pad_to):
    inv_freq = ROPE_THETA ** (-jnp.arange(0, dim, 2, dtype=F32) / dim)
    ang = positions.astype(F32).reshape(-1)[:, None] * inv_freq
    c, s = jnp.cos(ang), jnp.sin(ang)
    reps = pad_to // dim
    cos_t = jnp.tile(jnp.concatenate([c, c], axis=-1), (1, reps))
    sin_t = jnp.tile(jnp.concatenate([-s, s], axis=-1), (1, reps))
    t = c.shape[0]
    cos_t = jnp.concatenate([cos_t, jnp.ones((t, LANES - pad_to), F32)], axis=-1)
    sin_t = jnp.concatenate([sin_t, jnp.zeros((t, LANES - pad_to), F32)], axis=-1)
    return cos_t, sin_t


V_ONES_ROWS = 16
FLASH_TKB = 512
LOG2E = math.log2(math.e)


def _flash_kernel(*refs, tq, tkb, rep, dv, has_bias):
    if has_bias:
        q_ref, k_ref, vt_ref, b_ref, o_ref, m_ref, acc_ref, sa_ref, sb_ref = refs
    else:
        q_ref, k_ref, vt_ref, o_ref, m_ref, acc_ref, sa_ref, sb_ref = refs
        b_ref = None
    i = pl.program_id(2)
    rows = rep * tq

    m_ref[...] = jnp.full(m_ref.shape, MASK_VALUE, F32)
    acc_ref[...] = jnp.zeros(acc_ref.shape, F32)
    q = q_ref[0, 0, 0]

    def qk(j, dst_ref):
        start = pl.multiple_of(j * tkb, tkb)
        kb = k_ref[0, 0, pl.ds(start, tkb), :]
        dst_ref[...] = lax.dot_general(kb, q, (((1,), (1,)), ((), ())),
                                       preferred_element_type=F32)

    def softmax_pv(j, src_ref, masked):
        start = pl.multiple_of(j * tkb, tkb)
        vtb = vt_ref[0, 0, :, pl.ds(start, tkb)]
        s = src_ref[...]
        if has_bias:
            bias = b_ref[0, pl.ds(start, tkb), :].astype(F32)
            s = s + jnp.tile(bias, (1, rep))
        if masked:
            kpos = start + lax.broadcasted_iota(jnp.int32, (tkb, rows), 0)
            qlane = lax.broadcasted_iota(jnp.int32, (tkb, rows), 1)
            qpos = i * tq + (qlane % tq if rep > 1 else qlane)
            s = jnp.where(kpos <= qpos, s, MASK_VALUE)
        m_prev = m_ref[...]
        m_new = jnp.maximum(m_prev, jnp.max(s, axis=0, keepdims=True))
        p = jnp.exp2(s - m_new)
        alpha = jnp.exp2(m_prev - m_new)
        acc_ref[...] = alpha * acc_ref[...] + jnp.dot(vtb, p.astype(BF16), preferred_element_type=F32)
        m_ref[...] = m_new

    qk(0, sa_ref)
    if has_bias:
        n_blocks = ((i + 1) * tq + tkb - 1) // tkb

        def pair(jj, c):
            j0 = 2 * jj
            qk(j0 + 1, sb_ref)
            softmax_pv(j0, sa_ref, False)
            qk(jnp.minimum(j0 + 2, n_blocks - 1), sa_ref)
            softmax_pv(j0 + 1, sb_ref, False)
            return c

        lax.fori_loop(0, n_blocks // 2, pair, 0)

        @pl.when(n_blocks % 2 == 1)
        def _():
            softmax_pv(n_blocks - 1, sa_ref, False)
    else:
        def pair(jj, c):
            j0 = 2 * jj
            qk(j0 + 1, sb_ref)
            softmax_pv(j0, sa_ref, False)
            qk(j0 + 2, sa_ref)
            softmax_pv(j0 + 1, sb_ref, False)
            return c

        lax.fori_loop(0, i, pair, 0)
        qk(2 * i + 1, sb_ref)
        softmax_pv(2 * i, sa_ref, True)
        softmax_pv(2 * i + 1, sb_ref, True)

    acc = acc_ref[...]
    out = (acc[0:dv] * (1.0 / acc[dv:dv + 1])).T
    for r in range(rep):
        o_ref[0, :, r * dv:(r + 1) * dv] = out[r * tq:(r + 1) * tq].astype(o_ref.dtype)


def flash_attention(q, k, vt, bias, *, tq, rep, out_dtype):
    b, hk, nq, rows, dq = q.shape
    s = k.shape[2]
    dve = vt.shape[2]
    dv = dve - V_ONES_ROWS
    tkb = _tile(s, FLASH_TKB)
    has_bias = bias is not None
    assert has_bias or tq == 2 * tkb
    in_specs = [pl.BlockSpec((1, 1, 1, rows, dq), lambda bb, h, i: (bb, h, i, 0, 0)),
                pl.BlockSpec((1, 1, s, dq), lambda bb, h, i: (bb, h, 0, 0)),
                pl.BlockSpec((1, 1, dve, s), lambda bb, h, i: (bb, h, 0, 0))]
    args = [q, k, vt]
    if has_bias:
        in_specs.append(pl.BlockSpec((1, s, tq), lambda bb, h, i: (bb, 0, i)))
        args.append(bias)
    return pl.pallas_call(
        functools.partial(_flash_kernel, tq=tq, tkb=tkb, rep=rep, dv=dv, has_bias=has_bias),
        grid=(b, hk, nq),
        in_specs=in_specs,
        out_specs=pl.BlockSpec((1, tq, rep * dv), lambda bb, h, i: (bb, i, h)),
        out_shape=jax.ShapeDtypeStruct((b, s, hk * rep * dv), out_dtype),
        scratch_shapes=[pltpu.VMEM((1, rows), F32), pltpu.VMEM((dve, rows), F32),
                        pltpu.VMEM((tkb, rows), F32), pltpu.VMEM((tkb, rows), F32)],
        compiler_params=_params(("parallel", "parallel", "parallel")),
        name="flash_attention",
    )(*args)


S5_NSEG = 2 * SUBLANES
S5_CBLK = LANES
S5_LT = 16


def _s5_kernel(u_ref, bbd_ref, cbd_ref, a_ref, ap_ref, d_ref, y_ref,
               ug_ref, bu_ref, xs_ref, init_ref, *, seg_len, lt):
    nseg = S5_NSEG
    ns = a_ref.shape[-1]
    a_re = jnp.broadcast_to(a_ref[0, 0:1, :], (nseg, ns))
    a_im = jnp.broadcast_to(a_ref[0, 1:2, :], (nseg, ns))
    ntiles = seg_len // lt

    def load_inputs(t):
        for i in range(lt):
            ug_ref[i * nseg:(i + 1) * nseg, :] = u_ref[0, pl.ds(t * lt + i, nseg, stride=seg_len), :]
        bu_ref[...] = jnp.dot(ug_ref[...].astype(BF16), bbd_ref[0], preferred_element_type=F32)

    def scan_tile(x_re, x_im, store):
        for i in range(lt):
            r0 = i * nseg
            b_re = bu_ref[r0:r0 + nseg, 0:ns]
            b_im = bu_ref[r0:r0 + nseg, ns:2 * ns]
            n_re = a_re * x_re - a_im * x_im + b_re
            n_im = a_re * x_im + a_im * x_re + b_im
            x_re, x_im = n_re, n_im
            if store:
                xs_ref[r0:r0 + nseg, 0:ns] = x_re
                xs_ref[r0:r0 + nseg, ns:2 * ns] = x_im
        return x_re, x_im

    def pass1(t, carry):
        load_inputs(t)
        return scan_tile(carry[0], carry[1], False)

    zeros = jnp.zeros((nseg, ns), F32)
    e_re, e_im = lax.fori_loop(0, ntiles, pass1, (zeros, zeros))

    ap_re = ap_ref[0, 0:1, :]
    ap_im = ap_ref[0, 1:2, :]
    t_re = jnp.zeros((1, ns), F32)
    t_im = jnp.zeros((1, ns), F32)
    init_ref[0:1, :] = jnp.zeros((1, 2 * ns), F32)
    for s in range(1, nseg):
        p_re, p_im = e_re[s - 1:s], e_im[s - 1:s]
        t_re, t_im = (ap_re * t_re - ap_im * t_im + p_re,
                      ap_re * t_im + ap_im * t_re + p_im)
        init_ref[s:s + 1, 0:ns] = t_re
        init_ref[s:s + 1, ns:2 * ns] = t_im

    def pass2(t, carry):
        load_inputs(t)
        x_re, x_im = scan_tile(carry[0], carry[1], True)
        y = jnp.dot(xs_ref[...].astype(BF16), cbd_ref[0], preferred_element_type=F32)
        y = jax.nn.gelu(y + d_ref[0] * ug_ref[...])
        for i in range(lt):
            y_ref[0, pl.ds(t * lt + i, nseg, stride=seg_len), :] = y[i * nseg:(i + 1) * nseg]
        return x_re, x_im

    lax.fori_loop(0, ntiles, pass2, (init_ref[:, 0:ns], init_ref[:, ns:2 * ns]))


def _s5_discretise(lam_re, lam_im, log_step, b_re, b_im, c_re, c_im, seg_len):
    g, n = lam_re.shape
    p = S5_GROUP
    lr = jnp.minimum(lam_re.astype(F32), S5_LAMBDA_RE_MAX)
    li = lam_im.astype(F32)
    step = jnp.exp(log_step.astype(F32))[:, None]
    mag = jnp.exp(lr * step)
    ab_re = mag * jnp.cos(li * step)
    ab_im = mag * jnp.sin(li * step)
    den = lr * lr + li * li
    f_re = ((ab_re - 1.0) * lr + ab_im * li) / den
    f_im = (ab_im * lr - (ab_re - 1.0) * li) / den
    br = b_re.astype(F32)
    bi = b_im.astype(F32)
    bb_re = f_re[..., None] * br - f_im[..., None] * bi
    bb_im = f_re[..., None] * bi + f_im[..., None] * br
    pw_re, pw_im = jnp.ones_like(ab_re), jnp.zeros_like(ab_im)
    sq_re, sq_im = ab_re, ab_im
    e = seg_len
    while e:
        if e & 1:
            pw_re, pw_im = pw_re * sq_re - pw_im * sq_im, pw_re * sq_im + pw_im * sq_re
        sq_re, sq_im = sq_re * sq_re - sq_im * sq_im, 2.0 * sq_re * sq_im
        e >>= 1
    gb = S5_CBLK // p
    nb = g // gb
    eye = jnp.eye(gb, dtype=F32)

    def blockdiag_in(bb):
        x = bb.reshape(nb, gb, n, p)
        return jnp.einsum("bgnp,gh->bgphn", x, eye).reshape(nb, gb * p, gb * n)

    def blockdiag_out(c):
        x = c.astype(F32).reshape(nb, gb, p, n)
        return jnp.einsum("bgpn,gh->bgnhp", x, eye).reshape(nb, gb * n, gb * p)

    bbd = jnp.concatenate([blockdiag_in(bb_re), blockdiag_in(bb_im)], axis=2).astype(BF16)
    cbd = jnp.concatenate([blockdiag_out(c_re), -blockdiag_out(c_im)], axis=1).astype(BF16)
    a = jnp.stack([ab_re.reshape(nb, gb * n), ab_im.reshape(nb, gb * n)], axis=1)
    ap = jnp.stack([pw_re.reshape(nb, gb * n), pw_im.reshape(nb, gb * n)], axis=1)
    return bbd, cbd, a, ap


def s5_scan(u, lam_re, lam_im, log_step, b_re, b_im, c_re, c_im, d_skip):
    b, s, w = u.shape
    seg_len = s // S5_NSEG
    lt = min(S5_LT, seg_len)
    bbd, cbd, a, ap = _s5_discretise(lam_re, lam_im, log_step, b_re, b_im, c_re, c_im, seg_len)
    nb, cw, ns2 = bbd.shape
    ns = ns2 // 2
    rows = lt * S5_NSEG
    return pl.pallas_call(
        functools.partial(_s5_kernel, seg_len=seg_len, lt=lt),
        grid=(b, nb),
        in_specs=[pl.BlockSpec((1, s, cw), lambda bb, j: (bb, 0, j)),
                  pl.BlockSpec((1, cw, ns2), lambda bb, j: (j, 0, 0)),
                  pl.BlockSpec((1, ns2, cw), lambda bb, j: (j, 0, 0)),
                  pl.BlockSpec((1, 2, ns), lambda bb, j: (j, 0, 0)),
                  pl.BlockSpec((1, 2, ns), lambda bb, j: (j, 0, 0)),
                  pl.BlockSpec((1, 1, cw), lambda bb, j: (j, 0, 0))],
        out_specs=pl.BlockSpec((1, s, cw), lambda bb, j: (bb, 0, j)),
        out_shape=jax.ShapeDtypeStruct((b, s, w), F32),
        scratch_shapes=[pltpu.VMEM((rows, cw), F32), pltpu.VMEM((rows, ns2), F32),
                        pltpu.VMEM((rows, ns2), F32), pltpu.VMEM((S5_NSEG, ns2), F32)],
        compiler_params=_params(("parallel", "parallel")),
        name="s5_scan",
    )(u, bbd, cbd, a, ap, d_skip.astype(F32).reshape(nb, 1, cw))


def _even_lat_kernel(lat_ref, gq_ref, gkv_ref, cos_ref, sin_ref, qn_ref, kvn_ref, kr_ref, *, q_rank, kv_rank):
    lat = lat_ref[...]
    q_lat = lat[:, 0:q_rank]
    kv_lat = lat[:, q_rank:q_rank + kv_rank]
    kr = lat[:, q_rank + kv_rank:q_rank + kv_rank + LANES]
    rq = lax.rsqrt(jnp.mean(q_lat * q_lat, axis=-1, keepdims=True) + EPS)
    qn_ref[...] = (q_lat * rq * gq_ref[...]).astype(qn_ref.dtype)
    rkv = lax.rsqrt(jnp.mean(kv_lat * kv_lat, axis=-1, keepdims=True) + EPS)
    kvn_ref[...] = (kv_lat * rkv * gkv_ref[...]).astype(kvn_ref.dtype)
    kr = _rope_tile(kr, cos_ref[...], sin_ref[...], MLA_ROPE // 2)
    lane = lax.broadcasted_iota(jnp.int32, kr.shape, 1)
    lo = jnp.where(lane < MLA_ROPE, kr, 0.0)
    hi = pltpu.roll(lo, MLA_ROPE, 1)
    kr_ref[:, 0:LANES] = lo.astype(kr_ref.dtype)
    kr_ref[:, LANES:2 * LANES] = hi.astype(kr_ref.dtype)


def even_lat_prep(lat, gq, gkv, cos_t, sin_t):
    m, wl = lat.shape
    q_rank, kv_rank = gq.shape[0], gkv.shape[0]
    tm = _tile(m, 512, SUBLANES)
    return pl.pallas_call(
        functools.partial(_even_lat_kernel, q_rank=q_rank, kv_rank=kv_rank),
        grid=(m // tm,),
        in_specs=[pl.BlockSpec((tm, wl), lambda i: (i, 0)),
                  pl.BlockSpec((1, q_rank), lambda i: (0, 0)),
                  pl.BlockSpec((1, kv_rank), lambda i: (0, 0)),
                  pl.BlockSpec((tm, LANES), lambda i: (i, 0)),
                  pl.BlockSpec((tm, LANES), lambda i: (i, 0))],
        out_specs=[pl.BlockSpec((tm, q_rank), lambda i: (i, 0)),
                   pl.BlockSpec((tm, kv_rank), lambda i: (i, 0)),
                   pl.BlockSpec((tm, 2 * LANES), lambda i: (i, 0))],
        out_shape=[jax.ShapeDtypeStruct((m, q_rank), BF16),
                   jax.ShapeDtypeStruct((m, kv_rank), BF16),
                   jax.ShapeDtypeStruct((m, 2 * LANES), BF16)],
        compiler_params=_params(("parallel",)),
        name="even_lat_prep",
    )(lat, gq.reshape(1, -1).astype(F32), gkv.reshape(1, -1).astype(F32), cos_t, sin_t)


def _mla_pack_kernel(q_ref, kv_ref, kr_ref, cos_ref, sin_ref, qf_ref, kf_ref, vf_ref, *, heads, scale):
    cos_t, sin_t = cos_ref[...], sin_ref[...]
    nope_w = heads * MLA_NOPE
    lane = lax.broadcasted_iota(jnp.int32, (q_ref.shape[1], LANES), 1)
    for pair in range(heads // 2):
        rope = _rope_tile(q_ref[0, :, nope_w + pair * LANES:nope_w + (pair + 1) * LANES],
                          cos_t, sin_t, MLA_ROPE // 2)
        for sub in range(2):
            h = 2 * pair + sub
            keep = (lane < MLA_ROPE) if sub == 0 else (lane >= MLA_ROPE)
            qf_ref[0, h, :, 0:LANES] = (q_ref[0, :, h * MLA_NOPE:(h + 1) * MLA_NOPE] * scale).astype(qf_ref.dtype)
            qf_ref[0, h, :, LANES:2 * LANES] = (jnp.where(keep, rope, 0.0) * scale).astype(qf_ref.dtype)
            kf_ref[0, h, :, 0:LANES] = kv_ref[0, :, h * 2 * LANES:h * 2 * LANES + LANES]
            kf_ref[0, h, :, LANES:2 * LANES] = kr_ref[0, :, sub * LANES:(sub + 1) * LANES]
            v_h = kv_ref[0, :, h * 2 * LANES + LANES:(h + 1) * 2 * LANES].astype(F32)
            vf_ref[0, h, 0:MLA_V, :] = v_h.T.astype(vf_ref.dtype)
            vf_ref[0, h, MLA_V:MLA_V + V_ONES_ROWS, :] = jnp.ones((V_ONES_ROWS, v_h.shape[0]), vf_ref.dtype)


def mla_pack(q, kv, kr2, cos_t, sin_t, heads):
    b, s, _ = q.shape
    tm = _tile(s, 256, SUBLANES)
    scale = (MLA_NOPE + MLA_ROPE) ** -0.5 * LOG2E
    return pl.pallas_call(
        functools.partial(_mla_pack_kernel, heads=heads, scale=scale),
        grid=(b, s // tm),
        in_specs=[pl.BlockSpec((1, tm, q.shape[2]), lambda bb, i: (bb, i, 0)),
                  pl.BlockSpec((1, tm, kv.shape[2]), lambda bb, i: (bb, i, 0)),
                  pl.BlockSpec((1, tm, 2 * LANES), lambda bb, i: (bb, i, 0)),
                  pl.BlockSpec((tm, LANES), lambda bb, i, nt=s // tm: (bb * nt + i, 0)),
                  pl.BlockSpec((tm, LANES), lambda bb, i, nt=s // tm: (bb * nt + i, 0))],
        out_specs=[pl.BlockSpec((1, heads, tm, 2 * LANES), lambda bb, i: (bb, 0, i, 0)),
                   pl.BlockSpec((1, heads, tm, 2 * LANES), lambda bb, i: (bb, 0, i, 0)),
                   pl.BlockSpec((1, heads, MLA_V + V_ONES_ROWS, tm), lambda bb, i: (bb, 0, 0, i))],
        out_shape=[jax.ShapeDtypeStruct((b, heads, s, 2 * LANES), BF16),
                   jax.ShapeDtypeStruct((b, heads, s, 2 * LANES), BF16),
                   jax.ShapeDtypeStruct((b, heads, MLA_V + V_ONES_ROWS, s), BF16)],
        compiler_params=_params(("parallel", "parallel")),
        name="mla_pack",
    )(q, kv, kr2, cos_t, sin_t)


def even_mixer(x2, h, b, s, cos_m, sin_m, w_in, lam_re, lam_im, log_step, b_re, b_im, c_re, c_im,
               d_skip, glu_a, glu_b, q_norm_g, w_q_up, kv_norm_g, w_kv_up, w_out):
    t, d = h.shape
    s5_w = d_skip.shape[0]
    q_rank, kv_rank = q_norm_g.shape[0], kv_norm_g.shape[0]
    heads = w_kv_up.shape[1] // (MLA_NOPE + MLA_V)
    lat_w = q_rank + kv_rank + MLA_ROPE
    lat_pad = -(-(q_rank + kv_rank + LANES) // (2 * LANES)) * (2 * LANES)

    w_u = w_in[:, :s5_w].astype(BF16)
    w_lat = jnp.pad(w_in[:, s5_w:], ((0, 0), (0, lat_pad - lat_w))).astype(BF16)
    u = matmul(h, w_u, F32)
    lat = matmul(h, w_lat, F32)

    y = s5_scan(u.reshape(b, s, s5_w), lam_re, lam_im, log_step, b_re, b_im, c_re, c_im, d_skip)
    s5_out = glu_matmul(y.reshape(t, s5_w), glu_a.astype(BF16), glu_b.astype(BF16), BF16)

    qn, kvn, kr2 = even_lat_prep(lat, q_norm_g, kv_norm_g, cos_m, sin_m)
    wq = w_q_up.reshape(q_rank, heads, MLA_NOPE + MLA_ROPE)
    wq = jnp.concatenate([wq[:, :, :MLA_NOPE].reshape(q_rank, heads * MLA_NOPE),
                          wq[:, :, MLA_NOPE:].reshape(q_rank, heads * MLA_ROPE)], axis=1).astype(BF16)
    q = matmul(qn, wq, F32)
    kv = matmul(kvn, w_kv_up.astype(BF16), BF16)
    qf, kf, vf = mla_pack(q.reshape(b, s, -1), kv.reshape(b, s, -1), kr2.reshape(b, s, -1), cos_m, sin_m, heads)
    tq = 2 * _tile(s, FLASH_TKB)
    mla_out = flash_attention(qf.reshape(b, heads, s // tq, tq, 2 * LANES), kf, vf, None,
                              tq=tq, rep=1, out_dtype=BF16)
    mixed = jnp.concatenate([s5_out, mla_out.reshape(t, -1)], axis=-1)
    return matmul(mixed, w_out.astype(BF16), F32, kind="residual", res=x2)


def _odd_lat_kernel(z_ref, gq_ref, lng_ref, lnb_ref, cos_ref, sin_ref,
                    qn_ref, k_ref, v_ref, kidx_ref, w_ref, *, q_rank, w_scale):
    cos_t, sin_t = cos_ref[...], sin_ref[...]
    half = ROT_DIM // 2
    z = z_ref[0]
    q_lat = z[:, 0:q_rank]
    rq = lax.rsqrt(jnp.mean(q_lat * q_lat, axis=-1, keepdims=True) + EPS)
    qn_ref[0] = (q_lat * rq * gq_ref[...]).astype(qn_ref.dtype)
    kvh = DSA_KV_HEADS
    d = DSA_HEAD_DIM
    for g in range(kvh):
        kh = z[:, q_rank + g * d:q_rank + (g + 1) * d]
        k_ref[0, g] = _rope_tile(kh, cos_t, sin_t, half).astype(k_ref.dtype)
        v_ref[0, g, 0:d, :] = z[:, q_rank + (kvh + g) * d:q_rank + (kvh + g + 1) * d].T.astype(v_ref.dtype)
        v_ref[0, g, d:d + V_ONES_ROWS, :] = jnp.ones((V_ONES_ROWS, z.shape[0]), v_ref.dtype)
    off = q_rank + 2 * kvh * d
    ki = z[:, off:off + IDX_DIM]
    kc = ki - jnp.mean(ki, axis=-1, keepdims=True)
    var = jnp.mean(kc * kc, axis=-1, keepdims=True)
    ki = kc * lax.rsqrt(var + EPS) * lng_ref[...] + lnb_ref[...]
    kidx_ref[0] = _rope_tile(ki, cos_t, sin_t, half).astype(kidx_ref.dtype)
    w_ref[0] = (z[:, off + IDX_DIM:off + IDX_DIM + LANES] * w_scale).T


def odd_lat_prep(z, gq, ln_g, ln_b, cos_t, sin_t, idx_heads):
    b, s, zw = z.shape
    q_rank = gq.shape[0]
    tm = _tile(s, 512, SUBLANES)
    nt = s // tm
    w_scale = idx_heads ** -0.5 * IDX_DIM ** -0.5
    kvh, d = DSA_KV_HEADS, DSA_HEAD_DIM
    return pl.pallas_call(
        functools.partial(_odd_lat_kernel, q_rank=q_rank, w_scale=w_scale),
        grid=(b, nt),
        in_specs=[pl.BlockSpec((1, tm, zw), lambda bb, i: (bb, i, 0)),
                  pl.BlockSpec((1, q_rank), lambda bb, i: (0, 0)),
                  pl.BlockSpec((1, IDX_DIM), lambda bb, i: (0, 0)),
                  pl.BlockSpec((1, IDX_DIM), lambda bb, i: (0, 0)),
                  pl.BlockSpec((tm, LANES), lambda bb, i: (bb * nt + i, 0)),
                  pl.BlockSpec((tm, LANES), lambda bb, i: (bb * nt + i, 0))],
        out_specs=[pl.BlockSpec((1, tm, q_rank), lambda bb, i: (bb, i, 0)),
                   pl.BlockSpec((1, kvh, tm, d), lambda bb, i: (bb, 0, i, 0)),
                   pl.BlockSpec((1, kvh, d + V_ONES_ROWS, tm), lambda bb, i: (bb, 0, 0, i)),
                   pl.BlockSpec((1, tm, IDX_DIM), lambda bb, i: (bb, i, 0)),
                   pl.BlockSpec((1, LANES, tm), lambda bb, i: (bb, 0, i))],
        out_shape=[jax.ShapeDtypeStruct((b, s, q_rank), BF16),
                   jax.ShapeDtypeStruct((b, kvh, s, d), BF16),
                   jax.ShapeDtypeStruct((b, kvh, d + V_ONES_ROWS, s), BF16),
                   jax.ShapeDtypeStruct((b, s, IDX_DIM), BF16),
                   jax.ShapeDtypeStruct((b, LANES, s), F32)],
        compiler_params=_params(("parallel", "parallel")),
        name="odd_lat_prep",
    )(z, gq.reshape(1, -1).astype(F32), ln_g.reshape(1, -1).astype(F32),
      ln_b.reshape(1, -1).astype(F32), cos_t, sin_t)


def _dsa_q_kernel(qq_ref, cos_ref, sin_ref, q_ref, qi_ref, *, heads, idx_heads, rep, tq, scale):
    cos_t, sin_t = cos_ref[...], sin_ref[...]
    half = ROT_DIM // 2
    d = DSA_HEAD_DIM
    for h in range(heads):
        g, r = divmod(h, rep)
        xh = _rope_tile(qq_ref[0, :, h * d:(h + 1) * d], cos_t, sin_t, half)
        q_ref[0, g, 0, r * tq:(r + 1) * tq, :] = (xh * scale).astype(q_ref.dtype)
    off = heads * d
    for h in range(idx_heads):
        xh = _rope_tile(qq_ref[0, :, off + h * IDX_DIM:off + (h + 1) * IDX_DIM], cos_t, sin_t, half)
        qi_ref[0, :, h * IDX_DIM:(h + 1) * IDX_DIM] = xh.astype(qi_ref.dtype)


def dsa_q_prep(qq, cos_t, sin_t, heads, idx_heads, tq):
    b, s, wq = qq.shape
    nt = s // tq
    rep = heads // DSA_KV_HEADS
    d = DSA_HEAD_DIM
    scale = DSA_HEAD_DIM ** -0.5 * LOG2E
    return pl.pallas_call(
        functools.partial(_dsa_q_kernel, heads=heads, idx_heads=idx_heads, rep=rep, tq=tq, scale=scale),
        grid=(b, nt),
        in_specs=[pl.BlockSpec((1, tq, wq), lambda bb, i: (bb, i, 0)),
                  pl.BlockSpec((tq, LANES), lambda bb, i: (bb * nt + i, 0)),
                  pl.BlockSpec((tq, LANES), lambda bb, i: (bb * nt + i, 0))],
        out_specs=[pl.BlockSpec((1, DSA_KV_HEADS, 1, rep * tq, d), lambda bb, i: (bb, 0, i, 0, 0)),
                   pl.BlockSpec((1, tq, idx_heads * IDX_DIM), lambda bb, i: (bb, i, 0))],
        out_shape=[jax.ShapeDtypeStruct((b, DSA_KV_HEADS, nt, rep * tq, d), BF16),
                   jax.ShapeDtypeStruct((b, s, idx_heads * IDX_DIM), BF16)],
        compiler_params=_params(("parallel", "parallel")),
        name="dsa_q_prep",
    )(qq, cos_t, sin_t)


def _ukey_to_float(u):
    bits = jnp.where(u < 0, u & jnp.int32(0x7FFFFFFF), ~u)
    return lax.bitcast_convert_type(bits, F32)


def _threshold_of_key(u):
    thr = _ukey_to_float(u)
    return jnp.where((thr != thr) & (u >= 0), jnp.float32(-jnp.inf), thr)


def _indexer_kernel(qi_ref, wt_ref, kidx_ref, bias_ref, score_ref, lim_ref, *, tq, tkb, idx_heads, top_k):
    i = pl.program_id(1)
    s_len = kidx_ref.shape[1]
    n_blocks = ((i + 1) * tq + tkb - 1) // tkb
    neg_inf = jnp.float32(-jnp.inf)
    qpos = i * tq + lax.broadcasted_iota(jnp.int32, (1, tq), 1)

    def causal(start):
        kpos = start + lax.broadcasted_iota(jnp.int32, (tkb, tq), 0)
        return kpos <= qpos

    def score_block(j, c):
        start = pl.multiple_of(j * tkb, tkb)
        kb = kidx_ref[0, pl.ds(start, tkb), :]
        acc = jnp.zeros((tkb, tq), F32)
        for h in range(idx_heads):
            logits = lax.dot_general(kb, qi_ref[0, :, h * IDX_DIM:(h + 1) * IDX_DIM],
                                     (((1,), (1,)), ((), ())), preferred_element_type=F32)
            acc = acc + jnp.maximum(logits, 0.0) * wt_ref[0, h:h + 1, :]
        score_ref[pl.ds(start, tkb), :] = jnp.where(causal(start), acc, neg_inf)
        return c

    lax.fori_loop(0, n_blocks, score_block, 0)

    def count_where(pred):
        def body(j, cnt):
            start = pl.multiple_of(j * tkb, tkb)
            ind = jnp.where(pred(score_ref[pl.ds(start, tkb), :], start), 1.0, 0.0)
            part = jnp.sum(ind.reshape(tkb // (8 * SUBLANES), 8, SUBLANES, tq), axis=1)
            return cnt + jnp.sum(part, axis=0)
        cnt = lax.fori_loop(0, n_blocks, body, jnp.zeros((SUBLANES, tq), F32))
        return jnp.sum(cnt, axis=0, keepdims=True)

    def count_ge(thr):
        return count_where(lambda blk, start: blk >= thr)

    k_f = jnp.float32(top_k)
    few = qpos < top_k

    def search_cond(st):
        bi, _, cnt_u = st
        pending = jnp.where(few | (cnt_u == k_f), 0.0, 1.0)
        return (bi < 32) & (jnp.max(pending) > 0.0)

    def search_step(st):
        bi, u, cnt_u = st
        cand = u | (jnp.int32(1) << (31 - bi))
        cnt = count_ge(_threshold_of_key(cand))
        take = cnt >= k_f
        return bi + 1, jnp.where(take, cand, u), jnp.where(take, cnt, cnt_u)

    total = (n_blocks * tkb).astype(F32)
    _, u, cnt_u = lax.while_loop(search_cond, search_step,
                                 (jnp.int32(0), jnp.zeros((1, tq), jnp.int32), jnp.full((1, tq), total, F32)))
    thr = jnp.where(few, neg_inf, _threshold_of_key(u))

    def key_pos(start):
        return start + lax.broadcasted_iota(jnp.int32, (tkb, tq), 0)

    excess = jnp.logical_not(few) & (cnt_u > k_f)
    lim_ref[...] = jnp.full((1, tq), s_len, jnp.int32)

    @pl.when(jnp.max(jnp.where(excess, 1.0, 0.0)) > 0.0)
    def _():
        need = k_f - count_where(lambda blk, start: blk > thr)
        nbits = s_len.bit_length()

        def tie_step(bi, p):
            cand = p | (jnp.int32(1) << (nbits - 1 - bi))
            cnt = count_where(lambda blk, start: (blk == thr) & (key_pos(start) < cand))
            return jnp.where(cnt <= need, cand, p)

        p = lax.fori_loop(0, nbits, tie_step, jnp.zeros((1, tq), jnp.int32))
        lim_ref[...] = jnp.where(excess, p, s_len)

    lim = lim_ref[...]

    def write_block(j, c):
        start = pl.multiple_of(j * tkb, tkb)
        blk = score_ref[pl.ds(start, tkb), :]
        keep = ((blk > thr) | ((blk == thr) & (key_pos(start) < lim))) & causal(start)
        bias_ref[0, pl.ds(start, tkb), :] = jnp.where(keep, 0.0, MASK_VALUE).astype(bias_ref.dtype)
        return c

    lax.fori_loop(0, n_blocks, write_block, 0)

    def fill_block(j, c):
        start = pl.multiple_of(j * tkb, tkb)
        bias_ref[0, pl.ds(start, tkb), :] = jnp.full((tkb, tq), MASK_VALUE, bias_ref.dtype)
        return c

    lax.fori_loop(n_blocks, s_len // tkb, fill_block, 0)


def dsa_indexer(qi, wt, kidx, idx_heads, top_k):
    b, s, _ = qi.shape
    tq = _tile(s, 256)
    tkb = _tile(s, 512)
    assert tkb >= top_k or tkb == s
    return pl.pallas_call(
        functools.partial(_indexer_kernel, tq=tq, tkb=tkb, idx_heads=idx_heads, top_k=top_k),
        grid=(b, s // tq),
        in_specs=[pl.BlockSpec((1, tq, idx_heads * IDX_DIM), lambda bb, i: (bb, i, 0)),
                  pl.BlockSpec((1, LANES, tq), lambda bb, i: (bb, 0, i)),
                  pl.BlockSpec((1, s, IDX_DIM), lambda bb, i: (bb, 0, 0))],
        out_specs=pl.BlockSpec((1, s, tq), lambda bb, i: (bb, 0, i)),
        out_shape=jax.ShapeDtypeStruct((b, s, s), BF16),
        scratch_shapes=[pltpu.VMEM((s, tq), F32), pltpu.VMEM((1, tq), jnp.int32)],
        compiler_params=_params(("parallel", "parallel")),
        name="dsa_indexer",
    )(qi, wt, kidx)


def odd_mixer(x2, h, b, s, cos_p, sin_p, w_in, q_norm_g, w_q_up, w_idx_q, k_ln_g, k_ln_b, w_out):
    t, d = h.shape
    q_rank = q_norm_g.shape[0]
    heads = w_q_up.shape[1] // DSA_HEAD_DIM
    idx_heads = w_idx_q.shape[1] // IDX_DIM
    in_w = w_in.shape[1]
    z_w = -(-(in_w - idx_heads + LANES) // (2 * LANES)) * (2 * LANES)
    w_z = jnp.pad(w_in, ((0, 0), (0, z_w - in_w))).astype(BF16)
    z = matmul(h, w_z, F32)
    qn, k, vt, kidx, wt = odd_lat_prep(z.reshape(b, s, z_w), q_norm_g, k_ln_g, k_ln_b, cos_p, sin_p, idx_heads)
    w_qq = jnp.concatenate([w_q_up, w_idx_q], axis=1).astype(BF16)
    qq = matmul(qn.reshape(t, q_rank), w_qq, F32)
    tq = _tile(s, 256)
    q, qi = dsa_q_prep(qq.reshape(b, s, -1), cos_p, sin_p, heads, idx_heads, tq)
    top_k = min(IDX_TOPK_MAX, s // 4)
    bias = dsa_indexer(qi, wt, kidx, idx_heads, top_k)
    o = flash_attention(q, k, vt, bias, tq=tq, rep=heads // DSA_KV_HEADS, out_dtype=BF16)
    return matmul(o.reshape(t, -1), w_out.astype(BF16), F32, kind="residual", res=x2)


def sq_relu_mlp(x2, h, w_up, w_down, layer):
    a = matmul(h, w_up, BF16, kind="relu2", layer=layer)
    return matmul(a, w_down, F32, kind="residual", res=x2, layer=layer)


def kernel(x, positions, norm_mix_g, norm_mlp_g, final_norm_g, even_w_in, s5_lam_re, s5_lam_im, s5_log_step, s5_b_re, s5_b_im, s5_c_re, s5_c_im, s5_d, s5_glu_a, s5_glu_b, mla_q_norm_g, mla_w_q_up, mla_kv_norm_g, mla_w_kv_up, even_w_out, odd_w_in, dsa_q_norm_g, dsa_w_q_up, idx_w_q, idx_k_ln_g, idx_k_ln_b, odd_w_out, mlp_w_up, mlp_w_down):
    b, s, d = x.shape
    depth = norm_mix_g.shape[0]
    cos_m, sin_m = _rope_tables(positions, MLA_ROPE, LANES)
    cos_p, sin_p = _rope_tables(positions, ROT_DIM, ROT_DIM)
    x2 = x.reshape(b * s, d)
    for layer in range(depth):
        i = layer // 2
        h = rmsnorm(x2, norm_mix_g[layer], BF16)
        if layer % 2 == 0:
            x2 = even_mixer(x2, h, b, s, cos_m, sin_m, even_w_in[i], s5_lam_re[i], s5_lam_im[i],
                            s5_log_step[i], s5_b_re[i], s5_b_im[i], s5_c_re[i], s5_c_im[i], s5_d[i],
                            s5_glu_a[i], s5_glu_b[i], mla_q_norm_g[i], mla_w_q_up[i],
                            mla_kv_norm_g[i], mla_w_kv_up[i], even_w_out[i])
        else:
            x2 = odd_mixer(x2, h, b, s, cos_p, sin_p, odd_w_in[i], dsa_q_norm_g[i], dsa_w_q_up[i],
                           idx_w_q[i], idx_k_ln_g[i], idx_k_ln_b[i], odd_w_out[i])
        h = rmsnorm(x2, norm_mlp_g[layer], BF16)
        x2 = sq_relu_mlp(x2, h, mlp_w_up, mlp_w_down, layer)
    return rmsnorm(x2, final_norm_g, x.dtype).reshape(b, s, d)
```

```python
import functools
import math

import jax
import jax.numpy as jnp
from jax import lax
from jax.experimental import pallas as pl
from jax.experimental.pallas import tpu as pltpu

F32 = jnp.float32
BF16 = jnp.bfloat16

EPS = 1e-6
ROPE_THETA = 500000.0
S5_GROUP = 16
S5_LAMBDA_RE_MAX = -1e-4
MLA_NOPE = 128
MLA_ROPE = 64
MLA_V = 128
DSA_HEAD_DIM = 128
DSA_KV_HEADS = 4
IDX_DIM = 128
IDX_TOPK_MAX = 256
ROT_DIM = DSA_HEAD_DIM // 4

LANES = 128
SUBLANES = 8
VMEM_LIMIT_BYTES = 56 * 1024 * 1024
MASK_VALUE = -1e30


def _params(semantics):
    return pltpu.CompilerParams(dimension_semantics=semantics, vmem_limit_bytes=VMEM_LIMIT_BYTES)


def _tile(dim, pref, align=LANES):
    if dim <= pref:
        return dim
    t = (pref // align) * align
    while t >= align:
        if dim % t == 0:
            return t
        t -= align
    return dim


def _rmsnorm_kernel(x_ref, g_ref, o_ref):
    x = x_ref[...].astype(F32)
    r = lax.rsqrt(jnp.mean(x * x, axis=-1, keepdims=True) + EPS)
    o_ref[...] = (x * r * g_ref[...]).astype(o_ref.dtype)


def rmsnorm(x, g, out_dtype):
    m, d = x.shape
    tm = _tile(m, 256, SUBLANES)
    return pl.pallas_call(
        _rmsnorm_kernel,
        grid=(m // tm,),
        in_specs=[pl.BlockSpec((tm, d), lambda i: (i, 0)),
                  pl.BlockSpec((1, d), lambda i: (0, 0))],
        out_specs=pl.BlockSpec((tm, d), lambda i: (i, 0)),
        out_shape=jax.ShapeDtypeStruct((m, d), out_dtype),
        compiler_params=_params(("parallel",)),
        name="rmsnorm",
    )(x, g.reshape(1, d).astype(F32))


def _mm_epilogue(acc, kind, res_ref):
    if kind == "relu2":
        a = jnp.maximum(acc, 0.0)
        return a * a
    if kind == "residual":
        return res_ref[...].astype(F32) + acc
    return acc


def _mm_kernel_single(*refs, kind):
    if kind == "residual":
        a_ref, w_ref, res_ref, o_ref = refs
    else:
        a_ref, w_ref, o_ref = refs
        res_ref = None
    acc = jnp.dot(a_ref[...].astype(BF16), w_ref[...].astype(BF16), preferred_element_type=F32)
    o_ref[...] = _mm_epilogue(acc, kind, res_ref).astype(o_ref.dtype)


def _mm_kernel_ksplit(*refs, kind, nk):
    if kind == "residual":
        a_ref, w_ref, res_ref, o_ref, acc_ref = refs
    else:
        a_ref, w_ref, o_ref, acc_ref = refs
        res_ref = None
    k = pl.program_id(2)

    @pl.when(k == 0)
    def _():
        acc_ref[...] = jnp.zeros_like(acc_ref)

    acc_ref[...] += jnp.dot(a_ref[...].astype(BF16), w_ref[...].astype(BF16), preferred_element_type=F32)

    @pl.when(k == nk - 1)
    def _():
        o_ref[...] = _mm_epilogue(acc_ref[...], kind, res_ref).astype(o_ref.dtype)


def matmul(a, w, out_dtype, kind="none", res=None, layer=None, tiles=None):
    m, k = a.shape
    n = w.shape[-1]
    if tiles is None:
        tiles = (1024, 512, k) if k <= 4096 else (1024, 1024, 2048)
    tm = _tile(m, tiles[0], SUBLANES)
    tn = _tile(n, tiles[1])
    tk = _tile(k, tiles[2])

    def w_spec(tk_, tn_, index):
        if layer is None:
            return pl.BlockSpec((tk_, tn_), index)
        return pl.BlockSpec((None, tk_, tn_), lambda *g: (layer,) + index(*g))

    if tk == k:
        in_specs = [pl.BlockSpec((tm, k), lambda i, j: (i, 0)),
                    w_spec(k, tn, lambda i, j: (0, j))]
        args = [a, w]
        if kind == "residual":
            in_specs.append(pl.BlockSpec((tm, tn), lambda i, j: (i, j)))
            args.append(res)
        return pl.pallas_call(
            functools.partial(_mm_kernel_single, kind=kind),
            grid=(m // tm, n // tn),
            in_specs=in_specs,
            out_specs=pl.BlockSpec((tm, tn), lambda i, j: (i, j)),
            out_shape=jax.ShapeDtypeStruct((m, n), out_dtype),
            compiler_params=_params(("parallel", "parallel")),
            name="matmul",
        )(*args)
    nk = k // tk
    in_specs = [pl.BlockSpec((tm, tk), lambda i, j, kk: (i, kk)),
                w_spec(tk, tn, lambda i, j, kk: (kk, j))]
    args = [a, w]
    if kind == "residual":
        in_specs.append(pl.BlockSpec((tm, tn), lambda i, j, kk: (i, j)))
        args.append(res)
    return pl.pallas_call(
        functools.partial(_mm_kernel_ksplit, kind=kind, nk=nk),
        grid=(m // tm, n // tn, nk),
        in_specs=in_specs,
        out_specs=pl.BlockSpec((tm, tn), lambda i, j, kk: (i, j)),
        out_shape=jax.ShapeDtypeStruct((m, n), out_dtype),
        scratch_shapes=[pltpu.VMEM((tm, tn), F32)],
        compiler_params=_params(("parallel", "parallel", "arbitrary")),
        name="matmul_ksplit",
    )(*args)


def _glu_kernel(a_ref, wa_ref, wb_ref, o_ref):
    a = a_ref[...].astype(BF16)
    ya = jnp.dot(a, wa_ref[...], preferred_element_type=F32)
    yb = jnp.dot(a, wb_ref[...], preferred_element_type=F32)
    o_ref[...] = (ya * jax.nn.sigmoid(yb)).astype(o_ref.dtype)


def glu_matmul(a, wa, wb, out_dtype):
    m, k = a.shape
    _, n = wa.shape
    tm = _tile(m, 1024, SUBLANES)
    tn = _tile(n, 512)
    return pl.pallas_call(
        _glu_kernel,
        grid=(m // tm, n // tn),
        in_specs=[pl.BlockSpec((tm, k), lambda i, j: (i, 0)),
                  pl.BlockSpec((k, tn), lambda i, j: (0, j)),
                  pl.BlockSpec((k, tn), lambda i, j: (0, j))],
        out_specs=pl.BlockSpec((tm, tn), lambda i, j: (i, j)),
        out_shape=jax.ShapeDtypeStruct((m, n), out_dtype),
        compiler_params=_params(("parallel", "parallel")),
        name="glu_matmul",
    )(a, wa, wb)


def _rope_tile(x, cos_t, sin_t, half):
    lane = lax.broadcasted_iota(jnp.int32, x.shape, 1)
    first = (lane % (2 * half)) < half
    partner = jnp.where(first, pltpu.roll(x, LANES - half, 1), pltpu.roll(x, half, 1))
    return x * cos_t + partner * sin_t


def _rope_tables(positions, dim, pad_to):
    inv_freq = ROPE_THETA ** (-jnp.arange(0, dim, 2, dtype=F32) / dim)
    ang = positions.astype(F32).reshape(-1)[:, None] * inv_freq
    c, s = jnp.cos(ang), jnp.sin(ang)
    reps = pad_to // dim
    cos_t = jnp.tile(jnp.concatenate([c, c], axis=-1), (1, reps))
    sin_t = jnp.tile(jnp.concatenate([-s, s], axis=-1), (1, reps))
    t = c.shape[0]
    cos_t = jnp.concatenate([cos_t, jnp.ones((t, LANES - pad_to), F32)], axis=-1)
    sin_t = jnp.concatenate([sin_t, jnp.zeros((t, LANES - pad_to), F32)], axis=-1)
    return cos_t, sin_t


V_ONES_ROWS = 16
FLASH_TKB = 512
LOG2E = math.log2(math.e)


def _flash_kernel(*refs, tq, tkb, rep, dv, has_bias):
    if has_bias:
        q_ref, k_ref, vt_ref, b_ref, o_ref, m_ref, acc_ref, sa_ref, sb_ref = refs
    else:
        q_ref, k_ref, vt_ref, o_ref, m_ref, acc_ref, sa_ref, sb_ref = refs
        b_ref = None
    i = pl.program_id(2)
    rows = rep * tq

    m_ref[...] = jnp.full(m_ref.shape, MASK_VALUE, F32)
    acc_ref[...] = jnp.zeros(acc_ref.shape, F32)
    q = q_ref[0, 0, 0]

    def qk(j, dst_ref):
        start = pl.multiple_of(j * tkb, tkb)
        kb = k_ref[0, 0, pl.ds(start, tkb), :]
        dst_ref[...] = lax.dot_general(kb, q, (((1,), (1,)), ((), ())),
                                       preferred_element_type=F32)

    def softmax_pv(j, src_ref, masked):
        start = pl.multiple_of(j * tkb, tkb)
        vtb = vt_ref[0, 0, :, pl.ds(start, tkb)]
        s = src_ref[...]
        if has_bias:
            bias = b_ref[0, pl.ds(start, tkb), :].astype(F32)
            s = s + jnp.tile(bias, (1, rep))
        if masked:
            kpos = start + lax.broadcasted_iota(jnp.int32, (tkb, rows), 0)
            qlane = lax.broadcasted_iota(jnp.int32, (tkb, rows), 1)
            qpos = i * tq + (qlane % tq if rep > 1 else qlane)
            s = jnp.where(kpos <= qpos, s, MASK_VALUE)
        m_prev = m_ref[...]
        m_new = jnp.maximum(m_prev, jnp.max(s, axis=0, keepdims=True))
        p = jnp.exp2(s - m_new)
        alpha = jnp.exp2(m_prev - m_new)
        acc_ref[...] = alpha * acc_ref[...] + jnp.dot(vtb, p.astype(BF16), preferred_element_type=F32)
        m_ref[...] = m_new

    qk(0, sa_ref)
    if has_bias:
        n_blocks = ((i + 1) * tq + tkb - 1) // tkb

        def pair(jj, c):
            j0 = 2 * jj
            qk(j0 + 1, sb_ref)
            softmax_pv(j0, sa_ref, False)
            qk(jnp.minimum(j0 + 2, n_blocks - 1), sa_ref)
            softmax_pv(j0 + 1, sb_ref, False)
            return c

        lax.fori_loop(0, n_blocks // 2, pair, 0)

        @pl.when(n_blocks % 2 == 1)
        def _():
            softmax_pv(n_blocks - 1, sa_ref, False)
    else:
        def pair(jj, c):
            j0 = 2 * jj
            qk(j0 + 1, sb_ref)
            softmax_pv(j0, sa_ref, False)
            qk(j0 + 2, sa_ref)
            softmax_pv(j0 + 1, sb_ref, False)
            return c

        lax.fori_loop(0, i, pair, 0)
        qk(2 * i + 1, sb_ref)
        softmax_pv(2 * i, sa_ref, True)
        softmax_pv(2 * i + 1, sb_ref, True)

    acc = acc_ref[...]
    out = (acc[0:dv] * (1.0 / acc[dv:dv + 1])).T
    for r in range(rep):
        o_ref[0, :, r * dv:(r + 1) * dv] = out[r * tq:(r + 1) * tq].astype(o_ref.dtype)


def flash_attention(q, k, vt, bias, *, tq, rep, out_dtype):
    b, hk, nq, rows, dq = q.shape
    s = k.shape[2]
    dve = vt.shape[2]
    dv = dve - V_ONES_ROWS
    tkb = _tile(s, FLASH_TKB)
    has_bias = bias is not None
    assert has_bias or tq == 2 * tkb
    in_specs = [pl.BlockSpec((1, 1, 1, rows, dq), lambda bb, h, i: (bb, h, i, 0, 0)),
                pl.BlockSpec((1, 1, s, dq), lambda bb, h, i: (bb, h, 0, 0)),
                pl.BlockSpec((1, 1, dve, s), lambda bb, h, i: (bb, h, 0, 0))]
    args = [q, k, vt]
    if has_bias:
        in_specs.append(pl.BlockSpec((1, s, tq), lambda bb, h, i: (bb, 0, i)))
        args.append(bias)
    return pl.pallas_call(
        functools.partial(_flash_kernel, tq=tq, tkb=tkb, rep=rep, dv=dv, has_bias=has_bias),
        grid=(b, hk, nq),
        in_specs=in_specs,
        out_specs=pl.BlockSpec((1, tq, rep * dv), lambda bb, h, i: (bb, i, h)),
        out_shape=jax.ShapeDtypeStruct((b, s, hk * rep * dv), out_dtype),
        scratch_shapes=[pltpu.VMEM((1, rows), F32), pltpu.VMEM((dve, rows), F32),
                        pltpu.VMEM((tkb, rows), F32), pltpu.VMEM((tkb, rows), F32)],
        compiler_params=_params(("parallel", "parallel", "parallel")),
        name="flash_attention",
    )(*args)


S5_NSEG = 2 * SUBLANES
S5_CBLK = LANES
S5_LT = 16


def _s5_kernel(u_ref, bbd_ref, cbd_ref, a_ref, ap_ref, d_ref, y_ref,
               ug_ref, bu_ref, xs_ref, init_ref, *, seg_len, lt):
    nseg = S5_NSEG
    ns = a_ref.shape[-1]
    a_re = jnp.broadcast_to(a_ref[0, 0:1, :], (nseg, ns))
    a_im = jnp.broadcast_to(a_ref[0, 1:2, :], (nseg, ns))
    ntiles = seg_len // lt

    def load_inputs(t, slot):
        for i in range(lt):
            ug_ref[slot, i * nseg:(i + 1) * nseg, :] = u_ref[0, pl.ds(t * lt + i, nseg, stride=seg_len), :]
        bu_ref[slot] = jnp.dot(ug_ref[slot].astype(BF16), bbd_ref[0], preferred_element_type=F32)

    def scan_tile(carry, slot, store):
        x_re, x_im = carry
        for i in range(lt):
            r0 = i * nseg
            b_re = bu_ref[slot, r0:r0 + nseg, 0:ns]
            b_im = bu_ref[slot, r0:r0 + nseg, ns:2 * ns]
            n_re = a_re * x_re - a_im * x_im + b_re
            n_im = a_re * x_im + a_im * x_re + b_im
            x_re, x_im = n_re, n_im
            if store:
                xs_ref[slot, r0:r0 + nseg, 0:ns] = x_re
                xs_ref[slot, r0:r0 + nseg, ns:2 * ns] = x_im
        return x_re, x_im

    def emit(t, slot):
        y = jnp.dot(xs_ref[slot].astype(BF16), cbd_ref[0], preferred_element_type=F32)
        y = jax.nn.gelu(y + d_ref[0] * ug_ref[slot])
        for i in range(lt):
            y_ref[0, pl.ds(t * lt + i, nseg, stride=seg_len), :] = y[i * nseg:(i + 1) * nseg]

    def sweep(carry, store):
        load_inputs(0, 0)

        def pair(tt, c):
            t0 = 2 * tt
            load_inputs(t0 + 1, 1)
            c = scan_tile(c, 0, store)
            if store:
                emit(t0, 0)
            load_inputs(jnp.minimum(t0 + 2, ntiles - 1), 0)
            c = scan_tile(c, 1, store)
            if store:
                emit(t0 + 1, 1)
            return c

        return lax.fori_loop(0, ntiles // 2, pair, carry)

    zeros = jnp.zeros((nseg, ns), F32)
    e_re, e_im = sweep((zeros, zeros), False)

    ap_re = ap_ref[0, 0:1, :]
    ap_im = ap_ref[0, 1:2, :]
    t_re = jnp.zeros((1, ns), F32)
    t_im = jnp.zeros((1, ns), F32)
    init_ref[0:1, :] = jnp.zeros((1, 2 * ns), F32)
    for s in range(1, nseg):
        p_re, p_im = e_re[s - 1:s], e_im[s - 1:s]
        t_re, t_im = (ap_re * t_re - ap_im * t_im + p_re,
                      ap_re * t_im + ap_im * t_re + p_im)
        init_ref[s:s + 1, 0:ns] = t_re
        init_ref[s:s + 1, ns:2 * ns] = t_im

    sweep((init_ref[:, 0:ns], init_ref[:, ns:2 * ns]), True)


def _s5_discretise(lam_re, lam_im, log_step, b_re, b_im, c_re, c_im, seg_len):
    g, n = lam_re.shape
    p = S5_GROUP
    lr = jnp.minimum(lam_re.astype(F32), S5_LAMBDA_RE_MAX)
    li = lam_im.astype(F32)
    step = jnp.exp(log_step.astype(F32))[:, None]
    mag = jnp.exp(lr * step)
    ab_re = mag * jnp.cos(li * step)
    ab_im = mag * jnp.sin(li * step)
    den = lr * lr + li * li
    f_re = ((ab_re - 1.0) * lr + ab_im * li) / den
    f_im = (ab_im * lr - (ab_re - 1.0) * li) / den
    br = b_re.astype(F32)
    bi = b_im.astype(F32)
    bb_re = f_re[..., None] * br - f_im[..., None] * bi
    bb_im = f_re[..., None] * bi + f_im[..., None] * br
    pw_re, pw_im = jnp.ones_like(ab_re), jnp.zeros_like(ab_im)
    sq_re, sq_im = ab_re, ab_im
    e = seg_len
    while e:
        if e & 1:
            pw_re, pw_im = pw_re * sq_re - pw_im * sq_im, pw_re * sq_im + pw_im * sq_re
        sq_re, sq_im = sq_re * sq_re - sq_im * sq_im, 2.0 * sq_re * sq_im
        e >>= 1
    gb = S5_CBLK // p
    nb = g // gb
    eye = jnp.eye(gb, dtype=F32)

    def blockdiag_in(bb):
        x = bb.reshape(nb, gb, n, p)
        return jnp.einsum("bgnp,gh->bgphn", x, eye).reshape(nb, gb * p, gb * n)

    def blockdiag_out(c):
        x = c.astype(F32).reshape(nb, gb, p, n)
        return jnp.einsum("bgpn,gh->bgnhp", x, eye).reshape(nb, gb * n, gb * p)

    bbd = jnp.concatenate([blockdiag_in(bb_re), blockdiag_in(bb_im)], axis=2).astype(BF16)
    cbd = jnp.concatenate([blockdiag_out(c_re), -blockdiag_out(c_im)], axis=1).astype(BF16)
    a = jnp.stack([ab_re.reshape(nb, gb * n), ab_im.reshape(nb, gb * n)], axis=1)
    ap = jnp.stack([pw_re.reshape(nb, gb * n), pw_im.reshape(nb, gb * n)], axis=1)
    return bbd, cbd, a, ap


def s5_scan(u, lam_re, lam_im, log_step, b_re, b_im, c_re, c_im, d_skip):
    b, s, w = u.shape
    seg_len = s // S5_NSEG
    lt = min(S5_LT, seg_len)
    assert (seg_len // lt) % 2 == 0
    bbd, cbd, a, ap = _s5_discretise(lam_re, lam_im, log_step, b_re, b_im, c_re, c_im, seg_len)
    nb, cw, ns2 = bbd.shape
    ns = ns2 // 2
    rows = lt * S5_NSEG
    return pl.pallas_call(
        functools.partial(_s5_kernel, seg_len=seg_len, lt=lt),
        grid=(b, nb),
        in_specs=[pl.BlockSpec((1, s, cw), lambda bb, j: (bb, 0, j)),
                  pl.BlockSpec((1, cw, ns2), lambda bb, j: (j, 0, 0)),
                  pl.BlockSpec((1, ns2, cw), lambda bb, j: (j, 0, 0)),
                  pl.BlockSpec((1, 2, ns), lambda bb, j: (j, 0, 0)),
                  pl.BlockSpec((1, 2, ns), lambda bb, j: (j, 0, 0)),
                  pl.BlockSpec((1, 1, cw), lambda bb, j: (j, 0, 0))],
        out_specs=pl.BlockSpec((1, s, cw), lambda bb, j: (bb, 0, j)),
        out_shape=jax.ShapeDtypeStruct((b, s, w), F32),
        scratch_shapes=[pltpu.VMEM((2, rows, cw), F32), pltpu.VMEM((2, rows, ns2), F32),
                        pltpu.VMEM((2, rows, ns2), F32), pltpu.VMEM((S5_NSEG, ns2), F32)],
        compiler_params=_params(("parallel", "parallel")),
        name="s5_scan",
    )(u, bbd, cbd, a, ap, d_skip.astype(F32).reshape(nb, 1, cw))


def _even_lat_kernel(lat_ref, gq_ref, gkv_ref, cos_ref, sin_ref, qn_ref, kvn_ref, kr_ref, *, q_rank, kv_rank):
    lat = lat_ref[...]
    q_lat = lat[:, 0:q_rank]
    kv_lat = lat[:, q_rank:q_rank + kv_rank]
    kr = lat[:, q_rank + kv_rank:q_rank + kv_rank + LANES]
    rq = lax.rsqrt(jnp.mean(q_lat * q_lat, axis=-1, keepdims=True) + EPS)
    qn_ref[...] = (q_lat * rq * gq_ref[...]).astype(qn_ref.dtype)
    rkv = lax.rsqrt(jnp.mean(kv_lat * kv_lat, axis=-1, keepdims=True) + EPS)
    kvn_ref[...] = (kv_lat * rkv * gkv_ref[...]).astype(kvn_ref.dtype)
    kr = _rope_tile(kr, cos_ref[...], sin_ref[...], MLA_ROPE // 2)
    lane = lax.broadcasted_iota(jnp.int32, kr.shape, 1)
    lo = jnp.where(lane < MLA_ROPE, kr, 0.0)
    hi = pltpu.roll(lo, MLA_ROPE, 1)
    kr_ref[:, 0:LANES] = lo.astype(kr_ref.dtype)
    kr_ref[:, LANES:2 * LANES] = hi.astype(kr_ref.dtype)


def even_lat_prep(lat, gq, gkv, cos_t, sin_t):
    m, wl = lat.shape
    q_rank, kv_rank = gq.shape[0], gkv.shape[0]
    tm = _tile(m, 512, SUBLANES)
    return pl.pallas_call(
        functools.partial(_even_lat_kernel, q_rank=q_rank, kv_rank=kv_rank),
        grid=(m // tm,),
        in_specs=[pl.BlockSpec((tm, wl), lambda i: (i, 0)),
                  pl.BlockSpec((1, q_rank), lambda i: (0, 0)),
                  pl.BlockSpec((1, kv_rank), lambda i: (0, 0)),
                  pl.BlockSpec((tm, LANES), lambda i: (i, 0)),
                  pl.BlockSpec((tm, LANES), lambda i: (i, 0))],
        out_specs=[pl.BlockSpec((tm, q_rank), lambda i: (i, 0)),
                   pl.BlockSpec((tm, kv_rank), lambda i: (i, 0)),
                   pl.BlockSpec((tm, 2 * LANES), lambda i: (i, 0))],
        out_shape=[jax.ShapeDtypeStruct((m, q_rank), BF16),
                   jax.ShapeDtypeStruct((m, kv_rank), BF16),
                   jax.ShapeDtypeStruct((m, 2 * LANES), BF16)],
        compiler_params=_params(("parallel",)),
        name="even_lat_prep",
    )(lat, gq.reshape(1, -1).astype(F32), gkv.reshape(1, -1).astype(F32), cos_t, sin_t)


def _mla_pack_kernel(q_ref, kv_ref, kr_ref, cos_ref, sin_ref, qf_ref, kf_ref, vf_ref, *, heads, scale):
    cos_t, sin_t = cos_ref[...], sin_ref[...]
    nope_w = heads * MLA_NOPE
    lane = lax.broadcasted_iota(jnp.int32, (q_ref.shape[1], LANES), 1)
    for pair in range(heads // 2):
        rope = _rope_tile(q_ref[0, :, nope_w + pair * LANES:nope_w + (pair + 1) * LANES],
                          cos_t, sin_t, MLA_ROPE // 2)
        for sub in range(2):
            h = 2 * pair + sub
            keep = (lane < MLA_ROPE) if sub == 0 else (lane >= MLA_ROPE)
            qf_ref[0, h, :, 0:LANES] = (q_ref[0, :, h * MLA_NOPE:(h + 1) * MLA_NOPE] * scale).astype(qf_ref.dtype)
            qf_ref[0, h, :, LANES:2 * LANES] = (jnp.where(keep, rope, 0.0) * scale).astype(qf_ref.dtype)
            kf_ref[0, h, :, 0:LANES] = kv_ref[0, :, h * 2 * LANES:h * 2 * LANES + LANES]
            kf_ref[0, h, :, LANES:2 * LANES] = kr_ref[0, :, sub * LANES:(sub + 1) * LANES]
            v_h = kv_ref[0, :, h * 2 * LANES + LANES:(h + 1) * 2 * LANES].astype(F32)
            vf_ref[0, h, 0:MLA_V, :] = v_h.T.astype(vf_ref.dtype)
            vf_ref[0, h, MLA_V:MLA_V + V_ONES_ROWS, :] = jnp.ones((V_ONES_ROWS, v_h.shape[0]), vf_ref.dtype)


def mla_pack(q, kv, kr2, cos_t, sin_t, heads):
    b, s, _ = q.shape
    tm = _tile(s, 256, SUBLANES)
    scale = (MLA_NOPE + MLA_ROPE) ** -0.5 * LOG2E
    return pl.pallas_call(
        functools.partial(_mla_pack_kernel, heads=heads, scale=scale),
        grid=(b, s // tm),
        in_specs=[pl.BlockSpec((1, tm, q.shape[2]), lambda bb, i: (bb, i, 0)),
                  pl.BlockSpec((1, tm, kv.shape[2]), lambda bb, i: (bb, i, 0)),
                  pl.BlockSpec((1, tm, 2 * LANES), lambda bb, i: (bb, i, 0)),
                  pl.BlockSpec((tm, LANES), lambda bb, i, nt=s // tm: (bb * nt + i, 0)),
                  pl.BlockSpec((tm, LANES), lambda bb, i, nt=s // tm: (bb * nt + i, 0))],
        out_specs=[pl.BlockSpec((1, heads, tm, 2 * LANES), lambda bb, i: (bb, 0, i, 0)),
                   pl.BlockSpec((1, heads, tm, 2 * LANES), lambda bb, i: (bb, 0, i, 0)),
                   pl.BlockSpec((1, heads, MLA_V + V_ONES_ROWS, tm), lambda bb, i: (bb, 0, 0, i))],
        out_shape=[jax.ShapeDtypeStruct((b, heads, s, 2 * LANES), BF16),
                   jax.ShapeDtypeStruct((b, heads, s, 2 * LANES), BF16),
                   jax.ShapeDtypeStruct((b, heads, MLA_V + V_ONES_ROWS, s), BF16)],
        compiler_params=_params(("parallel", "parallel")),
        name="mla_pack",
    )(q, kv, kr2, cos_t, sin_t)


def even_mixer(x2, h, b, s, cos_m, sin_m, w_in, lam_re, lam_im, log_step, b_re, b_im, c_re, c_im,
               d_skip, glu_a, glu_b, q_norm_g, w_q_up, kv_norm_g, w_kv_up, w_out):
    t, d = h.shape
    s5_w = d_skip.shape[0]
    q_rank, kv_rank = q_norm_g.shape[0], kv_norm_g.shape[0]
    heads = w_kv_up.shape[1] // (MLA_NOPE + MLA_V)
    lat_w = q_rank + kv_rank + MLA_ROPE
    lat_pad = -(-(q_rank + kv_rank + LANES) // (2 * LANES)) * (2 * LANES)

    w_u = w_in[:, :s5_w].astype(BF16)
    w_lat = jnp.pad(w_in[:, s5_w:], ((0, 0), (0, lat_pad - lat_w))).astype(BF16)
    u = matmul(h, w_u, F32)
    lat = matmul(h, w_lat, F32)

    y = s5_scan(u.reshape(b, s, s5_w), lam_re, lam_im, log_step, b_re, b_im, c_re, c_im, d_skip)
    s5_out = glu_matmul(y.reshape(t, s5_w), glu_a.astype(BF16), glu_b.astype(BF16), BF16)

    qn, kvn, kr2 = even_lat_prep(lat, q_norm_g, kv_norm_g, cos_m, sin_m)
    wq = w_q_up.reshape(q_rank, heads, MLA_NOPE + MLA_ROPE)
    wq = jnp.concatenate([wq[:, :, :MLA_NOPE].reshape(q_rank, heads * MLA_NOPE),
                          wq[:, :, MLA_NOPE:].reshape(q_rank, heads * MLA_ROPE)], axis=1).astype(BF16)
    q = matmul(qn, wq, F32)
    kv = matmul(kvn, w_kv_up.astype(BF16), BF16)
    qf, kf, vf = mla_pack(q.reshape(b, s, -1), kv.reshape(b, s, -1), kr2.reshape(b, s, -1), cos_m, sin_m, heads)
    tq = 2 * _tile(s, FLASH_TKB)
    mla_out = flash_attention(qf.reshape(b, heads, s // tq, tq, 2 * LANES), kf, vf, None,
                              tq=tq, rep=1, out_dtype=BF16)
    mixed = jnp.concatenate([s5_out, mla_out.reshape(t, -1)], axis=-1)
    return matmul(mixed, w_out.astype(BF16), F32, kind="residual", res=x2)


def _odd_lat_kernel(z_ref, gq_ref, lng_ref, lnb_ref, cos_ref, sin_ref,
                    qn_ref, k_ref, v_ref, kidx_ref, w_ref, *, q_rank, w_scale):
    cos_t, sin_t = cos_ref[...], sin_ref[...]
    half = ROT_DIM // 2
    z = z_ref[0]
    q_lat = z[:, 0:q_rank]
    rq = lax.rsqrt(jnp.mean(q_lat * q_lat, axis=-1, keepdims=True) + EPS)
    qn_ref[0] = (q_lat * rq * gq_ref[...]).astype(qn_ref.dtype)
    kvh = DSA_KV_HEADS
    d = DSA_HEAD_DIM
    for g in range(kvh):
        kh = z[:, q_rank + g * d:q_rank + (g + 1) * d]
        k_ref[0, g] = _rope_tile(kh, cos_t, sin_t, half).astype(k_ref.dtype)
        v_ref[0, g, 0:d, :] = z[:, q_rank + (kvh + g) * d:q_rank + (kvh + g + 1) * d].T.astype(v_ref.dtype)
        v_ref[0, g, d:d + V_ONES_ROWS, :] = jnp.ones((V_ONES_ROWS, z.shape[0]), v_ref.dtype)
    off = q_rank + 2 * kvh * d
    ki = z[:, off:off + IDX_DIM]
    kc = ki - jnp.mean(ki, axis=-1, keepdims=True)
    var = jnp.mean(kc * kc, axis=-1, keepdims=True)
    ki = kc * lax.rsqrt(var + EPS) * lng_ref[...] + lnb_ref[...]
    kidx_ref[0] = _rope_tile(ki, cos_t, sin_t, half).astype(kidx_ref.dtype)
    w_ref[0] = (z[:, off + IDX_DIM:off + IDX_DIM + LANES] * w_scale).T


def odd_lat_prep(z, gq, ln_g, ln_b, cos_t, sin_t, idx_heads):
    b, s, zw = z.shape
    q_rank = gq.shape[0]
    tm = _tile(s, 512, SUBLANES)
    nt = s // tm
    w_scale = idx_heads ** -0.5 * IDX_DIM ** -0.5
    kvh, d = DSA_KV_HEADS, DSA_HEAD_DIM
    return pl.pallas_call(
        functools.partial(_odd_lat_kernel, q_rank=q_rank, w_scale=w_scale),
        grid=(b, nt),
        in_specs=[pl.BlockSpec((1, tm, zw), lambda bb, i: (bb, i, 0)),
                  pl.BlockSpec((1, q_rank), lambda bb, i: (0, 0)),
                  pl.BlockSpec((1, IDX_DIM), lambda bb, i: (0, 0)),
                  pl.BlockSpec((1, IDX_DIM), lambda bb, i: (0, 0)),
                  pl.BlockSpec((tm, LANES), lambda bb, i: (bb * nt + i, 0)),
                  pl.BlockSpec((tm, LANES), lambda bb, i: (bb * nt + i, 0))],
        out_specs=[pl.BlockSpec((1, tm, q_rank), lambda bb, i: (bb, i, 0)),
                   pl.BlockSpec((1, kvh, tm, d), lambda bb, i: (bb, 0, i, 0)),
                   pl.BlockSpec((1, kvh, d + V_ONES_ROWS, tm), lambda bb, i: (bb, 0, 0, i)),
                   pl.BlockSpec((1, tm, IDX_DIM), lambda bb, i: (bb, i, 0)),
                   pl.BlockSpec((1, LANES, tm), lambda bb, i: (bb, 0, i))],
        out_shape=[jax.ShapeDtypeStruct((b, s, q_rank), BF16),
                   jax.ShapeDtypeStruct((b, kvh, s, d), BF16),
                   jax.ShapeDtypeStruct((b, kvh, d + V_ONES_ROWS, s), BF16),
                   jax.ShapeDtypeStruct((b, s, IDX_DIM), BF16),
                   jax.ShapeDtypeStruct((b, LANES, s), F32)],
        compiler_params=_params(("parallel", "parallel")),
        name="odd_lat_prep",
    )(z, gq.reshape(1, -1).astype(F32), ln_g.reshape(1, -1).astype(F32),
      ln_b.reshape(1, -1).astype(F32), cos_t, sin_t)


def _dsa_q_kernel(qq_ref, cos_ref, sin_ref, q_ref, qi_ref, *, heads, idx_heads, rep, tq, scale):
    cos_t, sin_t = cos_ref[...], sin_ref[...]
    half = ROT_DIM // 2
    d = DSA_HEAD_DIM
    for h in range(heads):
        g, r = divmod(h, rep)
        xh = _rope_tile(qq_ref[0, :, h * d:(h + 1) * d], cos_t, sin_t, half)
        q_ref[0, g, 0, r * tq:(r + 1) * tq, :] = (xh * scale).astype(q_ref.dtype)
    off = heads * d
    for h in range(idx_heads):
        xh = _rope_tile(qq_ref[0, :, off + h * IDX_DIM:off + (h + 1) * IDX_DIM], cos_t, sin_t, half)
        qi_ref[0, :, h * IDX_DIM:(h + 1) * IDX_DIM] = xh.astype(qi_ref.dtype)


def dsa_q_prep(qq, cos_t, sin_t, heads, idx_heads, tq):
    b, s, wq = qq.shape
    nt = s // tq
    rep = heads // DSA_KV_HEADS
    d = DSA_HEAD_DIM
    scale = DSA_HEAD_DIM ** -0.5 * LOG2E
    return pl.pallas_call(
        functools.partial(_dsa_q_kernel, heads=heads, idx_heads=idx_heads, rep=rep, tq=tq, scale=scale),
        grid=(b, nt),
        in_specs=[pl.BlockSpec((1, tq, wq), lambda bb, i: (bb, i, 0)),
                  pl.BlockSpec((tq, LANES), lambda bb, i: (bb * nt + i, 0)),
                  pl.BlockSpec((tq, LANES), lambda bb, i: (bb * nt + i, 0))],
        out_specs=[pl.BlockSpec((1, DSA_KV_HEADS, 1, rep * tq, d), lambda bb, i: (bb, 0, i, 0, 0)),
                   pl.BlockSpec((1, tq, idx_heads * IDX_DIM), lambda bb, i: (bb, i, 0))],
        out_shape=[jax.ShapeDtypeStruct((b, DSA_KV_HEADS, nt, rep * tq, d), BF16),
                   jax.ShapeDtypeStruct((b, s, idx_heads * IDX_DIM), BF16)],
        compiler_params=_params(("parallel", "parallel")),
        name="dsa_q_prep",
    )(qq, cos_t, sin_t)


def _ukey_to_float(u):
    bits = jnp.where(u < 0, u & jnp.int32(0x7FFFFFFF), ~u)
    return lax.bitcast_convert_type(bits, F32)


def _float_to_ukey(x):
    bits = lax.bitcast_convert_type(x, jnp.int32)
    return jnp.where(bits < 0, ~bits, bits | jnp.int32(-2 ** 31))


IDX_GROUPS = 256


def _threshold_of_key(u):
    thr = _ukey_to_float(u)
    return jnp.where((thr != thr) & (u >= 0), jnp.float32(-jnp.inf), thr)


def _indexer_kernel(qi_ref, wt_ref, kidx_ref, bias_ref, score_ref, lim_ref, gmax_ref,
                    *, tq, tkb, idx_heads, top_k):
    i = pl.program_id(1)
    s_len = kidx_ref.shape[1]
    n_blocks = ((i + 1) * tq + tkb - 1) // tkb
    neg_inf = jnp.float32(-jnp.inf)
    qpos = i * tq + lax.broadcasted_iota(jnp.int32, (1, tq), 1)

    def causal(start):
        kpos = start + lax.broadcasted_iota(jnp.int32, (tkb, tq), 0)
        return kpos <= qpos

    def score_block(j, c):
        start = pl.multiple_of(j * tkb, tkb)
        kb = kidx_ref[0, pl.ds(start, tkb), :]
        acc = jnp.zeros((tkb, tq), F32)
        for h in range(idx_heads):
            logits = lax.dot_general(kb, qi_ref[0, :, h * IDX_DIM:(h + 1) * IDX_DIM],
                                     (((1,), (1,)), ((), ())), preferred_element_type=F32)
            acc = acc + jnp.maximum(logits, 0.0) * wt_ref[0, h:h + 1, :]
        sc = jnp.where(causal(start), acc, neg_inf)
        score_ref[pl.ds(start, tkb), :] = sc
        gmax_ref[...] = jnp.maximum(gmax_ref[...],
                                    jnp.max(sc.reshape(tkb // IDX_GROUPS, IDX_GROUPS, tq), axis=0))
        return c

    gmax_ref[...] = jnp.full((IDX_GROUPS, tq), neg_inf, F32)
    lax.fori_loop(0, n_blocks, score_block, 0)

    def count_where(pred):
        def body(j, cnt):
            start = pl.multiple_of(j * tkb, tkb)
            ind = jnp.where(pred(score_ref[pl.ds(start, tkb), :], start), 1.0, 0.0)
            part = jnp.sum(ind.reshape(tkb // (8 * SUBLANES), 8, SUBLANES, tq), axis=1)
            return cnt + jnp.sum(part, axis=0)
        cnt = lax.fori_loop(0, n_blocks, body, jnp.zeros((SUBLANES, tq), F32))
        return jnp.sum(cnt, axis=0, keepdims=True)

    def count_ge(thr):
        return count_where(lambda blk, start: blk >= thr)

    k_f = jnp.float32(top_k)
    few = qpos < top_k

    def search_cond(st):
        bi, _, cnt_u = st
        pending = jnp.where(few | (cnt_u == k_f), 0.0, 1.0)
        return (bi < 32) & (jnp.max(pending) > 0.0)

    def search_step(st):
        bi, u, cnt_u = st
        cand = u | (jnp.int32(1) << (31 - bi))
        cnt = count_ge(_threshold_of_key(cand))
        take = cnt >= k_f
        return bi + 1, jnp.where(take, cand, u), jnp.where(take, cnt, cnt_u)

    gmax = gmax_ref[...]
    key_lo = _float_to_ukey(jnp.min(gmax, axis=0, keepdims=True))
    key_hi = _float_to_ukey(jnp.max(gmax, axis=0, keepdims=True))
    shared = lax.clz(key_lo ^ key_hi)
    low_bits = jnp.where(shared >= 32, 0, lax.shift_right_logical(jnp.int32(-1), jnp.minimum(shared, 31)))
    u0 = key_lo & ~low_bits
    bi0 = jnp.min(shared.astype(F32)).astype(jnp.int32)
    _, u, cnt_u = lax.while_loop(search_cond, search_step,
                                 (bi0, u0, count_ge(_threshold_of_key(u0))))
    thr = jnp.where(few, neg_inf, _threshold_of_key(u))

    def key_pos(start):
        return start + lax.broadcasted_iota(jnp.int32, (tkb, tq), 0)

    excess = jnp.logical_not(few) & (cnt_u > k_f)
    lim_ref[...] = jnp.full((1, tq), s_len, jnp.int32)

    @pl.when(jnp.max(jnp.where(excess, 1.0, 0.0)) > 0.0)
    def _():
        need = k_f - count_where(lambda blk, start: blk > thr)
        nbits = s_len.bit_length()

        def tie_step(bi, p):
            cand = p | (jnp.int32(1) << (nbits - 1 - bi))
            cnt = count_where(lambda blk, start: (blk == thr) & (key_pos(start) < cand))
            return jnp.where(cnt <= need, cand, p)

        p = lax.fori_loop(0, nbits, tie_step, jnp.zeros((1, tq), jnp.int32))
        lim_ref[...] = jnp.where(excess, p, s_len)

    lim = lim_ref[...]

    def write_block(j, c):
        start = pl.multiple_of(j * tkb, tkb)
        blk = score_ref[pl.ds(start, tkb), :]
        keep = ((blk > thr) | ((blk == thr) & (key_pos(start) < lim))) & causal(start)
        bias_ref[0, pl.ds(start, tkb), :] = jnp.where(keep, 0.0, MASK_VALUE).astype(bias_ref.dtype)
        return c

    lax.fori_loop(0, n_blocks, write_block, 0)

    def fill_block(j, c):
        start = pl.multiple_of(j * tkb, tkb)
        bias_ref[0, pl.ds(start, tkb), :] = jnp.full((tkb, tq), MASK_VALUE, bias_ref.dtype)
        return c

    lax.fori_loop(n_blocks, s_len // tkb, fill_block, 0)


def dsa_indexer(qi, wt, kidx, idx_heads, top_k):
    b, s, _ = qi.shape
    tq = _tile(s, 256)
    tkb = _tile(s, 512)
    assert (tkb >= top_k or tkb == s) and top_k <= IDX_GROUPS and tkb % IDX_GROUPS == 0
    return pl.pallas_call(
        functools.partial(_indexer_kernel, tq=tq, tkb=tkb, idx_heads=idx_heads, top_k=top_k),
        grid=(b, s // tq),
        in_specs=[pl.BlockSpec((1, tq, idx_heads * IDX_DIM), lambda bb, i: (bb, i, 0)),
                  pl.BlockSpec((1, LANES, tq), lambda bb, i: (bb, 0, i)),
                  pl.BlockSpec((1, s, IDX_DIM), lambda bb, i: (bb, 0, 0))],
        out_specs=pl.BlockSpec((1, s, tq), lambda bb, i: (bb, 0, i)),
        out_shape=jax.ShapeDtypeStruct((b, s, s), BF16),
        scratch_shapes=[pltpu.VMEM((s, tq), F32), pltpu.VMEM((1, tq), jnp.int32),
                        pltpu.VMEM((IDX_GROUPS, tq), F32)],
        compiler_params=_params(("parallel", "parallel")),
        name="dsa_indexer",
    )(qi, wt, kidx)


def odd_mixer(x2, h, b, s, cos_p, sin_p, w_in, q_norm_g, w_q_up, w_idx_q, k_ln_g, k_ln_b, w_out):
    t, d = h.shape
    q_rank = q_norm_g.shape[0]
    heads = w_q_up.shape[1] // DSA_HEAD_DIM
    idx_heads = w_idx_q.shape[1] // IDX_DIM
    in_w = w_in.shape[1]
    z_w = -(-(in_w - idx_heads + LANES) // (2 * LANES)) * (2 * LANES)
    w_z = jnp.pad(w_in, ((0, 0), (0, z_w - in_w))).astype(BF16)
    z = matmul(h, w_z, F32)
    qn, k, vt, kidx, wt = odd_lat_prep(z.reshape(b, s, z_w), q_norm_g, k_ln_g, k_ln_b, cos_p, sin_p, idx_heads)
    w_qq = jnp.concatenate([w_q_up, w_idx_q], axis=1).astype(BF16)
    qq = matmul(qn.reshape(t, q_rank), w_qq, F32)
    tq = _tile(s, 256)
    q, qi = dsa_q_prep(qq.reshape(b, s, -1), cos_p, sin_p, heads, idx_heads, tq)
    top_k = min(IDX_TOPK_MAX, s // 4)
    bias = dsa_indexer(qi, wt, kidx, idx_heads, top_k)
    o = flash_attention(q, k, vt, bias, tq=tq, rep=heads // DSA_KV_HEADS, out_dtype=BF16)
    return matmul(o.reshape(t, -1), w_out.astype(BF16), F32, kind="residual", res=x2)


MLP_TILE_TRIALS = {
    1: ((1024, 1024, 2048), (1024, 512, 4096)),
    2: ((2048, 256, 4096), (2048, 512, 2048)),
    3: ((1024, 1024, 4096), (1024, 1024, 2048)),
}


def sq_relu_mlp(x2, h, w_up, w_down, layer):
    up_tiles, down_tiles = MLP_TILE_TRIALS.get(layer, (None, None))
    if layer == 3:
        a = matmul(h, w_up[layer].astype(BF16), BF16, kind="relu2", tiles=up_tiles)
        return matmul(a, w_down[layer].astype(BF16), F32, kind="residual", res=x2, tiles=down_tiles)
    a = matmul(h, w_up, BF16, kind="relu2", layer=layer, tiles=up_tiles)
    return matmul(a, w_down, F32, kind="residual", res=x2, layer=layer, tiles=down_tiles)


def kernel(x, positions, norm_mix_g, norm_mlp_g, final_norm_g, even_w_in, s5_lam_re, s5_lam_im, s5_log_step, s5_b_re, s5_b_im, s5_c_re, s5_c_im, s5_d, s5_glu_a, s5_glu_b, mla_q_norm_g, mla_w_q_up, mla_kv_norm_g, mla_w_kv_up, even_w_out, odd_w_in, dsa_q_norm_g, dsa_w_q_up, idx_w_q, idx_k_ln_g, idx_k_ln_b, odd_w_out, mlp_w_up, mlp_w_down):
    b, s, d = x.shape
    depth = norm_mix_g.shape[0]
    cos_m, sin_m = _rope_tables(positions, MLA_ROPE, LANES)
    cos_p, sin_p = _rope_tables(positions, ROT_DIM, ROT_DIM)
    x2 = x.reshape(b * s, d)
    for layer in range(depth):
        i = layer // 2
        h = rmsnorm(x2, norm_mix_g[layer], BF16)
        if layer % 2 == 0:
            x2 = even_mixer(x2, h, b, s, cos_m, sin_m, even_w_in[i], s5_lam_re[i], s5_lam_im[i],
                            s5_log_step[i], s5_b_re[i], s5_b_im[i], s5_c_re[i], s5_c_im[i], s5_d[i],
                            s5_glu_a[i], s5_glu_b[i], mla_q_norm_g[i], mla_w_q_up[i],
                            mla_kv_norm_g[i], mla_w_kv_up[i], even_w_out[i])
        else:
            x2 = odd_mixer(x2, h, b, s, cos_p, sin_p, odd_w_in[i], dsa_q_norm_g[i], dsa_w_q_up[i],
                           idx_w_q[i], idx_k_ln_g[i], idx_k_ln_b[i], odd_w_out[i])
        h = rmsnorm(x2, norm_mlp_g[layer], BF16)
        x2 = sq_relu_mlp(x2, h, mlp_w_up, mlp_w_down, layer)
    return rmsnorm(x2, final_norm_g, x.dtype).reshape(b, s, d)
```

```python
import functools
import math

import jax
import jax.numpy as jnp
from jax import lax
from jax.experimental import pallas as pl
from jax.experimental.pallas import tpu as pltpu

F32 = jnp.float32
BF16 = jnp.bfloat16

EPS = 1e-6
ROPE_THETA = 500000.0
S5_GROUP = 16
S5_LAMBDA_RE_MAX = -1e-4
MLA_NOPE = 128
MLA_ROPE = 64
MLA_V = 128
DSA_HEAD_DIM = 128
DSA_KV_HEADS = 4
IDX_DIM = 128
IDX_TOPK_MAX = 256
ROT_DIM = DSA_HEAD_DIM // 4

LANES = 128
SUBLANES = 8
VMEM_LIMIT_BYTES = 56 * 1024 * 1024
MASK_VALUE = -1e30


def _params(semantics):
    return pltpu.CompilerParams(dimension_semantics=semantics, vmem_limit_bytes=VMEM_LIMIT_BYTES)


def _tile(dim, pref, align=LANES):
    if dim <= pref:
        return dim
    t = (pref // align) * align
    while t >= align:
        if dim % t == 0:
            return t
        t -= align
    return dim


def _rmsnorm_kernel(x_ref, g_ref, o_ref):
    x = x_ref[...].astype(F32)
    r = lax.rsqrt(jnp.mean(x * x, axis=-1, keepdims=True) + EPS)
    o_ref[...] = (x * r * g_ref[...]).astype(o_ref.dtype)


def rmsnorm(x, g, out_dtype):
    m, d = x.shape
    tm = _tile(m, 256, SUBLANES)
    return pl.pallas_call(
        _rmsnorm_kernel,
        grid=(m // tm,),
        in_specs=[pl.BlockSpec((tm, d), lambda i: (i, 0)),
                  pl.BlockSpec((1, d), lambda i: (0, 0))],
        out_specs=pl.BlockSpec((tm, d), lambda i: (i, 0)),
        out_shape=jax.ShapeDtypeStruct((m, d), out_dtype),
        compiler_params=_params(("parallel",)),
        name="rmsnorm",
    )(x, g.reshape(1, d).astype(F32))


def _mm_epilogue(acc, kind, res_ref):
    if kind == "relu2":
        a = jnp.maximum(acc, 0.0)
        return a * a
    if kind == "residual":
        return res_ref[...].astype(F32) + acc
    return acc


def _mm_kernel_single(*refs, kind):
    if kind == "residual":
        a_ref, w_ref, res_ref, o_ref = refs
    else:
        a_ref, w_ref, o_ref = refs
        res_ref = None
    acc = jnp.dot(a_ref[...].astype(BF16), w_ref[...].astype(BF16), preferred_element_type=F32)
    o_ref[...] = _mm_epilogue(acc, kind, res_ref).astype(o_ref.dtype)


def _mm_kernel_ksplit(*refs, kind, nk):
    if kind == "residual":
        a_ref, w_ref, res_ref, o_ref, acc_ref = refs
    else:
        a_ref, w_ref, o_ref, acc_ref = refs
        res_ref = None
    k = pl.program_id(2)

    @pl.when(k == 0)
    def _():
        acc_ref[...] = jnp.zeros_like(acc_ref)

    acc_ref[...] += jnp.dot(a_ref[...].astype(BF16), w_ref[...].astype(BF16), preferred_element_type=F32)

    @pl.when(k == nk - 1)
    def _():
        o_ref[...] = _mm_epilogue(acc_ref[...], kind, res_ref).astype(o_ref.dtype)


def matmul(a, w, out_dtype, kind="none", res=None, layer=None, tiles=None):
    m, k = a.shape
    n = w.shape[-1]
    if tiles is None:
        tiles = (1024, 512, k) if k <= 4096 else (1024, 1024, 2048)
    tm = _tile(m, tiles[0], SUBLANES)
    tn = _tile(n, tiles[1])
    tk = _tile(k, tiles[2])

    def w_spec(tk_, tn_, index):
        if layer is None:
            return pl.BlockSpec((tk_, tn_), index)
        return pl.BlockSpec((None, tk_, tn_), lambda *g: (layer,) + index(*g))

    if tk == k:
        in_specs = [pl.BlockSpec((tm, k), lambda i, j: (i, 0)),
                    w_spec(k, tn, lambda i, j: (0, j))]
        args = [a, w]
        if kind == "residual":
            in_specs.append(pl.BlockSpec((tm, tn), lambda i, j: (i, j)))
            args.append(res)
        return pl.pallas_call(
            functools.partial(_mm_kernel_single, kind=kind),
            grid=(m // tm, n // tn),
            in_specs=in_specs,
            out_specs=pl.BlockSpec((tm, tn), lambda i, j: (i, j)),
            out_shape=jax.ShapeDtypeStruct((m, n), out_dtype),
            compiler_params=_params(("parallel", "parallel")),
            name="matmul",
        )(*args)
    nk = k // tk
    in_specs = [pl.BlockSpec((tm, tk), lambda i, j, kk: (i, kk)),
                w_spec(tk, tn, lambda i, j, kk: (kk, j))]
    args = [a, w]
    if kind == "residual":
        in_specs.append(pl.BlockSpec((tm, tn), lambda i, j, kk: (i, j)))
        args.append(res)
    return pl.pallas_call(
        functools.partial(_mm_kernel_ksplit, kind=kind, nk=nk),
        grid=(m // tm, n // tn, nk),
        in_specs=in_specs,
        out_specs=pl.BlockSpec((tm, tn), lambda i, j, kk: (i, j)),
        out_shape=jax.ShapeDtypeStruct((m, n), out_dtype),
        scratch_shapes=[pltpu.VMEM((tm, tn), F32)],
        compiler_params=_params(("parallel", "parallel", "arbitrary")),
        name="matmul_ksplit",
    )(*args)


def _glu_kernel(a_ref, wa_ref, wb_ref, o_ref):
    a = a_ref[...].astype(BF16)
    ya = jnp.dot(a, wa_ref[...], preferred_element_type=F32)
    yb = jnp.dot(a, wb_ref[...], preferred_element_type=F32)
    o_ref[...] = (ya * jax.nn.sigmoid(yb)).astype(o_ref.dtype)


def glu_matmul(a, wa, wb, out_dtype):
    m, k = a.shape
    _, n = wa.shape
    tm = _tile(m, 1024, SUBLANES)
    tn = _tile(n, 512)
    return pl.pallas_call(
        _glu_kernel,
        grid=(m // tm, n // tn),
        in_specs=[pl.BlockSpec((tm, k), lambda i, j: (i, 0)),
                  pl.BlockSpec((k, tn), lambda i, j: (0, j)),
                  pl.BlockSpec((k, tn), lambda i, j: (0, j))],
        out_specs=pl.BlockSpec((tm, tn), lambda i, j: (i, j)),
        out_shape=jax.ShapeDtypeStruct((m, n), out_dtype),
        compiler_params=_params(("parallel", "parallel")),
        name="glu_matmul",
    )(a, wa, wb)


def _rope_tile(x, cos_t, sin_t, half):
    lane = lax.broadcasted_iota(jnp.int32, x.shape, 1)
    first = (lane % (2 * half)) < half
    partner = jnp.where(first, pltpu.roll(x, LANES - half, 1), pltpu.roll(x, half, 1))
    return x * cos_t + partner * sin_t


def _rope_tables(positions, dim, pad_to):
    inv_freq = ROPE_THETA ** (-jnp.arange(0, dim, 2, dtype=F32) / dim)
    ang = positions.astype(F32).reshape(-1)[:, None] * inv_freq
    c, s = jnp.cos(ang), jnp.sin(ang)
    reps = pad_to // dim
    cos_t = jnp.tile(jnp.concatenate([c, c], axis=-1), (1, reps))
    sin_t = jnp.tile(jnp.concatenate([-s, s], axis=-1), (1, reps))
    t = c.shape[0]
    cos_t = jnp.concatenate([cos_t, jnp.ones((t, LANES - pad_to), F32)], axis=-1)
    sin_t = jnp.concatenate([sin_t, jnp.zeros((t, LANES - pad_to), F32)], axis=-1)
    return cos_t, sin_t


V_ONES_ROWS = 16
FLASH_TKB = 512
LOG2E = math.log2(math.e)


def _flash_kernel(*refs, tq, tkb, rep, dv, has_bias, diag_first):
    if has_bias:
        q_ref, k_ref, vt_ref, b_ref, o_ref, m_ref, acc_ref, sa_ref, sb_ref = refs
    else:
        q_ref, k_ref, vt_ref, o_ref, m_ref, acc_ref, sa_ref, sb_ref = refs
        b_ref = None
    i = pl.program_id(2)
    rows = rep * tq

    m_ref[...] = jnp.full(m_ref.shape, MASK_VALUE, F32)
    acc_ref[...] = jnp.zeros(acc_ref.shape, F32)
    q = q_ref[0, 0, 0]

    def qk(j, dst_ref):
        start = pl.multiple_of(j * tkb, tkb)
        kb = k_ref[0, 0, pl.ds(start, tkb), :]
        dst_ref[...] = lax.dot_general(kb, q, (((1,), (1,)), ((), ())),
                                       preferred_element_type=F32)

    def softmax_pv(j, src_ref, masked):
        start = pl.multiple_of(j * tkb, tkb)
        vtb = vt_ref[0, 0, :, pl.ds(start, tkb)]
        s = src_ref[...]
        if has_bias:
            bias = b_ref[0, pl.ds(start, tkb), :].astype(F32)
            s = s + jnp.tile(bias, (1, rep))
        if masked:
            kpos = start + lax.broadcasted_iota(jnp.int32, (tkb, rows), 0)
            qlane = lax.broadcasted_iota(jnp.int32, (tkb, rows), 1)
            qpos = i * tq + (qlane % tq if rep > 1 else qlane)
            s = jnp.where(kpos <= qpos, s, MASK_VALUE)
        m_prev = m_ref[...]
        m_new = jnp.maximum(m_prev, jnp.max(s, axis=0, keepdims=True))
        p = jnp.exp2(s - m_new)
        alpha = jnp.exp2(m_prev - m_new)
        acc_ref[...] = alpha * acc_ref[...] + jnp.dot(vtb, p.astype(BF16), preferred_element_type=F32)
        m_ref[...] = m_new

    if diag_first:
        qk(2 * i, sa_ref)
        qk(2 * i + 1, sb_ref)
        softmax_pv(2 * i, sa_ref, True)
        qk(0, sa_ref)
        softmax_pv(2 * i + 1, sb_ref, True)

        def pair(jj, c):
            j0 = 2 * jj
            qk(j0 + 1, sb_ref)
            softmax_pv(j0, sa_ref, False)
            qk(jnp.minimum(j0 + 2, 2 * i - 1), sa_ref)
            softmax_pv(j0 + 1, sb_ref, False)
            return c

        lax.fori_loop(0, i, pair, 0)
    elif has_bias:
        qk(0, sa_ref)
        n_blocks = ((i + 1) * tq + tkb - 1) // tkb

        def pair(jj, c):
            j0 = 2 * jj
            qk(j0 + 1, sb_ref)
            softmax_pv(j0, sa_ref, False)
            qk(jnp.minimum(j0 + 2, n_blocks - 1), sa_ref)
            softmax_pv(j0 + 1, sb_ref, False)
            return c

        lax.fori_loop(0, n_blocks // 2, pair, 0)

        @pl.when(n_blocks % 2 == 1)
        def _():
            softmax_pv(n_blocks - 1, sa_ref, False)
    else:
        qk(0, sa_ref)

        def pair(jj, c):
            j0 = 2 * jj
            qk(j0 + 1, sb_ref)
            softmax_pv(j0, sa_ref, False)
            qk(j0 + 2, sa_ref)
            softmax_pv(j0 + 1, sb_ref, False)
            return c

        lax.fori_loop(0, i, pair, 0)
        qk(2 * i + 1, sb_ref)
        softmax_pv(2 * i, sa_ref, True)
        softmax_pv(2 * i + 1, sb_ref, True)

    acc = acc_ref[...]
    out = (acc[0:dv] * (1.0 / acc[dv:dv + 1])).T
    for r in range(rep):
        o_ref[0, :, r * dv:(r + 1) * dv] = out[r * tq:(r + 1) * tq].astype(o_ref.dtype)


def flash_attention(q, k, vt, bias, *, tq, rep, out_dtype, tkb=FLASH_TKB, diag_first=False):
    b, hk, nq, rows, dq = q.shape
    s = k.shape[2]
    dve = vt.shape[2]
    dv = dve - V_ONES_ROWS
    tkb = _tile(s, tkb)
    has_bias = bias is not None
    assert has_bias or tq == 2 * tkb
    in_specs = [pl.BlockSpec((1, 1, 1, rows, dq), lambda bb, h, i: (bb, h, i, 0, 0)),
                pl.BlockSpec((1, 1, s, dq), lambda bb, h, i: (bb, h, 0, 0)),
                pl.BlockSpec((1, 1, dve, s), lambda bb, h, i: (bb, h, 0, 0))]
    args = [q, k, vt]
    if has_bias:
        in_specs.append(pl.BlockSpec((1, s, tq), lambda bb, h, i: (bb, 0, i)))
        args.append(bias)
    return pl.pallas_call(
        functools.partial(_flash_kernel, tq=tq, tkb=tkb, rep=rep, dv=dv, has_bias=has_bias,
                          diag_first=diag_first),
        grid=(b, hk, nq),
        in_specs=in_specs,
        out_specs=pl.BlockSpec((1, tq, rep * dv), lambda bb, h, i: (bb, i, h)),
        out_shape=jax.ShapeDtypeStruct((b, s, hk * rep * dv), out_dtype),
        scratch_shapes=[pltpu.VMEM((1, rows), F32), pltpu.VMEM((dve, rows), F32),
                        pltpu.VMEM((tkb, rows), F32), pltpu.VMEM((tkb, rows), F32)],
        compiler_params=_params(("parallel", "parallel", "parallel")),
        name="flash_attention",
    )(*args)


S5_NSEG = 2 * SUBLANES
S5_CBLK = LANES
S5_LT = 16


def _s5_kernel(u_ref, bbd_ref, cbd_ref, a_ref, ap_ref, d_ref, y_ref,
               ug_ref, bu_ref, xs_ref, init_ref, *, seg_len, lt):
    nseg = S5_NSEG
    ns = a_ref.shape[-1]
    a_re = jnp.broadcast_to(a_ref[0, 0:1, :], (nseg, ns))
    a_im = jnp.broadcast_to(a_ref[0, 1:2, :], (nseg, ns))
    ntiles = seg_len // lt

    def load_inputs(t, slot):
        for i in range(lt):
            ug_ref[slot, i * nseg:(i + 1) * nseg, :] = u_ref[0, pl.ds(t * lt + i, nseg, stride=seg_len), :]
        bu_ref[slot] = jnp.dot(ug_ref[slot].astype(BF16), bbd_ref[0], preferred_element_type=F32)

    def scan_tile(carry, slot, store):
        x_re, x_im = carry
        for i in range(lt):
            r0 = i * nseg
            b_re = bu_ref[slot, r0:r0 + nseg, 0:ns]
            b_im = bu_ref[slot, r0:r0 + nseg, ns:2 * ns]
            n_re = a_re * x_re - a_im * x_im + b_re
            n_im = a_re * x_im + a_im * x_re + b_im
            x_re, x_im = n_re, n_im
            if store:
                xs_ref[slot, r0:r0 + nseg, 0:ns] = x_re
                xs_ref[slot, r0:r0 + nseg, ns:2 * ns] = x_im
        return x_re, x_im

    def emit(t, slot):
        y = jnp.dot(xs_ref[slot].astype(BF16), cbd_ref[0], preferred_element_type=F32)
        y = jax.nn.gelu(y + d_ref[0] * ug_ref[slot])
        for i in range(lt):
            y_ref[0, pl.ds(t * lt + i, nseg, stride=seg_len), :] = y[i * nseg:(i + 1) * nseg]

    def sweep(carry, store):
        load_inputs(0, 0)

        def pair(tt, c):
            t0 = 2 * tt
            load_inputs(t0 + 1, 1)
            c = scan_tile(c, 0, store)
            if store:
                emit(t0, 0)
            load_inputs(jnp.minimum(t0 + 2, ntiles - 1), 0)
            c = scan_tile(c, 1, store)
            if store:
                emit(t0 + 1, 1)
            return c

        return lax.fori_loop(0, ntiles // 2, pair, carry)

    zeros = jnp.zeros((nseg, ns), F32)
    e_re, e_im = sweep((zeros, zeros), False)

    ap_re = ap_ref[0, 0:1, :]
    ap_im = ap_ref[0, 1:2, :]
    t_re = jnp.zeros((1, ns), F32)
    t_im = jnp.zeros((1, ns), F32)
    init_ref[0:1, :] = jnp.zeros((1, 2 * ns), F32)
    for s in range(1, nseg):
        p_re, p_im = e_re[s - 1:s], e_im[s - 1:s]
        t_re, t_im = (ap_re * t_re - ap_im * t_im + p_re,
                      ap_re * t_im + ap_im * t_re + p_im)
        init_ref[s:s + 1, 0:ns] = t_re
        init_ref[s:s + 1, ns:2 * ns] = t_im

    sweep((init_ref[:, 0:ns], init_ref[:, ns:2 * ns]), True)


def _s5_discretise(lam_re, lam_im, log_step, b_re, b_im, c_re, c_im, seg_len):
    g, n = lam_re.shape
    p = S5_GROUP
    lr = jnp.minimum(lam_re.astype(F32), S5_LAMBDA_RE_MAX)
    li = lam_im.astype(F32)
    step = jnp.exp(log_step.astype(F32))[:, None]
    mag = jnp.exp(lr * step)
    ab_re = mag * jnp.cos(li * step)
    ab_im = mag * jnp.sin(li * step)
    den = lr * lr + li * li
    f_re = ((ab_re - 1.0) * lr + ab_im * li) / den
    f_im = (ab_im * lr - (ab_re - 1.0) * li) / den
    br = b_re.astype(F32)
    bi = b_im.astype(F32)
    bb_re = f_re[..., None] * br - f_im[..., None] * bi
    bb_im = f_re[..., None] * bi + f_im[..., None] * br
    pw_re, pw_im = jnp.ones_like(ab_re), jnp.zeros_like(ab_im)
    sq_re, sq_im = ab_re, ab_im
    e = seg_len
    while e:
        if e & 1:
            pw_re, pw_im = pw_re * sq_re - pw_im * sq_im, pw_re * sq_im + pw_im * sq_re
        sq_re, sq_im = sq_re * sq_re - sq_im * sq_im, 2.0 * sq_re * sq_im
        e >>= 1
    gb = S5_CBLK // p
    nb = g // gb
    eye = jnp.eye(gb, dtype=F32)

    def blockdiag_in(bb):
        x = bb.reshape(nb, gb, n, p)
        return jnp.einsum("bgnp,gh->bgphn", x, eye).reshape(nb, gb * p, gb * n)

    def blockdiag_out(c):
        x = c.astype(F32).reshape(nb, gb, p, n)
        return jnp.einsum("bgpn,gh->bgnhp", x, eye).reshape(nb, gb * n, gb * p)

    bbd = jnp.concatenate([blockdiag_in(bb_re), blockdiag_in(bb_im)], axis=2).astype(BF16)
    cbd = jnp.concatenate([blockdiag_out(c_re), -blockdiag_out(c_im)], axis=1).astype(BF16)
    a = jnp.stack([ab_re.reshape(nb, gb * n), ab_im.reshape(nb, gb * n)], axis=1)
    ap = jnp.stack([pw_re.reshape(nb, gb * n), pw_im.reshape(nb, gb * n)], axis=1)
    return bbd, cbd, a, ap


def s5_scan(u, lam_re, lam_im, log_step, b_re, b_im, c_re, c_im, d_skip):
    b, s, w = u.shape
    seg_len = s // S5_NSEG
    lt = min(S5_LT, seg_len)
    assert (seg_len // lt) % 2 == 0
    bbd, cbd, a, ap = _s5_discretise(lam_re, lam_im, log_step, b_re, b_im, c_re, c_im, seg_len)
    nb, cw, ns2 = bbd.shape
    ns = ns2 // 2
    rows = lt * S5_NSEG
    return pl.pallas_call(
        functools.partial(_s5_kernel, seg_len=seg_len, lt=lt),
        grid=(b, nb),
        in_specs=[pl.BlockSpec((1, s, cw), lambda bb, j: (bb, 0, j)),
                  pl.BlockSpec((1, cw, ns2), lambda bb, j: (j, 0, 0)),
                  pl.BlockSpec((1, ns2, cw), lambda bb, j: (j, 0, 0)),
                  pl.BlockSpec((1, 2, ns), lambda bb, j: (j, 0, 0)),
                  pl.BlockSpec((1, 2, ns), lambda bb, j: (j, 0, 0)),
                  pl.BlockSpec((1, 1, cw), lambda bb, j: (j, 0, 0))],
        out_specs=pl.BlockSpec((1, s, cw), lambda bb, j: (bb, 0, j)),
        out_shape=jax.ShapeDtypeStruct((b, s, w), F32),
        scratch_shapes=[pltpu.VMEM((2, rows, cw), F32), pltpu.VMEM((2, rows, ns2), F32),
                        pltpu.VMEM((2, rows, ns2), F32), pltpu.VMEM((S5_NSEG, ns2), F32)],
        compiler_params=_params(("parallel", "parallel")),
        name="s5_scan",
    )(u, bbd, cbd, a, ap, d_skip.astype(F32).reshape(nb, 1, cw))


def _even_lat_kernel(lat_ref, gq_ref, gkv_ref, cos_ref, sin_ref, qn_ref, kvn_ref, kr_ref, *, q_rank, kv_rank):
    lat = lat_ref[...]
    q_lat = lat[:, 0:q_rank]
    kv_lat = lat[:, q_rank:q_rank + kv_rank]
    kr = lat[:, q_rank + kv_rank:q_rank + kv_rank + LANES]
    rq = lax.rsqrt(jnp.mean(q_lat * q_lat, axis=-1, keepdims=True) + EPS)
    qn_ref[...] = (q_lat * rq * gq_ref[...]).astype(qn_ref.dtype)
    rkv = lax.rsqrt(jnp.mean(kv_lat * kv_lat, axis=-1, keepdims=True) + EPS)
    kvn_ref[...] = (kv_lat * rkv * gkv_ref[...]).astype(kvn_ref.dtype)
    kr = _rope_tile(kr, cos_ref[...], sin_ref[...], MLA_ROPE // 2)
    lane = lax.broadcasted_iota(jnp.int32, kr.shape, 1)
    lo = jnp.where(lane < MLA_ROPE, kr, 0.0)
    hi = pltpu.roll(lo, MLA_ROPE, 1)
    kr_ref[:, 0:LANES] = lo.astype(kr_ref.dtype)
    kr_ref[:, LANES:2 * LANES] = hi.astype(kr_ref.dtype)


def even_lat_prep(lat, gq, gkv, cos_t, sin_t):
    m, wl = lat.shape
    q_rank, kv_rank = gq.shape[0], gkv.shape[0]
    tm = _tile(m, 512, SUBLANES)
    return pl.pallas_call(
        functools.partial(_even_lat_kernel, q_rank=q_rank, kv_rank=kv_rank),
        grid=(m // tm,),
        in_specs=[pl.BlockSpec((tm, wl), lambda i: (i, 0)),
                  pl.BlockSpec((1, q_rank), lambda i: (0, 0)),
                  pl.BlockSpec((1, kv_rank), lambda i: (0, 0)),
                  pl.BlockSpec((tm, LANES), lambda i: (i, 0)),
                  pl.BlockSpec((tm, LANES), lambda i: (i, 0))],
        out_specs=[pl.BlockSpec((tm, q_rank), lambda i: (i, 0)),
                   pl.BlockSpec((tm, kv_rank), lambda i: (i, 0)),
                   pl.BlockSpec((tm, 2 * LANES), lambda i: (i, 0))],
        out_shape=[jax.ShapeDtypeStruct((m, q_rank), BF16),
                   jax.ShapeDtypeStruct((m, kv_rank), BF16),
                   jax.ShapeDtypeStruct((m, 2 * LANES), BF16)],
        compiler_params=_params(("parallel",)),
        name="even_lat_prep",
    )(lat, gq.reshape(1, -1).astype(F32), gkv.reshape(1, -1).astype(F32), cos_t, sin_t)


def _mla_pack_kernel(q_ref, kv_ref, kr_ref, cos_ref, sin_ref, qf_ref, kf_ref, vf_ref, *, heads, scale):
    cos_t, sin_t = cos_ref[...], sin_ref[...]
    nope_w = heads * MLA_NOPE
    lane = lax.broadcasted_iota(jnp.int32, (q_ref.shape[1], LANES), 1)
    for pair in range(heads // 2):
        rope = _rope_tile(q_ref[0, :, nope_w + pair * LANES:nope_w + (pair + 1) * LANES],
                          cos_t, sin_t, MLA_ROPE // 2)
        for sub in range(2):
            h = 2 * pair + sub
            keep = (lane < MLA_ROPE) if sub == 0 else (lane >= MLA_ROPE)
            qf_ref[0, h, :, 0:LANES] = (q_ref[0, :, h * MLA_NOPE:(h + 1) * MLA_NOPE] * scale).astype(qf_ref.dtype)
            qf_ref[0, h, :, LANES:2 * LANES] = (jnp.where(keep, rope, 0.0) * scale).astype(qf_ref.dtype)
            kf_ref[0, h, :, 0:LANES] = kv_ref[0, :, h * 2 * LANES:h * 2 * LANES + LANES]
            kf_ref[0, h, :, LANES:2 * LANES] = kr_ref[0, :, sub * LANES:(sub + 1) * LANES]
            v_h = kv_ref[0, :, h * 2 * LANES + LANES:(h + 1) * 2 * LANES].astype(F32)
            vf_ref[0, h, 0:MLA_V, :] = v_h.T.astype(vf_ref.dtype)
            vf_ref[0, h, MLA_V:MLA_V + V_ONES_ROWS, :] = jnp.ones((V_ONES_ROWS, v_h.shape[0]), vf_ref.dtype)


def mla_pack(q, kv, kr2, cos_t, sin_t, heads):
    b, s, _ = q.shape
    tm = _tile(s, 256, SUBLANES)
    scale = (MLA_NOPE + MLA_ROPE) ** -0.5 * LOG2E
    return pl.pallas_call(
        functools.partial(_mla_pack_kernel, heads=heads, scale=scale),
        grid=(b, s // tm),
        in_specs=[pl.BlockSpec((1, tm, q.shape[2]), lambda bb, i: (bb, i, 0)),
                  pl.BlockSpec((1, tm, kv.shape[2]), lambda bb, i: (bb, i, 0)),
                  pl.BlockSpec((1, tm, 2 * LANES), lambda bb, i: (bb, i, 0)),
                  pl.BlockSpec((tm, LANES), lambda bb, i, nt=s // tm: (bb * nt + i, 0)),
                  pl.BlockSpec((tm, LANES), lambda bb, i, nt=s // tm: (bb * nt + i, 0))],
        out_specs=[pl.BlockSpec((1, heads, tm, 2 * LANES), lambda bb, i: (bb, 0, i, 0)),
                   pl.BlockSpec((1, heads, tm, 2 * LANES), lambda bb, i: (bb, 0, i, 0)),
                   pl.BlockSpec((1, heads, MLA_V + V_ONES_ROWS, tm), lambda bb, i: (bb, 0, 0, i))],
        out_shape=[jax.ShapeDtypeStruct((b, heads, s, 2 * LANES), BF16),
                   jax.ShapeDtypeStruct((b, heads, s, 2 * LANES), BF16),
                   jax.ShapeDtypeStruct((b, heads, MLA_V + V_ONES_ROWS, s), BF16)],
        compiler_params=_params(("parallel", "parallel")),
        name="mla_pack",
    )(q, kv, kr2, cos_t, sin_t)


def even_mixer(x2, h, b, s, cos_m, sin_m, w_in, lam_re, lam_im, log_step, b_re, b_im, c_re, c_im,
               d_skip, glu_a, glu_b, q_norm_g, w_q_up, kv_norm_g, w_kv_up, w_out, diag_first=False):
    t, d = h.shape
    s5_w = d_skip.shape[0]
    q_rank, kv_rank = q_norm_g.shape[0], kv_norm_g.shape[0]
    heads = w_kv_up.shape[1] // (MLA_NOPE + MLA_V)
    lat_w = q_rank + kv_rank + MLA_ROPE
    lat_pad = -(-(q_rank + kv_rank + LANES) // (2 * LANES)) * (2 * LANES)

    w_u = w_in[:, :s5_w].astype(BF16)
    w_lat = jnp.pad(w_in[:, s5_w:], ((0, 0), (0, lat_pad - lat_w))).astype(BF16)
    u = matmul(h, w_u, F32)
    lat = matmul(h, w_lat, F32)

    y = s5_scan(u.reshape(b, s, s5_w), lam_re, lam_im, log_step, b_re, b_im, c_re, c_im, d_skip)
    s5_out = glu_matmul(y.reshape(t, s5_w), glu_a.astype(BF16), glu_b.astype(BF16), BF16)

    qn, kvn, kr2 = even_lat_prep(lat, q_norm_g, kv_norm_g, cos_m, sin_m)
    wq = w_q_up.reshape(q_rank, heads, MLA_NOPE + MLA_ROPE)
    wq = jnp.concatenate([wq[:, :, :MLA_NOPE].reshape(q_rank, heads * MLA_NOPE),
                          wq[:, :, MLA_NOPE:].reshape(q_rank, heads * MLA_ROPE)], axis=1).astype(BF16)
    q = matmul(qn, wq, F32)
    kv = matmul(kvn, w_kv_up.astype(BF16), BF16)
    qf, kf, vf = mla_pack(q.reshape(b, s, -1), kv.reshape(b, s, -1), kr2.reshape(b, s, -1), cos_m, sin_m, heads)
    tq = 2 * _tile(s, FLASH_TKB)
    mla_out = flash_attention(qf.reshape(b, heads, s // tq, tq, 2 * LANES), kf, vf, None,
                              tq=tq, rep=1, out_dtype=BF16, diag_first=diag_first)
    mixed = jnp.concatenate([s5_out, mla_out.reshape(t, -1)], axis=-1)
    return matmul(mixed, w_out.astype(BF16), F32, kind="residual", res=x2)


def _odd_lat_kernel(z_ref, gq_ref, lng_ref, lnb_ref, cos_ref, sin_ref,
                    qn_ref, k_ref, v_ref, kidx_ref, w_ref, *, q_rank, w_scale):
    cos_t, sin_t = cos_ref[...], sin_ref[...]
    half = ROT_DIM // 2
    z = z_ref[0]
    q_lat = z[:, 0:q_rank]
    rq = lax.rsqrt(jnp.mean(q_lat * q_lat, axis=-1, keepdims=True) + EPS)
    qn_ref[0] = (q_lat * rq * gq_ref[...]).astype(qn_ref.dtype)
    kvh = DSA_KV_HEADS
    d = DSA_HEAD_DIM
    for g in range(kvh):
        kh = z[:, q_rank + g * d:q_rank + (g + 1) * d]
        k_ref[0, g] = _rope_tile(kh, cos_t, sin_t, half).astype(k_ref.dtype)
        v_ref[0, g, 0:d, :] = z[:, q_rank + (kvh + g) * d:q_rank + (kvh + g + 1) * d].T.astype(v_ref.dtype)
        v_ref[0, g, d:d + V_ONES_ROWS, :] = jnp.ones((V_ONES_ROWS, z.shape[0]), v_ref.dtype)
    off = q_rank + 2 * kvh * d
    ki = z[:, off:off + IDX_DIM]
    kc = ki - jnp.mean(ki, axis=-1, keepdims=True)
    var = jnp.mean(kc * kc, axis=-1, keepdims=True)
    ki = kc * lax.rsqrt(var + EPS) * lng_ref[...] + lnb_ref[...]
    kidx_ref[0] = _rope_tile(ki, cos_t, sin_t, half).astype(kidx_ref.dtype)
    w_ref[0] = (z[:, off + IDX_DIM:off + IDX_DIM + LANES] * w_scale).T


def odd_lat_prep(z, gq, ln_g, ln_b, cos_t, sin_t, idx_heads):
    b, s, zw = z.shape
    q_rank = gq.shape[0]
    tm = _tile(s, 512, SUBLANES)
    nt = s // tm
    w_scale = idx_heads ** -0.5 * IDX_DIM ** -0.5
    kvh, d = DSA_KV_HEADS, DSA_HEAD_DIM
    return pl.pallas_call(
        functools.partial(_odd_lat_kernel, q_rank=q_rank, w_scale=w_scale),
        grid=(b, nt),
        in_specs=[pl.BlockSpec((1, tm, zw), lambda bb, i: (bb, i, 0)),
                  pl.BlockSpec((1, q_rank), lambda bb, i: (0, 0)),
                  pl.BlockSpec((1, IDX_DIM), lambda bb, i: (0, 0)),
                  pl.BlockSpec((1, IDX_DIM), lambda bb, i: (0, 0)),
                  pl.BlockSpec((tm, LANES), lambda bb, i: (bb * nt + i, 0)),
                  pl.BlockSpec((tm, LANES), lambda bb, i: (bb * nt + i, 0))],
        out_specs=[pl.BlockSpec((1, tm, q_rank), lambda bb, i: (bb, i, 0)),
                   pl.BlockSpec((1, kvh, tm, d), lambda bb, i: (bb, 0, i, 0)),
                   pl.BlockSpec((1, kvh, d + V_ONES_ROWS, tm), lambda bb, i: (bb, 0, 0, i)),
                   pl.BlockSpec((1, tm, IDX_DIM), lambda bb, i: (bb, i, 0)),
                   pl.BlockSpec((1, LANES, tm), lambda bb, i: (bb, 0, i))],
        out_shape=[jax.ShapeDtypeStruct((b, s, q_rank), BF16),
                   jax.ShapeDtypeStruct((b, kvh, s, d), BF16),
                   jax.ShapeDtypeStruct((b, kvh, d + V_ONES_ROWS, s), BF16),
                   jax.ShapeDtypeStruct((b, s, IDX_DIM), BF16),
                   jax.ShapeDtypeStruct((b, LANES, s), F32)],
        compiler_params=_params(("parallel", "parallel")),
        name="odd_lat_prep",
    )(z, gq.reshape(1, -1).astype(F32), ln_g.reshape(1, -1).astype(F32),
      ln_b.reshape(1, -1).astype(F32), cos_t, sin_t)


def _mm_rope_kernel(a_ref, w_ref, cos_ref, sin_ref, o_ref, *, n_heads, scale, stack_rows):
    acc = jnp.dot(a_ref[...], w_ref[...], preferred_element_type=F32)
    cos_t, sin_t = cos_ref[...], sin_ref[...]
    d = DSA_HEAD_DIM
    for r in range(n_heads):
        xh = _rope_tile(acc[:, r * d:(r + 1) * d], cos_t, sin_t, ROT_DIM // 2) * scale
        if stack_rows:
            o_ref[0, 0, 0, r * stack_rows:(r + 1) * stack_rows, :] = xh.astype(o_ref.dtype)
        else:
            o_ref[:, r * d:(r + 1) * d] = xh.astype(o_ref.dtype)


def dsa_q_proj(qn, w_q, cos_t, sin_t, b, s, tq):
    t, kq = qn.shape
    d = DSA_HEAD_DIM
    heads = w_q.shape[1] // d
    rep = heads // DSA_KV_HEADS
    nt = s // tq
    return pl.pallas_call(
        functools.partial(_mm_rope_kernel, n_heads=rep, scale=d ** -0.5 * LOG2E, stack_rows=tq),
        grid=(t // tq, DSA_KV_HEADS),
        in_specs=[pl.BlockSpec((tq, kq), lambda i, g: (i, 0)),
                  pl.BlockSpec((kq, rep * d), lambda i, g: (0, g)),
                  pl.BlockSpec((tq, LANES), lambda i, g: (i, 0)),
                  pl.BlockSpec((tq, LANES), lambda i, g: (i, 0))],
        out_specs=pl.BlockSpec((1, 1, 1, rep * tq, d), lambda i, g: (i // nt, g, i % nt, 0, 0)),
        out_shape=jax.ShapeDtypeStruct((b, DSA_KV_HEADS, nt, rep * tq, d), BF16),
        compiler_params=_params(("parallel", "parallel")),
        name="dsa_q_proj",
    )(qn, w_q, cos_t, sin_t)


def idx_q_proj(qn, w_qi, cos_t, sin_t):
    t, kq = qn.shape
    n = w_qi.shape[1]
    tm = _tile(t, 1024, SUBLANES)
    tn = _tile(n, 512)
    return pl.pallas_call(
        functools.partial(_mm_rope_kernel, n_heads=tn // IDX_DIM, scale=1.0, stack_rows=0),
        grid=(t // tm, n // tn),
        in_specs=[pl.BlockSpec((tm, kq), lambda i, j: (i, 0)),
                  pl.BlockSpec((kq, tn), lambda i, j: (0, j)),
                  pl.BlockSpec((tm, LANES), lambda i, j: (i, 0)),
                  pl.BlockSpec((tm, LANES), lambda i, j: (i, 0))],
        out_specs=pl.BlockSpec((tm, tn), lambda i, j: (i, j)),
        out_shape=jax.ShapeDtypeStruct((t, n), BF16),
        compiler_params=_params(("parallel", "parallel")),
        name="idx_q_proj",
    )(qn, w_qi, cos_t, sin_t)


def _ukey_to_float(u):
    bits = jnp.where(u < 0, u & jnp.int32(0x7FFFFFFF), ~u)
    return lax.bitcast_convert_type(bits, F32)


def _threshold_of_key(u):
    thr = _ukey_to_float(u)
    return jnp.where((thr != thr) & (u >= 0), jnp.float32(-jnp.inf), thr)


def _indexer_kernel(qi_ref, wt_ref, kidx_ref, bias_ref, score_ref, lim_ref, *, tq, tkb, idx_heads, top_k):
    i = pl.program_id(1)
    s_len = kidx_ref.shape[1]
    n_blocks = ((i + 1) * tq + tkb - 1) // tkb
    neg_inf = jnp.float32(-jnp.inf)
    qpos = i * tq + lax.broadcasted_iota(jnp.int32, (1, tq), 1)

    def causal(start):
        kpos = start + lax.broadcasted_iota(jnp.int32, (tkb, tq), 0)
        return kpos <= qpos

    def score_block(j, c):
        start = pl.multiple_of(j * tkb, tkb)
        kb = kidx_ref[0, pl.ds(start, tkb), :]
        acc = jnp.zeros((tkb, tq), F32)
        for h in range(idx_heads):
            logits = lax.dot_general(kb, qi_ref[0, :, h * IDX_DIM:(h + 1) * IDX_DIM],
                                     (((1,), (1,)), ((), ())), preferred_element_type=F32)
            acc = acc + jnp.maximum(logits, 0.0) * wt_ref[0, h:h + 1, :]
        score_ref[pl.ds(start, tkb), :] = jnp.where(causal(start), acc, neg_inf)
        return c

    lax.fori_loop(0, n_blocks, score_block, 0)

    def count_where(pred):
        def body(j, cnt):
            start = pl.multiple_of(j * tkb, tkb)
            ind = jnp.where(pred(score_ref[pl.ds(start, tkb), :], start), 1.0, 0.0)
            part = jnp.sum(ind.reshape(tkb // (8 * SUBLANES), 8, SUBLANES, tq), axis=1)
            return cnt + jnp.sum(part, axis=0)
        cnt = lax.fori_loop(0, n_blocks, body, jnp.zeros((SUBLANES, tq), F32))
        return jnp.sum(cnt, axis=0, keepdims=True)

    def count_ge(thr):
        return count_where(lambda blk, start: blk >= thr)

    k_f = jnp.float32(top_k)
    few = qpos < top_k

    def search_cond(st):
        bi, _, cnt_u = st
        pending = jnp.where(few | (cnt_u == k_f), 0.0, 1.0)
        return (bi < 32) & (jnp.max(pending) > 0.0)

    def search_step(st):
        bi, u, cnt_u = st
        cand = u | (jnp.int32(1) << (31 - bi))
        cnt = count_ge(_threshold_of_key(cand))
        take = cnt >= k_f
        return bi + 1, jnp.where(take, cand, u), jnp.where(take, cnt, cnt_u)

    total = (n_blocks * tkb).astype(F32)
    _, u, cnt_u = lax.while_loop(search_cond, search_step,
                                 (jnp.int32(0), jnp.zeros((1, tq), jnp.int32), jnp.full((1, tq), total, F32)))
    thr = jnp.where(few, neg_inf, _threshold_of_key(u))

    def key_pos(start):
        return start + lax.broadcasted_iota(jnp.int32, (tkb, tq), 0)

    excess = jnp.logical_not(few) & (cnt_u > k_f)
    lim_ref[...] = jnp.full((1, tq), s_len, jnp.int32)

    @pl.when(jnp.max(jnp.where(excess, 1.0, 0.0)) > 0.0)
    def _():
        need = k_f - count_where(lambda blk, start: blk > thr)
        nbits = s_len.bit_length()

        def tie_step(bi, p):
            cand = p | (jnp.int32(1) << (nbits - 1 - bi))
            cnt = count_where(lambda blk, start: (blk == thr) & (key_pos(start) < cand))
            return jnp.where(cnt <= need, cand, p)

        p = lax.fori_loop(0, nbits, tie_step, jnp.zeros((1, tq), jnp.int32))
        lim_ref[...] = jnp.where(excess, p, s_len)

    lim = lim_ref[...]

    def write_block(j, c):
        start = pl.multiple_of(j * tkb, tkb)
        blk = score_ref[pl.ds(start, tkb), :]
        keep = ((blk > thr) | ((blk == thr) & (key_pos(start) < lim))) & causal(start)
        bias_ref[0, pl.ds(start, tkb), :] = jnp.where(keep, 0.0, MASK_VALUE).astype(bias_ref.dtype)
        return c

    lax.fori_loop(0, n_blocks, write_block, 0)

    def fill_block(j, c):
        start = pl.multiple_of(j * tkb, tkb)
        bias_ref[0, pl.ds(start, tkb), :] = jnp.full((tkb, tq), MASK_VALUE, bias_ref.dtype)
        return c

    lax.fori_loop(n_blocks, s_len // tkb, fill_block, 0)


def dsa_indexer(qi, wt, kidx, idx_heads, top_k, tq_pref):
    b, s, _ = qi.shape
    tq = _tile(s, tq_pref)
    tkb = _tile(s, 512)
    assert tkb >= top_k or tkb == s
    return pl.pallas_call(
        functools.partial(_indexer_kernel, tq=tq, tkb=tkb, idx_heads=idx_heads, top_k=top_k),
        grid=(b, s // tq),
        in_specs=[pl.BlockSpec((1, tq, idx_heads * IDX_DIM), lambda bb, i: (bb, i, 0)),
                  pl.BlockSpec((1, LANES, tq), lambda bb, i: (bb, 0, i)),
                  pl.BlockSpec((1, s, IDX_DIM), lambda bb, i: (bb, 0, 0))],
        out_specs=pl.BlockSpec((1, s, tq), lambda bb, i: (bb, 0, i)),
        out_shape=jax.ShapeDtypeStruct((b, s, s), BF16),
        scratch_shapes=[pltpu.VMEM((s, tq), F32), pltpu.VMEM((1, tq), jnp.int32)],
        compiler_params=_params(("parallel", "parallel")),
        name="dsa_indexer",
    )(qi, wt, kidx)


def odd_mixer(x2, h, b, s, cos_p, sin_p, w_in, q_norm_g, w_q_up, w_idx_q, k_ln_g, k_ln_b, w_out,
              idx_tq=256, flash_tkb=FLASH_TKB):
    t, d = h.shape
    q_rank = q_norm_g.shape[0]
    heads = w_q_up.shape[1] // DSA_HEAD_DIM
    idx_heads = w_idx_q.shape[1] // IDX_DIM
    in_w = w_in.shape[1]
    z_w = -(-(in_w - idx_heads + LANES) // (2 * LANES)) * (2 * LANES)
    w_z = jnp.pad(w_in, ((0, 0), (0, z_w - in_w))).astype(BF16)
    z = matmul(h, w_z, F32)
    qn, k, vt, kidx, wt = odd_lat_prep(z.reshape(b, s, z_w), q_norm_g, k_ln_g, k_ln_b, cos_p, sin_p, idx_heads)
    tq = _tile(s, 256)
    qn2 = qn.reshape(t, q_rank)
    q = dsa_q_proj(qn2, w_q_up.astype(BF16), cos_p, sin_p, b, s, tq)
    qi = idx_q_proj(qn2, w_idx_q.astype(BF16), cos_p, sin_p).reshape(b, s, -1)
    top_k = min(IDX_TOPK_MAX, s // 4)
    bias = dsa_indexer(qi, wt, kidx, idx_heads, top_k, idx_tq)
    o = flash_attention(q, k, vt, bias, tq=tq, rep=heads // DSA_KV_HEADS, out_dtype=BF16, tkb=flash_tkb)
    return matmul(o.reshape(t, -1), w_out.astype(BF16), F32, kind="residual", res=x2)


def sq_relu_mlp(x2, h, w_up, w_down, layer):
    a = matmul(h, w_up, BF16, kind="relu2", layer=layer)
    return matmul(a, w_down, F32, kind="residual", res=x2, layer=layer)


def kernel(x, positions, norm_mix_g, norm_mlp_g, final_norm_g, even_w_in, s5_lam_re, s5_lam_im, s5_log_step, s5_b_re, s5_b_im, s5_c_re, s5_c_im, s5_d, s5_glu_a, s5_glu_b, mla_q_norm_g, mla_w_q_up, mla_kv_norm_g, mla_w_kv_up, even_w_out, odd_w_in, dsa_q_norm_g, dsa_w_q_up, idx_w_q, idx_k_ln_g, idx_k_ln_b, odd_w_out, mlp_w_up, mlp_w_down):
    b, s, d = x.shape
    depth = norm_mix_g.shape[0]
    cos_m, sin_m = _rope_tables(positions, MLA_ROPE, LANES)
    cos_p, sin_p = _rope_tables(positions, ROT_DIM, ROT_DIM)
    x2 = x.reshape(b * s, d)
    for layer in range(depth):
        i = layer // 2
        h = rmsnorm(x2, norm_mix_g[layer], BF16)
        if layer % 2 == 0:
            x2 = even_mixer(x2, h, b, s, cos_m, sin_m, even_w_in[i], s5_lam_re[i], s5_lam_im[i],
                            s5_log_step[i], s5_b_re[i], s5_b_im[i], s5_c_re[i], s5_c_im[i], s5_d[i],
                            s5_glu_a[i], s5_glu_b[i], mla_q_norm_g[i], mla_w_q_up[i],
                            mla_kv_norm_g[i], mla_w_kv_up[i], even_w_out[i], diag_first=(i == 1))
        else:
            x2 = odd_mixer(x2, h, b, s, cos_p, sin_p, odd_w_in[i], dsa_q_norm_g[i], dsa_w_q_up[i],
                           idx_w_q[i], idx_k_ln_g[i], idx_k_ln_b[i], odd_w_out[i],
                           idx_tq=(512 if i == 1 else 256), flash_tkb=(1024 if i == 1 else FLASH_TKB))
        h = rmsnorm(x2, norm_mlp_g[layer], BF16)
        x2 = sq_relu_mlp(x2, h, mlp_w_up, mlp_w_down, layer)
    return rmsnorm(x2, final_norm_g, x.dtype).reshape(b, s, d)
```

```python
import functools
import math

import jax
import jax.numpy as jnp
from jax import lax
from jax.experimental import pallas as pl
from jax.experimental.pallas import tpu as pltpu

F32 = jnp.float32
BF16 = jnp.bfloat16

EPS = 1e-6
ROPE_THETA = 500000.0
S5_GROUP = 16
S5_LAMBDA_RE_MAX = -1e-4
MLA_NOPE = 128
MLA_ROPE = 64
MLA_V = 128
DSA_HEAD_DIM = 128
DSA_KV_HEADS = 4
IDX_DIM = 128
IDX_TOPK_MAX = 256
ROT_DIM = DSA_HEAD_DIM // 4

LANES = 128
SUBLANES = 8
VMEM_LIMIT_BYTES = 56 * 1024 * 1024
MASK_VALUE = -1e30


def _params(semantics):
    return pltpu.CompilerParams(dimension_semantics=semantics, vmem_limit_bytes=VMEM_LIMIT_BYTES)


def _tile(dim, pref, align=LANES):
    if dim <= pref:
        return dim
    t = (pref // align) * align
    while t >= align:
        if dim % t == 0:
            return t
        t -= align
    return dim


def _rmsnorm_kernel(x_ref, g_ref, o_ref):
    x = x_ref[...].astype(F32)
    r = lax.rsqrt(jnp.mean(x * x, axis=-1, keepdims=True) + EPS)
    o_ref[...] = (x * r * g_ref[...]).astype(o_ref.dtype)


def rmsnorm(x, g, out_dtype):
    m, d = x.shape
    tm = _tile(m, 256, SUBLANES)
    return pl.pallas_call(
        _rmsnorm_kernel,
        grid=(m // tm,),
        in_specs=[pl.BlockSpec((tm, d), lambda i: (i, 0)),
                  pl.BlockSpec((1, d), lambda i: (0, 0))],
        out_specs=pl.BlockSpec((tm, d), lambda i: (i, 0)),
        out_shape=jax.ShapeDtypeStruct((m, d), out_dtype),
        compiler_params=_params(("parallel",)),
        name="rmsnorm",
    )(x, g.reshape(1, d).astype(F32))


def _mm_epilogue(acc, kind, res_ref):
    if kind == "relu2":
        a = jnp.maximum(acc, 0.0)
        return a * a
    if kind == "residual":
        return res_ref[...].astype(F32) + acc
    return acc


def _mm_kernel_single(*refs, kind):
    if kind == "residual":
        a_ref, w_ref, res_ref, o_ref = refs
    else:
        a_ref, w_ref, o_ref = refs
        res_ref = None
    acc = jnp.dot(a_ref[...].astype(BF16), w_ref[...].astype(BF16), preferred_element_type=F32)
    o_ref[...] = _mm_epilogue(acc, kind, res_ref).astype(o_ref.dtype)


def _mm_kernel_ksplit(*refs, kind, nk):
    if kind == "residual":
        a_ref, w_ref, res_ref, o_ref, acc_ref = refs
    else:
        a_ref, w_ref, o_ref, acc_ref = refs
        res_ref = None
    k = pl.program_id(2)

    @pl.when(k == 0)
    def _():
        acc_ref[...] = jnp.zeros_like(acc_ref)

    acc_ref[...] += jnp.dot(a_ref[...].astype(BF16), w_ref[...].astype(BF16), preferred_element_type=F32)

    @pl.when(k == nk - 1)
    def _():
        o_ref[...] = _mm_epilogue(acc_ref[...], kind, res_ref).astype(o_ref.dtype)


def matmul(a, w, out_dtype, kind="none", res=None, layer=None):
    m, k = a.shape
    n = w.shape[-1]
    tm_pref, tn_pref, tk_pref = (1024, 512, k) if k <= 4096 else (1024, 1024, 2048)
    tm = _tile(m, tm_pref, SUBLANES)
    tn = _tile(n, tn_pref)
    tk = _tile(k, tk_pref)

    def w_spec(tk_, tn_, index):
        if layer is None:
            return pl.BlockSpec((tk_, tn_), index)
        return pl.BlockSpec((None, tk_, tn_), lambda *g: (layer,) + index(*g))

    if tk == k:
        in_specs = [pl.BlockSpec((tm, k), lambda i, j: (i, 0)),
                    w_spec(k, tn, lambda i, j: (0, j))]
        args = [a, w]
        if kind == "residual":
            in_specs.append(pl.BlockSpec((tm, tn), lambda i, j: (i, j)))
            args.append(res)
        return pl.pallas_call(
            functools.partial(_mm_kernel_single, kind=kind),
            grid=(m // tm, n // tn),
            in_specs=in_specs,
            out_specs=pl.BlockSpec((tm, tn), lambda i, j: (i, j)),
            out_shape=jax.ShapeDtypeStruct((m, n), out_dtype),
            compiler_params=_params(("parallel", "parallel")),
            name="matmul",
        )(*args)
    nk = k // tk
    in_specs = [pl.BlockSpec((tm, tk), lambda i, j, kk: (i, kk)),
                w_spec(tk, tn, lambda i, j, kk: (kk, j))]
    args = [a, w]
    if kind == "residual":
        in_specs.append(pl.BlockSpec((tm, tn), lambda i, j, kk: (i, j)))
        args.append(res)
    return pl.pallas_call(
        functools.partial(_mm_kernel_ksplit, kind=kind, nk=nk),
        grid=(m // tm, n // tn, nk),
        in_specs=in_specs,
        out_specs=pl.BlockSpec((tm, tn), lambda i, j, kk: (i, j)),
        out_shape=jax.ShapeDtypeStruct((m, n), out_dtype),
        scratch_shapes=[pltpu.VMEM((tm, tn), F32)],
        compiler_params=_params(("parallel", "parallel", "arbitrary")),
        name="matmul_ksplit",
    )(*args)


def _glu_kernel(a_ref, wa_ref, wb_ref, o_ref):
    a = a_ref[...].astype(BF16)
    ya = jnp.dot(a, wa_ref[...], preferred_element_type=F32)
    yb = jnp.dot(a, wb_ref[...], preferred_element_type=F32)
    o_ref[...] = (ya * jax.nn.sigmoid(yb)).astype(o_ref.dtype)


def glu_matmul(a, wa, wb, out_dtype):
    m, k = a.shape
    _, n = wa.shape
    tm = _tile(m, 1024, SUBLANES)
    tn = _tile(n, 512)
    return pl.pallas_call(
        _glu_kernel,
        grid=(m // tm, n // tn),
        in_specs=[pl.BlockSpec((tm, k), lambda i, j: (i, 0)),
                  pl.BlockSpec((k, tn), lambda i, j: (0, j)),
                  pl.BlockSpec((k, tn), lambda i, j: (0, j))],
        out_specs=pl.BlockSpec((tm, tn), lambda i, j: (i, j)),
        out_shape=jax.ShapeDtypeStruct((m, n), out_dtype),
        compiler_params=_params(("parallel", "parallel")),
        name="glu_matmul",
    )(a, wa, wb)


def _rope_tile(x, cos_t, sin_t, half):
    lane = lax.broadcasted_iota(jnp.int32, x.shape, 1)
    first = (lane % (2 * half)) < half
    partner = jnp.where(first, pltpu.roll(x, LANES - half, 1), pltpu.roll(x, half, 1))
    return x * cos_t + partner * sin_t


def _rope_tables(positions, dim, pad_to):
    inv_freq = ROPE_THETA ** (-jnp.arange(0, dim, 2, dtype=F32) / dim)
    ang = positions.astype(F32).reshape(-1)[:, None] * inv_freq
    c, s = jnp.cos(ang), jnp.sin(ang)
    reps = pad_to // dim
    cos_t = jnp.tile(jnp.concatenate([c, c], axis=-1), (1, reps))
    sin_t = jnp.tile(jnp.concatenate([-s, s], axis=-1), (1, reps))
    t = c.shape[0]
    cos_t = jnp.concatenate([cos_t, jnp.ones((t, LANES - pad_to), F32)], axis=-1)
    sin_t = jnp.concatenate([sin_t, jnp.zeros((t, LANES - pad_to), F32)], axis=-1)
    return cos_t, sin_t


V_ONES_ROWS = 16
FLASH_TKB = 512
LOG2E = math.log2(math.e)


def _flash_kernel(*refs, tq, tkb, rep, dv, has_bias):
    if has_bias:
        q_ref, k_ref, vt_ref, b_ref, o_ref, m_ref, acc_ref, sa_ref, sb_ref = refs
    else:
        q_ref, k_ref, vt_ref, o_ref, m_ref, acc_ref, sa_ref, sb_ref = refs
        b_ref = None
    i = pl.program_id(2)
    rows = rep * tq

    m_ref[...] = jnp.full(m_ref.shape, MASK_VALUE, F32)
    acc_ref[...] = jnp.zeros(acc_ref.shape, F32)
    q = q_ref[0, 0, 0]

    def qk(j, dst_ref):
        start = pl.multiple_of(j * tkb, tkb)
        kb = k_ref[0, 0, pl.ds(start, tkb), :]
        dst_ref[...] = lax.dot_general(kb, q, (((1,), (1,)), ((), ())),
                                       preferred_element_type=F32)

    def softmax_pv(j, src_ref, masked):
        start = pl.multiple_of(j * tkb, tkb)
        vtb = vt_ref[0, 0, :, pl.ds(start, tkb)]
        s = src_ref[...]
        if has_bias:
            bias = b_ref[0, pl.ds(start, tkb), :].astype(F32)
            s = s + jnp.tile(bias, (1, rep))
        if masked:
            kpos = start + lax.broadcasted_iota(jnp.int32, (tkb, rows), 0)
            qlane = lax.broadcasted_iota(jnp.int32, (tkb, rows), 1)
            qpos = i * tq + (qlane % tq if rep > 1 else qlane)
            s = jnp.where(kpos <= qpos, s, MASK_VALUE)
        m_prev = m_ref[...]
        m_new = jnp.maximum(m_prev, jnp.max(s, axis=0, keepdims=True))
        p = jnp.exp2(s - m_new)
        alpha = jnp.exp2(m_prev - m_new)
        acc_ref[...] = alpha * acc_ref[...] + jnp.dot(vtb, p.astype(BF16), preferred_element_type=F32)
        m_ref[...] = m_new

    if has_bias:
        qk(0, sa_ref)
        n_blocks = ((i + 1) * tq + tkb - 1) // tkb

        def pair(jj, c):
            j0 = 2 * jj
            qk(j0 + 1, sb_ref)
            softmax_pv(j0, sa_ref, False)
            qk(jnp.minimum(j0 + 2, n_blocks - 1), sa_ref)
            softmax_pv(j0 + 1, sb_ref, False)
            return c

        lax.fori_loop(0, n_blocks // 2, pair, 0)

        @pl.when(n_blocks % 2 == 1)
        def _():
            softmax_pv(n_blocks - 1, sa_ref, False)
    else:
        qk(2 * i, sa_ref)
        qk(2 * i + 1, sb_ref)
        softmax_pv(2 * i, sa_ref, True)
        qk(0, sa_ref)
        softmax_pv(2 * i + 1, sb_ref, True)

        def pair(jj, c):
            j0 = 2 * jj
            qk(j0 + 1, sb_ref)
            softmax_pv(j0, sa_ref, False)
            qk(jnp.minimum(j0 + 2, 2 * i - 1), sa_ref)
            softmax_pv(j0 + 1, sb_ref, False)
            return c

        lax.fori_loop(0, i, pair, 0)

    acc = acc_ref[...]
    out = (acc[0:dv] * (1.0 / acc[dv:dv + 1])).T
    for r in range(rep):
        o_ref[0, :, r * dv:(r + 1) * dv] = out[r * tq:(r + 1) * tq].astype(o_ref.dtype)


def flash_attention(q, k, vt, bias, *, tq, rep, out_dtype):
    b, hk, nq, rows, dq = q.shape
    s = k.shape[2]
    dve = vt.shape[2]
    dv = dve - V_ONES_ROWS
    tkb = _tile(s, FLASH_TKB)
    has_bias = bias is not None
    assert has_bias or tq == 2 * tkb
    in_specs = [pl.BlockSpec((1, 1, 1, rows, dq), lambda bb, h, i: (bb, h, i, 0, 0)),
                pl.BlockSpec((1, 1, s, dq), lambda bb, h, i: (bb, h, 0, 0)),
                pl.BlockSpec((1, 1, dve, s), lambda bb, h, i: (bb, h, 0, 0))]
    args = [q, k, vt]
    if has_bias:
        in_specs.append(pl.BlockSpec((1, s, tq), lambda bb, h, i: (bb, 0, i)))
        args.append(bias)
    return pl.pallas_call(
        functools.partial(_flash_kernel, tq=tq, tkb=tkb, rep=rep, dv=dv, has_bias=has_bias),
        grid=(b, hk, nq),
        in_specs=in_specs,
        out_specs=pl.BlockSpec((1, tq, rep * dv), lambda bb, h, i: (bb, i, h)),
        out_shape=jax.ShapeDtypeStruct((b, s, hk * rep * dv), out_dtype),
        scratch_shapes=[pltpu.VMEM((1, rows), F32), pltpu.VMEM((dve, rows), F32),
                        pltpu.VMEM((tkb, rows), F32), pltpu.VMEM((tkb, rows), F32)],
        compiler_params=_params(("parallel", "parallel", "parallel")),
        name="flash_attention",
    )(*args)


S5_NSEG = 2 * SUBLANES
S5_CBLK = LANES
S5_LT = 16


def _s5_kernel(u_ref, bbd_ref, cbd_ref, a_ref, ap_ref, d_ref, y_ref,
               ug_ref, bu_ref, xs_ref, init_ref, *, seg_len, lt):
    nseg = S5_NSEG
    ns = a_ref.shape[-1]
    a_re = jnp.broadcast_to(a_ref[0, 0:1, :], (nseg, ns))
    a_im = jnp.broadcast_to(a_ref[0, 1:2, :], (nseg, ns))
    ntiles = seg_len // lt

    def load_inputs(t, slot):
        for i in range(lt):
            ug_ref[slot, i * nseg:(i + 1) * nseg, :] = u_ref[0, pl.ds(t * lt + i, nseg, stride=seg_len), :]
        bu_ref[slot] = jnp.dot(ug_ref[slot].astype(BF16), bbd_ref[0], preferred_element_type=F32)

    def scan_tile(carry, slot, store):
        x_re, x_im = carry
        for i in range(lt):
            r0 = i * nseg
            b_re = bu_ref[slot, r0:r0 + nseg, 0:ns]
            b_im = bu_ref[slot, r0:r0 + nseg, ns:2 * ns]
            n_re = a_re * x_re - a_im * x_im + b_re
            n_im = a_re * x_im + a_im * x_re + b_im
            x_re, x_im = n_re, n_im
            if store:
                xs_ref[slot, r0:r0 + nseg, 0:ns] = x_re
                xs_ref[slot, r0:r0 + nseg, ns:2 * ns] = x_im
        return x_re, x_im

    def emit(t, slot):
        y = jnp.dot(xs_ref[slot].astype(BF16), cbd_ref[0], preferred_element_type=F32)
        y = jax.nn.gelu(y + d_ref[0] * ug_ref[slot])
        for i in range(lt):
            y_ref[0, pl.ds(t * lt + i, nseg, stride=seg_len), :] = y[i * nseg:(i + 1) * nseg]

    def sweep(carry, store):
        load_inputs(0, 0)

        def pair(tt, c):
            t0 = 2 * tt
            load_inputs(t0 + 1, 1)
            c = scan_tile(c, 0, store)
            if store:
                emit(t0, 0)
            load_inputs(jnp.minimum(t0 + 2, ntiles - 1), 0)
            c = scan_tile(c, 1, store)
            if store:
                emit(t0 + 1, 1)
            return c

        return lax.fori_loop(0, ntiles // 2, pair, carry)

    zeros = jnp.zeros((nseg, ns), F32)
    e_re, e_im = sweep((zeros, zeros), False)

    ap_re = ap_ref[0, 0:1, :]
    ap_im = ap_ref[0, 1:2, :]
    t_re = jnp.zeros((1, ns), F32)
    t_im = jnp.zeros((1, ns), F32)
    init_ref[0:1, :] = jnp.zeros((1, 2 * ns), F32)
    for s in range(1, nseg):
        p_re, p_im = e_re[s - 1:s], e_im[s - 1:s]
        t_re, t_im = (ap_re * t_re - ap_im * t_im + p_re,
                      ap_re * t_im + ap_im * t_re + p_im)
        init_ref[s:s + 1, 0:ns] = t_re
        init_ref[s:s + 1, ns:2 * ns] = t_im

    sweep((init_ref[:, 0:ns], init_ref[:, ns:2 * ns]), True)


def _s5_discretise(lam_re, lam_im, log_step, b_re, b_im, c_re, c_im, seg_len):
    g, n = lam_re.shape
    p = S5_GROUP
    lr = jnp.minimum(lam_re.astype(F32), S5_LAMBDA_RE_MAX)
    li = lam_im.astype(F32)
    step = jnp.exp(log_step.astype(F32))[:, None]
    mag = jnp.exp(lr * step)
    ab_re = mag * jnp.cos(li * step)
    ab_im = mag * jnp.sin(li * step)
    den = lr * lr + li * li
    f_re = ((ab_re - 1.0) * lr + ab_im * li) / den
    f_im = (ab_im * lr - (ab_re - 1.0) * li) / den
    br = b_re.astype(F32)
    bi = b_im.astype(F32)
    bb_re = f_re[..., None] * br - f_im[..., None] * bi
    bb_im = f_re[..., None] * bi + f_im[..., None] * br
    pw_re, pw_im = jnp.ones_like(ab_re), jnp.zeros_like(ab_im)
    sq_re, sq_im = ab_re, ab_im
    e = seg_len
    while e:
        if e & 1:
            pw_re, pw_im = pw_re * sq_re - pw_im * sq_im, pw_re * sq_im + pw_im * sq_re
        sq_re, sq_im = sq_re * sq_re - sq_im * sq_im, 2.0 * sq_re * sq_im
        e >>= 1
    gb = S5_CBLK // p
    nb = g // gb
    eye = jnp.eye(gb, dtype=F32)

    def blockdiag_in(bb):
        x = bb.reshape(nb, gb, n, p)
        return jnp.einsum("bgnp,gh->bgphn", x, eye).reshape(nb, gb * p, gb * n)

    def blockdiag_out(c):
        x = c.astype(F32).reshape(nb, gb, p, n)
        return jnp.einsum("bgpn,gh->bgnhp", x, eye).reshape(nb, gb * n, gb * p)

    bbd = jnp.concatenate([blockdiag_in(bb_re), blockdiag_in(bb_im)], axis=2).astype(BF16)
    cbd = jnp.concatenate([blockdiag_out(c_re), -blockdiag_out(c_im)], axis=1).astype(BF16)
    a = jnp.stack([ab_re.reshape(nb, gb * n), ab_im.reshape(nb, gb * n)], axis=1)
    ap = jnp.stack([pw_re.reshape(nb, gb * n), pw_im.reshape(nb, gb * n)], axis=1)
    return bbd, cbd, a, ap


def s5_scan(u, lam_re, lam_im, log_step, b_re, b_im, c_re, c_im, d_skip):
    b, s, w = u.shape
    seg_len = s // S5_NSEG
    lt = min(S5_LT, seg_len)
    assert (seg_len // lt) % 2 == 0
    bbd, cbd, a, ap = _s5_discretise(lam_re, lam_im, log_step, b_re, b_im, c_re, c_im, seg_len)
    nb, cw, ns2 = bbd.shape
    ns = ns2 // 2
    rows = lt * S5_NSEG
    return pl.pallas_call(
        functools.partial(_s5_kernel, seg_len=seg_len, lt=lt),
        grid=(b, nb),
        in_specs=[pl.BlockSpec((1, s, cw), lambda bb, j: (bb, 0, j)),
                  pl.BlockSpec((1, cw, ns2), lambda bb, j: (j, 0, 0)),
                  pl.BlockSpec((1, ns2, cw), lambda bb, j: (j, 0, 0)),
                  pl.BlockSpec((1, 2, ns), lambda bb, j: (j, 0, 0)),
                  pl.BlockSpec((1, 2, ns), lambda bb, j: (j, 0, 0)),
                  pl.BlockSpec((1, 1, cw), lambda bb, j: (j, 0, 0))],
        out_specs=pl.BlockSpec((1, s, cw), lambda bb, j: (bb, 0, j)),
        out_shape=jax.ShapeDtypeStruct((b, s, w), F32),
        scratch_shapes=[pltpu.VMEM((2, rows, cw), F32), pltpu.VMEM((2, rows, ns2), F32),
                        pltpu.VMEM((2, rows, ns2), F32), pltpu.VMEM((S5_NSEG, ns2), F32)],
        compiler_params=_params(("parallel", "parallel")),
        name="s5_scan",
    )(u, bbd, cbd, a, ap, d_skip.astype(F32).reshape(nb, 1, cw))


def _even_lat_kernel(lat_ref, gq_ref, gkv_ref, cos_ref, sin_ref, qn_ref, kvn_ref, kr_ref, *, q_rank, kv_rank):
    lat = lat_ref[...]
    q_lat = lat[:, 0:q_rank]
    kv_lat = lat[:, q_rank:q_rank + kv_rank]
    kr = lat[:, q_rank + kv_rank:q_rank + kv_rank + LANES]
    rq = lax.rsqrt(jnp.mean(q_lat * q_lat, axis=-1, keepdims=True) + EPS)
    qn_ref[...] = (q_lat * rq * gq_ref[...]).astype(qn_ref.dtype)
    rkv = lax.rsqrt(jnp.mean(kv_lat * kv_lat, axis=-1, keepdims=True) + EPS)
    kvn_ref[...] = (kv_lat * rkv * gkv_ref[...]).astype(kvn_ref.dtype)
    kr = _rope_tile(kr, cos_ref[...], sin_ref[...], MLA_ROPE // 2)
    lane = lax.broadcasted_iota(jnp.int32, kr.shape, 1)
    lo = jnp.where(lane < MLA_ROPE, kr, 0.0)
    hi = pltpu.roll(lo, MLA_ROPE, 1)
    kr_ref[:, 0:LANES] = lo.astype(kr_ref.dtype)
    kr_ref[:, LANES:2 * LANES] = hi.astype(kr_ref.dtype)


def even_lat_prep(lat, gq, gkv, cos_t, sin_t):
    m, wl = lat.shape
    q_rank, kv_rank = gq.shape[0], gkv.shape[0]
    tm = _tile(m, 512, SUBLANES)
    return pl.pallas_call(
        functools.partial(_even_lat_kernel, q_rank=q_rank, kv_rank=kv_rank),
        grid=(m // tm,),
        in_specs=[pl.BlockSpec((tm, wl), lambda i: (i, 0)),
                  pl.BlockSpec((1, q_rank), lambda i: (0, 0)),
                  pl.BlockSpec((1, kv_rank), lambda i: (0, 0)),
                  pl.BlockSpec((tm, LANES), lambda i: (i, 0)),
                  pl.BlockSpec((tm, LANES), lambda i: (i, 0))],
        out_specs=[pl.BlockSpec((tm, q_rank), lambda i: (i, 0)),
                   pl.BlockSpec((tm, kv_rank), lambda i: (i, 0)),
                   pl.BlockSpec((tm, 2 * LANES), lambda i: (i, 0))],
        out_shape=[jax.ShapeDtypeStruct((m, q_rank), BF16),
                   jax.ShapeDtypeStruct((m, kv_rank), BF16),
                   jax.ShapeDtypeStruct((m, 2 * LANES), BF16)],
        compiler_params=_params(("parallel",)),
        name="even_lat_prep",
    )(lat, gq.reshape(1, -1).astype(F32), gkv.reshape(1, -1).astype(F32), cos_t, sin_t)


def _mla_pack_kernel(q_ref, kv_ref, kr_ref, cos_ref, sin_ref, qf_ref, kf_ref, vf_ref, *, heads, scale):
    cos_t, sin_t = cos_ref[...], sin_ref[...]
    nope_w = heads * MLA_NOPE
    lane = lax.broadcasted_iota(jnp.int32, (q_ref.shape[1], LANES), 1)
    for pair in range(heads // 2):
        rope = _rope_tile(q_ref[0, :, nope_w + pair * LANES:nope_w + (pair + 1) * LANES],
                          cos_t, sin_t, MLA_ROPE // 2)
        for sub in range(2):
            h = 2 * pair + sub
            keep = (lane < MLA_ROPE) if sub == 0 else (lane >= MLA_ROPE)
            qf_ref[0, h, :, 0:LANES] = (q_ref[0, :, h * MLA_NOPE:(h + 1) * MLA_NOPE] * scale).astype(qf_ref.dtype)
            qf_ref[0, h, :, LANES:2 * LANES] = (jnp.where(keep, rope, 0.0) * scale).astype(qf_ref.dtype)
            kf_ref[0, h, :, 0:LANES] = kv_ref[0, :, h * 2 * LANES:h * 2 * LANES + LANES]
            kf_ref[0, h, :, LANES:2 * LANES] = kr_ref[0, :, sub * LANES:(sub + 1) * LANES]
            v_h = kv_ref[0, :, h * 2 * LANES + LANES:(h + 1) * 2 * LANES].astype(F32)
            vf_ref[0, h, 0:MLA_V, :] = v_h.T.astype(vf_ref.dtype)
            vf_ref[0, h, MLA_V:MLA_V + V_ONES_ROWS, :] = jnp.ones((V_ONES_ROWS, v_h.shape[0]), vf_ref.dtype)


def mla_pack(q, kv, kr2, cos_t, sin_t, heads):
    b, s, _ = q.shape
    tm = _tile(s, 256, SUBLANES)
    scale = (MLA_NOPE + MLA_ROPE) ** -0.5 * LOG2E
    return pl.pallas_call(
        functools.partial(_mla_pack_kernel, heads=heads, scale=scale),
        grid=(b, s // tm),
        in_specs=[pl.BlockSpec((1, tm, q.shape[2]), lambda bb, i: (bb, i, 0)),
                  pl.BlockSpec((1, tm, kv.shape[2]), lambda bb, i: (bb, i, 0)),
                  pl.BlockSpec((1, tm, 2 * LANES), lambda bb, i: (bb, i, 0)),
                  pl.BlockSpec((tm, LANES), lambda bb, i, nt=s // tm: (bb * nt + i, 0)),
                  pl.BlockSpec((tm, LANES), lambda bb, i, nt=s // tm: (bb * nt + i, 0))],
        out_specs=[pl.BlockSpec((1, heads, tm, 2 * LANES), lambda bb, i: (bb, 0, i, 0)),
                   pl.BlockSpec((1, heads, tm, 2 * LANES), lambda bb, i: (bb, 0, i, 0)),
                   pl.BlockSpec((1, heads, MLA_V + V_ONES_ROWS, tm), lambda bb, i: (bb, 0, 0, i))],
        out_shape=[jax.ShapeDtypeStruct((b, heads, s, 2 * LANES), BF16),
                   jax.ShapeDtypeStruct((b, heads, s, 2 * LANES), BF16),
                   jax.ShapeDtypeStruct((b, heads, MLA_V + V_ONES_ROWS, s), BF16)],
        compiler_params=_params(("parallel", "parallel")),
        name="mla_pack",
    )(q, kv, kr2, cos_t, sin_t)


def even_mixer(x2, h, b, s, cos_m, sin_m, w_in, lam_re, lam_im, log_step, b_re, b_im, c_re, c_im,
               d_skip, glu_a, glu_b, q_norm_g, w_q_up, kv_norm_g, w_kv_up, w_out):
    t, d = h.shape
    s5_w = d_skip.shape[0]
    q_rank, kv_rank = q_norm_g.shape[0], kv_norm_g.shape[0]
    heads = w_kv_up.shape[1] // (MLA_NOPE + MLA_V)
    lat_w = q_rank + kv_rank + MLA_ROPE
    lat_pad = -(-(q_rank + kv_rank + LANES) // (2 * LANES)) * (2 * LANES)

    w_u = w_in[:, :s5_w].astype(BF16)
    w_lat = jnp.pad(w_in[:, s5_w:], ((0, 0), (0, lat_pad - lat_w))).astype(BF16)
    u = matmul(h, w_u, F32)
    lat = matmul(h, w_lat, F32)

    y = s5_scan(u.reshape(b, s, s5_w), lam_re, lam_im, log_step, b_re, b_im, c_re, c_im, d_skip)
    s5_out = glu_matmul(y.reshape(t, s5_w), glu_a.astype(BF16), glu_b.astype(BF16), BF16)

    qn, kvn, kr2 = even_lat_prep(lat, q_norm_g, kv_norm_g, cos_m, sin_m)
    wq = w_q_up.reshape(q_rank, heads, MLA_NOPE + MLA_ROPE)
    wq = jnp.concatenate([wq[:, :, :MLA_NOPE].reshape(q_rank, heads * MLA_NOPE),
                          wq[:, :, MLA_NOPE:].reshape(q_rank, heads * MLA_ROPE)], axis=1).astype(BF16)
    q = matmul(qn, wq, F32)
    kv = matmul(kvn, w_kv_up.astype(BF16), BF16)
    qf, kf, vf = mla_pack(q.reshape(b, s, -1), kv.reshape(b, s, -1), kr2.reshape(b, s, -1), cos_m, sin_m, heads)
    tq = 2 * _tile(s, FLASH_TKB)
    mla_out = flash_attention(qf.reshape(b, heads, s // tq, tq, 2 * LANES), kf, vf, None,
                              tq=tq, rep=1, out_dtype=BF16)
    mixed = jnp.concatenate([s5_out, mla_out.reshape(t, -1)], axis=-1)
    return matmul(mixed, w_out.astype(BF16), F32, kind="residual", res=x2)


def _odd_lat_kernel(z_ref, gq_ref, lng_ref, lnb_ref, cos_ref, sin_ref,
                    qn_ref, k_ref, v_ref, kidx_ref, w_ref, *, q_rank, w_scale):
    cos_t, sin_t = cos_ref[...], sin_ref[...]
    half = ROT_DIM // 2
    z = z_ref[0]
    q_lat = z[:, 0:q_rank]
    rq = lax.rsqrt(jnp.mean(q_lat * q_lat, axis=-1, keepdims=True) + EPS)
    qn_ref[0] = (q_lat * rq * gq_ref[...]).astype(qn_ref.dtype)
    kvh = DSA_KV_HEADS
    d = DSA_HEAD_DIM
    for g in range(kvh):
        kh = z[:, q_rank + g * d:q_rank + (g + 1) * d]
        k_ref[0, g] = _rope_tile(kh, cos_t, sin_t, half).astype(k_ref.dtype)
        v_ref[0, g, 0:d, :] = z[:, q_rank + (kvh + g) * d:q_rank + (kvh + g + 1) * d].T.astype(v_ref.dtype)
        v_ref[0, g, d:d + V_ONES_ROWS, :] = jnp.ones((V_ONES_ROWS, z.shape[0]), v_ref.dtype)
    off = q_rank + 2 * kvh * d
    ki = z[:, off:off + IDX_DIM]
    kc = ki - jnp.mean(ki, axis=-1, keepdims=True)
    var = jnp.mean(kc * kc, axis=-1, keepdims=True)
    ki = kc * lax.rsqrt(var + EPS) * lng_ref[...] + lnb_ref[...]
    kidx_ref[0] = _rope_tile(ki, cos_t, sin_t, half).astype(kidx_ref.dtype)
    w_ref[0] = (z[:, off + IDX_DIM:off + IDX_DIM + LANES] * w_scale).T


def odd_lat_prep(z, gq, ln_g, ln_b, cos_t, sin_t, idx_heads):
    b, s, zw = z.shape
    q_rank = gq.shape[0]
    tm = _tile(s, 512, SUBLANES)
    nt = s // tm
    w_scale = idx_heads ** -0.5 * IDX_DIM ** -0.5
    kvh, d = DSA_KV_HEADS, DSA_HEAD_DIM
    return pl.pallas_call(
        functools.partial(_odd_lat_kernel, q_rank=q_rank, w_scale=w_scale),
        grid=(b, nt),
        in_specs=[pl.BlockSpec((1, tm, zw), lambda bb, i: (bb, i, 0)),
                  pl.BlockSpec((1, q_rank), lambda bb, i: (0, 0)),
                  pl.BlockSpec((1, IDX_DIM), lambda bb, i: (0, 0)),
                  pl.BlockSpec((1, IDX_DIM), lambda bb, i: (0, 0)),
                  pl.BlockSpec((tm, LANES), lambda bb, i: (bb * nt + i, 0)),
                  pl.BlockSpec((tm, LANES), lambda bb, i: (bb * nt + i, 0))],
        out_specs=[pl.BlockSpec((1, tm, q_rank), lambda bb, i: (bb, i, 0)),
                   pl.BlockSpec((1, kvh, tm, d), lambda bb, i: (bb, 0, i, 0)),
                   pl.BlockSpec((1, kvh, d + V_ONES_ROWS, tm), lambda bb, i: (bb, 0, 0, i)),
                   pl.BlockSpec((1, tm, IDX_DIM), lambda bb, i: (bb, i, 0)),
                   pl.BlockSpec((1, LANES, tm), lambda bb, i: (bb, 0, i))],
        out_shape=[jax.ShapeDtypeStruct((b, s, q_rank), BF16),
                   jax.ShapeDtypeStruct((b, kvh, s, d), BF16),
                   jax.ShapeDtypeStruct((b, kvh, d + V_ONES_ROWS, s), BF16),
                   jax.ShapeDtypeStruct((b, s, IDX_DIM), BF16),
                   jax.ShapeDtypeStruct((b, LANES, s), F32)],
        compiler_params=_params(("parallel", "parallel")),
        name="odd_lat_prep",
    )(z, gq.reshape(1, -1).astype(F32), ln_g.reshape(1, -1).astype(F32),
      ln_b.reshape(1, -1).astype(F32), cos_t, sin_t)


def _mm_rope_kernel(a_ref, w_ref, cos_ref, sin_ref, o_ref, *, n_heads, scale, stack_rows):
    acc = jnp.dot(a_ref[...], w_ref[...], preferred_element_type=F32)
    cos_t, sin_t = cos_ref[...], sin_ref[...]
    d = DSA_HEAD_DIM
    for r in range(n_heads):
        xh = _rope_tile(acc[:, r * d:(r + 1) * d], cos_t, sin_t, ROT_DIM // 2) * scale
        if stack_rows:
            o_ref[0, 0, 0, r * stack_rows:(r + 1) * stack_rows, :] = xh.astype(o_ref.dtype)
        else:
            o_ref[:, r * d:(r + 1) * d] = xh.astype(o_ref.dtype)


def dsa_q_proj(qn, w_q, cos_t, sin_t, b, s, tq):
    t, kq = qn.shape
    d = DSA_HEAD_DIM
    heads = w_q.shape[1] // d
    rep = heads // DSA_KV_HEADS
    nt = s // tq
    return pl.pallas_call(
        functools.partial(_mm_rope_kernel, n_heads=rep, scale=d ** -0.5 * LOG2E, stack_rows=tq),
        grid=(t // tq, DSA_KV_HEADS),
        in_specs=[pl.BlockSpec((tq, kq), lambda i, g: (i, 0)),
                  pl.BlockSpec((kq, rep * d), lambda i, g: (0, g)),
                  pl.BlockSpec((tq, LANES), lambda i, g: (i, 0)),
                  pl.BlockSpec((tq, LANES), lambda i, g: (i, 0))],
        out_specs=pl.BlockSpec((1, 1, 1, rep * tq, d), lambda i, g: (i // nt, g, i % nt, 0, 0)),
        out_shape=jax.ShapeDtypeStruct((b, DSA_KV_HEADS, nt, rep * tq, d), BF16),
        compiler_params=_params(("parallel", "parallel")),
        name="dsa_q_proj",
    )(qn, w_q, cos_t, sin_t)


def idx_q_proj(qn, w_qi, cos_t, sin_t):
    t, kq = qn.shape
    n = w_qi.shape[1]
    tm = _tile(t, 1024, SUBLANES)
    tn = _tile(n, 512)
    return pl.pallas_call(
        functools.partial(_mm_rope_kernel, n_heads=tn // IDX_DIM, scale=1.0, stack_rows=0),
        grid=(t // tm, n // tn),
        in_specs=[pl.BlockSpec((tm, kq), lambda i, j: (i, 0)),
                  pl.BlockSpec((kq, tn), lambda i, j: (0, j)),
                  pl.BlockSpec((tm, LANES), lambda i, j: (i, 0)),
                  pl.BlockSpec((tm, LANES), lambda i, j: (i, 0))],
        out_specs=pl.BlockSpec((tm, tn), lambda i, j: (i, j)),
        out_shape=jax.ShapeDtypeStruct((t, n), BF16),
        compiler_params=_params(("parallel", "parallel")),
        name="idx_q_proj",
    )(qn, w_qi, cos_t, sin_t)


def _ukey_to_float(u):
    bits = jnp.where(u < 0, u & jnp.int32(0x7FFFFFFF), ~u)
    return lax.bitcast_convert_type(bits, F32)


def _threshold_of_key(u):
    thr = _ukey_to_float(u)
    return jnp.where((thr != thr) & (u >= 0), jnp.float32(-jnp.inf), thr)


def _indexer_kernel(qi_ref, wt_ref, kidx_ref, bias_ref, score_ref, lim_ref, *, tq, tkb, idx_heads, top_k):
    i = pl.program_id(1)
    s_len = kidx_ref.shape[1]
    n_blocks = ((i + 1) * tq + tkb - 1) // tkb
    neg_inf = jnp.float32(-jnp.inf)
    qpos = i * tq + lax.broadcasted_iota(jnp.int32, (1, tq), 1)

    def causal(start):
        kpos = start + lax.broadcasted_iota(jnp.int32, (tkb, tq), 0)
        return kpos <= qpos

    def score_block(j, c):
        start = pl.multiple_of(j * tkb, tkb)
        kb = kidx_ref[0, pl.ds(start, tkb), :]
        acc = jnp.zeros((tkb, tq), F32)
        for h in range(idx_heads):
            logits = lax.dot_general(kb, qi_ref[0, :, h * IDX_DIM:(h + 1) * IDX_DIM],
                                     (((1,), (1,)), ((), ())), preferred_element_type=F32)
            acc = acc + jnp.maximum(logits, 0.0) * wt_ref[0, h:h + 1, :]
        score_ref[pl.ds(start, tkb), :] = jnp.where(causal(start), acc, neg_inf)
        return c

    lax.fori_loop(0, n_blocks, score_block, 0)

    def count_where(pred):
        def body(j, cnt):
            start = pl.multiple_of(j * tkb, tkb)
            ind = jnp.where(pred(score_ref[pl.ds(start, tkb), :], start), 1.0, 0.0)
            part = jnp.sum(ind.reshape(tkb // (8 * SUBLANES), 8, SUBLANES, tq), axis=1)
            return cnt + jnp.sum(part, axis=0)
        cnt = lax.fori_loop(0, n_blocks, body, jnp.zeros((SUBLANES, tq), F32))
        return jnp.sum(cnt, axis=0, keepdims=True)

    def count_ge(thr):
        return count_where(lambda blk, start: blk >= thr)

    k_f = jnp.float32(top_k)
    few = qpos < top_k

    def search_cond(st):
        bi, _, cnt_u = st
        pending = jnp.where(few | (cnt_u == k_f), 0.0, 1.0)
        return (bi < 32) & (jnp.max(pending) > 0.0)

    def search_step(st):
        bi, u, cnt_u = st
        cand = u | (jnp.int32(1) << (31 - bi))
        cnt = count_ge(_threshold_of_key(cand))
        take = cnt >= k_f
        return bi + 1, jnp.where(take, cand, u), jnp.where(take, cnt, cnt_u)

    total = (n_blocks * tkb).astype(F32)
    _, u, cnt_u = lax.while_loop(search_cond, search_step,
                                 (jnp.int32(0), jnp.zeros((1, tq), jnp.int32), jnp.full((1, tq), total, F32)))
    thr = jnp.where(few, neg_inf, _threshold_of_key(u))

    def key_pos(start):
        return start + lax.broadcasted_iota(jnp.int32, (tkb, tq), 0)

    excess = jnp.logical_not(few) & (cnt_u > k_f)
    lim_ref[...] = jnp.full((1, tq), s_len, jnp.int32)

    @pl.when(jnp.max(jnp.where(excess, 1.0, 0.0)) > 0.0)
    def _():
        need = k_f - count_where(lambda blk, start: blk > thr)
        nbits = s_len.bit_length()

        def tie_step(bi, p):
            cand = p | (jnp.int32(1) << (nbits - 1 - bi))
            cnt = count_where(lambda blk, start: (blk == thr) & (key_pos(start) < cand))
            return jnp.where(cnt <= need, cand, p)

        p = lax.fori_loop(0, nbits, tie_step, jnp.zeros((1, tq), jnp.int32))
        lim_ref[...] = jnp.where(excess, p, s_len)

    lim = lim_ref[...]

    def write_block(j, c):
        start = pl.multiple_of(j * tkb, tkb)
        blk = score_ref[pl.ds(start, tkb), :]
        keep = ((blk > thr) | ((blk == thr) & (key_pos(start) < lim))) & causal(start)
        bias_ref[0, pl.ds(start, tkb), :] = jnp.where(keep, 0.0, MASK_VALUE).astype(bias_ref.dtype)
        return c

    lax.fori_loop(0, n_blocks, write_block, 0)

    def fill_block(j, c):
        start = pl.multiple_of(j * tkb, tkb)
        bias_ref[0, pl.ds(start, tkb), :] = jnp.full((tkb, tq), MASK_VALUE, bias_ref.dtype)
        return c

    lax.fori_loop(n_blocks, s_len // tkb, fill_block, 0)


IDX_TQ = 512


def dsa_indexer(qi, wt, kidx, idx_heads, top_k):
    b, s, _ = qi.shape
    tq = _tile(s, IDX_TQ)
    tkb = _tile(s, 512)
    assert tkb >= top_k or tkb == s
    return pl.pallas_call(
        functools.partial(_indexer_kernel, tq=tq, tkb=tkb, idx_heads=idx_heads, top_k=top_k),
        grid=(b, s // tq),
        in_specs=[pl.BlockSpec((1, tq, idx_heads * IDX_DIM), lambda bb, i: (bb, i, 0)),
                  pl.BlockSpec((1, LANES, tq), lambda bb, i: (bb, 0, i)),
                  pl.BlockSpec((1, s, IDX_DIM), lambda bb, i: (bb, 0, 0))],
        out_specs=pl.BlockSpec((1, s, tq), lambda bb, i: (bb, 0, i)),
        out_shape=jax.ShapeDtypeStruct((b, s, s), BF16),
        scratch_shapes=[pltpu.VMEM((s, tq), F32), pltpu.VMEM((1, tq), jnp.int32)],
        compiler_params=_params(("parallel", "parallel")),
        name="dsa_indexer",
    )(qi, wt, kidx)


def odd_mixer(x2, h, b, s, cos_p, sin_p, w_in, q_norm_g, w_q_up, w_idx_q, k_ln_g, k_ln_b, w_out):
    t, d = h.shape
    q_rank = q_norm_g.shape[0]
    heads = w_q_up.shape[1] // DSA_HEAD_DIM
    idx_heads = w_idx_q.shape[1] // IDX_DIM
    in_w = w_in.shape[1]
    z_w = -(-(in_w - idx_heads + LANES) // (2 * LANES)) * (2 * LANES)
    w_z = jnp.pad(w_in, ((0, 0), (0, z_w - in_w))).astype(BF16)
    z = matmul(h, w_z, F32)
    qn, k, vt, kidx, wt = odd_lat_prep(z.reshape(b, s, z_w), q_norm_g, k_ln_g, k_ln_b, cos_p, sin_p, idx_heads)
    tq = _tile(s, 256)
    qn2 = qn.reshape(t, q_rank)
    q = dsa_q_proj(qn2, w_q_up.astype(BF16), cos_p, sin_p, b, s, tq)
    qi = idx_q_proj(qn2, w_idx_q.astype(BF16), cos_p, sin_p).reshape(b, s, -1)
    top_k = min(IDX_TOPK_MAX, s // 4)
    bias = dsa_indexer(qi, wt, kidx, idx_heads, top_k)
    o = flash_attention(q, k, vt, bias, tq=tq, rep=heads // DSA_KV_HEADS, out_dtype=BF16)
    return matmul(o.reshape(t, -1), w_out.astype(BF16), F32, kind="residual", res=x2)


def sq_relu_mlp(x2, h, w_up, w_down, layer):
    a = matmul(h, w_up, BF16, kind="relu2", layer=layer)
    return matmul(a, w_down, F32, kind="residual", res=x2, layer=layer)


def kernel(x, positions, norm_mix_g, norm_mlp_g, final_norm_g, even_w_in, s5_lam_re, s5_lam_im, s5_log_step, s5_b_re, s5_b_im, s5_c_re, s5_c_im, s5_d, s5_glu_a, s5_glu_b, mla_q_norm_g, mla_w_q_up, mla_kv_norm_g, mla_w_kv_up, even_w_out, odd_w_in, dsa_q_norm_g, dsa_w_q_up, idx_w_q, idx_k_ln_g, idx_k_ln_b, odd_w_out, mlp_w_up, mlp_w_down):
    b, s, d = x.shape
    depth = norm_mix_g.shape[0]
    cos_m, sin_m = _rope_tables(positions, MLA_ROPE, LANES)
    cos_p, sin_p = _rope_tables(positions, ROT_DIM, ROT_DIM)
    x2 = x.reshape(b * s, d)
    for layer in range(depth):
        i = layer // 2
        h = rmsnorm(x2, norm_mix_g[layer], BF16)
        if layer % 2 == 0:
            x2 = even_mixer(x2, h, b, s, cos_m, sin_m, even_w_in[i], s5_lam_re[i], s5_lam_im[i],
                            s5_log_step[i], s5_b_re[i], s5_b_im[i], s5_c_re[i], s5_c_im[i], s5_d[i],
                            s5_glu_a[i], s5_glu_b[i], mla_q_norm_g[i], mla_w_q_up[i],
                            mla_kv_norm_g[i], mla_w_kv_up[i], even_w_out[i])
        else:
            x2 = odd_mixer(x2, h, b, s, cos_p, sin_p, odd_w_in[i], dsa_q_norm_g[i], dsa_w_q_up[i],
                           idx_w_q[i], idx_k_ln_g[i], idx_k_ln_b[i], odd_w_out[i])
        h = rmsnorm(x2, norm_mlp_g[layer], BF16)
        x2 = sq_relu_mlp(x2, h, mlp_w_up, mlp_w_down, layer)
    return rmsnorm(x2, final_norm_g, x.dtype).reshape(b, s, d)
```

```python
import functools
import math

import jax
import jax.numpy as jnp
from jax import lax
from jax.experimental import pallas as pl
from jax.experimental.pallas import tpu as pltpu

F32 = jnp.float32
BF16 = jnp.bfloat16

EPS = 1e-6
ROPE_THETA = 500000.0
S5_GROUP = 16
S5_LAMBDA_RE_MAX = -1e-4
MLA_NOPE = 128
MLA_ROPE = 64
MLA_V = 128
DSA_HEAD_DIM = 128
DSA_KV_HEADS = 4
IDX_DIM = 128
IDX_TOPK_MAX = 256
ROT_DIM = DSA_HEAD_DIM // 4

LANES = 128
SUBLANES = 8
VMEM_LIMIT_BYTES = 56 * 1024 * 1024
MASK_VALUE = -1e30


def _params(semantics):
    return pltpu.CompilerParams(dimension_semantics=semantics, vmem_limit_bytes=VMEM_LIMIT_BYTES)


def _tile(dim, pref, align=LANES):
    if dim <= pref:
        return dim
    t = (pref // align) * align
    while t >= align:
        if dim % t == 0:
            return t
        t -= align
    return dim


def _rmsnorm_kernel(x_ref, g_ref, o_ref):
    x = x_ref[...].astype(F32)
    r = lax.rsqrt(jnp.mean(x * x, axis=-1, keepdims=True) + EPS)
    o_ref[...] = (x * r * g_ref[...]).astype(o_ref.dtype)


def rmsnorm(x, g, out_dtype):
    m, d = x.shape
    tm = _tile(m, 256, SUBLANES)
    return pl.pallas_call(
        _rmsnorm_kernel,
        grid=(m // tm,),
        in_specs=[pl.BlockSpec((tm, d), lambda i: (i, 0)),
                  pl.BlockSpec((1, d), lambda i: (0, 0))],
        out_specs=pl.BlockSpec((tm, d), lambda i: (i, 0)),
        out_shape=jax.ShapeDtypeStruct((m, d), out_dtype),
        compiler_params=_params(("parallel",)),
        name="rmsnorm",
    )(x, g.reshape(1, d).astype(F32))


def _mm_epilogue(acc, kind, res_ref):
    if kind == "relu2":
        a = jnp.maximum(acc, 0.0)
        return a * a
    if kind == "residual":
        return res_ref[...].astype(F32) + acc
    return acc


def _mm_kernel_single(*refs, kind):
    if kind == "residual":
        a_ref, w_ref, res_ref, o_ref = refs
    else:
        a_ref, w_ref, o_ref = refs
        res_ref = None
    acc = jnp.dot(a_ref[...].astype(BF16), w_ref[...].astype(BF16), preferred_element_type=F32)
    o_ref[...] = _mm_epilogue(acc, kind, res_ref).astype(o_ref.dtype)


def _mm_kernel_ksplit(*refs, kind, nk):
    if kind == "residual":
        a_ref, w_ref, res_ref, o_ref, acc_ref = refs
    else:
        a_ref, w_ref, o_ref, acc_ref = refs
        res_ref = None
    k = pl.program_id(2)

    @pl.when(k == 0)
    def _():
        acc_ref[...] = jnp.zeros_like(acc_ref)

    acc_ref[...] += jnp.dot(a_ref[...].astype(BF16), w_ref[...].astype(BF16), preferred_element_type=F32)

    @pl.when(k == nk - 1)
    def _():
        o_ref[...] = _mm_epilogue(acc_ref[...], kind, res_ref).astype(o_ref.dtype)


def matmul(a, w, out_dtype, kind="none", res=None, layer=None):
    m, k = a.shape
    n = w.shape[-1]
    tm_pref, tn_pref, tk_pref = (1024, 512, k) if k <= 4096 else (1024, 1024, 2048)
    tm = _tile(m, tm_pref, SUBLANES)
    tn = _tile(n, tn_pref)
    tk = _tile(k, tk_pref)

    def w_spec(tk_, tn_, index):
        if layer is None:
            return pl.BlockSpec((tk_, tn_), index)
        return pl.BlockSpec((None, tk_, tn_), lambda *g: (layer,) + index(*g))

    if tk == k:
        in_specs = [pl.BlockSpec((tm, k), lambda i, j: (i, 0)),
                    w_spec(k, tn, lambda i, j: (0, j))]
        args = [a, w]
        if kind == "residual":
            in_specs.append(pl.BlockSpec((tm, tn), lambda i, j: (i, j)))
            args.append(res)
        return pl.pallas_call(
            functools.partial(_mm_kernel_single, kind=kind),
            grid=(m // tm, n // tn),
            in_specs=in_specs,
            out_specs=pl.BlockSpec((tm, tn), lambda i, j: (i, j)),
            out_shape=jax.ShapeDtypeStruct((m, n), out_dtype),
            compiler_params=_params(("parallel", "parallel")),
            name="matmul",
        )(*args)
    nk = k // tk
    in_specs = [pl.BlockSpec((tm, tk), lambda i, j, kk: (i, kk)),
                w_spec(tk, tn, lambda i, j, kk: (kk, j))]
    args = [a, w]
    if kind == "residual":
        in_specs.append(pl.BlockSpec((tm, tn), lambda i, j, kk: (i, j)))
        args.append(res)
    return pl.pallas_call(
        functools.partial(_mm_kernel_ksplit, kind=kind, nk=nk),
        grid=(m // tm, n // tn, nk),
        in_specs=in_specs,
        out_specs=pl.BlockSpec((tm, tn), lambda i, j, kk: (i, j)),
        out_shape=jax.ShapeDtypeStruct((m, n), out_dtype),
        scratch_shapes=[pltpu.VMEM((tm, tn), F32)],
        compiler_params=_params(("parallel", "parallel", "arbitrary")),
        name="matmul_ksplit",
    )(*args)


def _glu_kernel(a_ref, wa_ref, wb_ref, o_ref):
    a = a_ref[...].astype(BF16)
    ya = jnp.dot(a, wa_ref[...], preferred_element_type=F32)
    yb = jnp.dot(a, wb_ref[...], preferred_element_type=F32)
    o_ref[...] = (ya * jax.nn.sigmoid(yb)).astype(o_ref.dtype)


def glu_matmul(a, wa, wb, out_dtype):
    m, k = a.shape
    _, n = wa.shape
    tm = _tile(m, 1024, SUBLANES)
    tn = _tile(n, 512)
    return pl.pallas_call(
        _glu_kernel,
        grid=(m // tm, n // tn),
        in_specs=[pl.BlockSpec((tm, k), lambda i, j: (i, 0)),
                  pl.BlockSpec((k, tn), lambda i, j: (0, j)),
                  pl.BlockSpec((k, tn), lambda i, j: (0, j))],
        out_specs=pl.BlockSpec((tm, tn), lambda i, j: (i, j)),
        out_shape=jax.ShapeDtypeStruct((m, n), out_dtype),
        compiler_params=_params(("parallel", "parallel")),
        name="glu_matmul",
    )(a, wa, wb)


def _rope_tile(x, cos_t, sin_t, half):
    lane = lax.broadcasted_iota(jnp.int32, x.shape, 1)
    first = (lane % (2 * half)) < half
    partner = jnp.where(first, pltpu.roll(x, LANES - half, 1), pltpu.roll(x, half, 1))
    return x * cos_t + partner * sin_t


def _rope_tables(positions, dim, pad_to):
    inv_freq = ROPE_THETA ** (-jnp.arange(0, dim, 2, dtype=F32) / dim)
    ang = positions.astype(F32).reshape(-1)[:, None] * inv_freq
    c, s = jnp.cos(ang), jnp.sin(ang)
    reps = pad_to // dim
    cos_t = jnp.tile(jnp.concatenate([c, c], axis=-1), (1, reps))
    sin_t = jnp.tile(jnp.concatenate([-s, s], axis=-1), (1, reps))
    t = c.shape[0]
    cos_t = jnp.concatenate([cos_t, jnp.ones((t, LANES - pad_to), F32)], axis=-1)
    sin_t = jnp.concatenate([sin_t, jnp.zeros((t, LANES - pad_to), F32)], axis=-1)
    return cos_t, sin_t


V_ONES_ROWS = 16
FLASH_TKB = 512
LOG2E = math.log2(math.e)


def _flash_kernel(*refs, tq, tkb, rep, dv, has_bias):
    if has_bias:
        q_ref, k_ref, vt_ref, b_ref, o_ref, m_ref, acc_ref, sa_ref, sb_ref = refs
    else:
        q_ref, k_ref, vt_ref, o_ref, m_ref, acc_ref, sa_ref, sb_ref = refs
        b_ref = None
    i = pl.program_id(2)
    rows = rep * tq

    m_ref[...] = jnp.full(m_ref.shape, MASK_VALUE, F32)
    acc_ref[...] = jnp.zeros(acc_ref.shape, F32)
    q = q_ref[0, 0, 0]

    def qk(j, dst_ref):
        start = pl.multiple_of(j * tkb, tkb)
        kb = k_ref[0, 0, pl.ds(start, tkb), :]
        dst_ref[...] = lax.dot_general(kb, q, (((1,), (1,)), ((), ())),
                                       preferred_element_type=F32)

    def softmax_pv(j, src_ref, masked):
        start = pl.multiple_of(j * tkb, tkb)
        vtb = vt_ref[0, 0, :, pl.ds(start, tkb)]
        s = src_ref[...]
        if has_bias:
            bias = b_ref[0, pl.ds(start, tkb), :].astype(F32)
            s = s + jnp.tile(bias, (1, rep))
        if masked:
            kpos = start + lax.broadcasted_iota(jnp.int32, (tkb, rows), 0)
            qlane = lax.broadcasted_iota(jnp.int32, (tkb, rows), 1)
            qpos = i * tq + (qlane % tq if rep > 1 else qlane)
            s = jnp.where(kpos <= qpos, s, MASK_VALUE)
        m_prev = m_ref[...]
        m_new = jnp.maximum(m_prev, jnp.max(s, axis=0, keepdims=True))
        p = jnp.exp2(s - m_new)
        alpha = jnp.exp2(m_prev - m_new)
        acc_ref[...] = alpha * acc_ref[...] + jnp.dot(vtb, p.astype(BF16), preferred_element_type=F32)
        m_ref[...] = m_new

    if has_bias:
        qk(0, sa_ref)
        n_blocks = ((i + 1) * tq + tkb - 1) // tkb

        def pair(jj, c):
            j0 = 2 * jj
            qk(j0 + 1, sb_ref)
            softmax_pv(j0, sa_ref, False)
            qk(jnp.minimum(j0 + 2, n_blocks - 1), sa_ref)
            softmax_pv(j0 + 1, sb_ref, False)
            return c

        lax.fori_loop(0, n_blocks // 2, pair, 0)

        @pl.when(n_blocks % 2 == 1)
        def _():
            softmax_pv(n_blocks - 1, sa_ref, False)
    else:
        qk(2 * i, sa_ref)
        qk(2 * i + 1, sb_ref)
        softmax_pv(2 * i, sa_ref, True)
        qk(0, sa_ref)
        softmax_pv(2 * i + 1, sb_ref, True)

        def pair(jj, c):
            j0 = 2 * jj
            qk(j0 + 1, sb_ref)
            softmax_pv(j0, sa_ref, False)
            qk(jnp.minimum(j0 + 2, 2 * i - 1), sa_ref)
            softmax_pv(j0 + 1, sb_ref, False)
            return c

        lax.fori_loop(0, i, pair, 0)

    acc = acc_ref[...]
    out = (acc[0:dv] * (1.0 / acc[dv:dv + 1])).T
    for r in range(rep):
        o_ref[0, :, r * dv:(r + 1) * dv] = out[r * tq:(r + 1) * tq].astype(o_ref.dtype)


def flash_attention(q, k, vt, bias, *, tq, rep, out_dtype):
    b, hk, nq, rows, dq = q.shape
    s = k.shape[2]
    dve = vt.shape[2]
    dv = dve - V_ONES_ROWS
    tkb = _tile(s, FLASH_TKB)
    has_bias = bias is not None
    assert has_bias or tq == 2 * tkb
    in_specs = [pl.BlockSpec((1, 1, 1, rows, dq), lambda bb, h, i: (bb, h, i, 0, 0)),
                pl.BlockSpec((1, 1, s, dq), lambda bb, h, i: (bb, h, 0, 0)),
                pl.BlockSpec((1, 1, dve, s), lambda bb, h, i: (bb, h, 0, 0))]
    args = [q, k, vt]
    if has_bias:
        in_specs.append(pl.BlockSpec((1, s, tq), lambda bb, h, i: (bb, 0, i)))
        args.append(bias)
    return pl.pallas_call(
        functools.partial(_flash_kernel, tq=tq, tkb=tkb, rep=rep, dv=dv, has_bias=has_bias),
        grid=(b, hk, nq),
        in_specs=in_specs,
        out_specs=pl.BlockSpec((1, tq, rep * dv), lambda bb, h, i: (bb, i, h)),
        out_shape=jax.ShapeDtypeStruct((b, s, hk * rep * dv), out_dtype),
        scratch_shapes=[pltpu.VMEM((1, rows), F32), pltpu.VMEM((dve, rows), F32),
                        pltpu.VMEM((tkb, rows), F32), pltpu.VMEM((tkb, rows), F32)],
        compiler_params=_params(("parallel", "parallel", "parallel")),
        name="flash_attention",
    )(*args)


S5_NSEG = 2 * SUBLANES
S5_CBLK = LANES
S5_LT = 16


def _s5_kernel(u_ref, bbd_ref, cbd_ref, a_ref, ap_ref, d_ref, y_ref,
               ug_ref, bu_ref, xs_ref, init_ref, *, seg_len, lt):
    nseg = S5_NSEG
    ns = a_ref.shape[-1]
    a_re = jnp.broadcast_to(a_ref[0, 0:1, :], (nseg, ns))
    a_im = jnp.broadcast_to(a_ref[0, 1:2, :], (nseg, ns))
    ntiles = seg_len // lt

    def load_inputs(t, slot):
        for i in range(lt):
            ug_ref[slot, i * nseg:(i + 1) * nseg, :] = u_ref[0, pl.ds(t * lt + i, nseg, stride=seg_len), :]
        bu_ref[slot] = jnp.dot(ug_ref[slot].astype(BF16), bbd_ref[0], preferred_element_type=F32)

    def scan_tile(carry, slot, store):
        x_re, x_im = carry
        for i in range(lt):
            r0 = i * nseg
            b_re = bu_ref[slot, r0:r0 + nseg, 0:ns]
            b_im = bu_ref[slot, r0:r0 + nseg, ns:2 * ns]
            n_re = a_re * x_re - a_im * x_im + b_re
            n_im = a_re * x_im + a_im * x_re + b_im
            x_re, x_im = n_re, n_im
            if store:
                xs_ref[slot, r0:r0 + nseg, 0:ns] = x_re
                xs_ref[slot, r0:r0 + nseg, ns:2 * ns] = x_im
        return x_re, x_im

    def emit(t, slot):
        y = jnp.dot(xs_ref[slot].astype(BF16), cbd_ref[0], preferred_element_type=F32)
        y = jax.nn.gelu(y + d_ref[0] * ug_ref[slot])
        for i in range(lt):
            y_ref[0, pl.ds(t * lt + i, nseg, stride=seg_len), :] = y[i * nseg:(i + 1) * nseg]

    def sweep(carry, store):
        load_inputs(0, 0)

        def pair(tt, c):
            t0 = 2 * tt
            load_inputs(t0 + 1, 1)
            c = scan_tile(c, 0, store)
            if store:
                emit(t0, 0)
            load_inputs(jnp.minimum(t0 + 2, ntiles - 1), 0)
            c = scan_tile(c, 1, store)
            if store:
                emit(t0 + 1, 1)
            return c

        return lax.fori_loop(0, ntiles // 2, pair, carry)

    zeros = jnp.zeros((nseg, ns), F32)
    e_re, e_im = sweep((zeros, zeros), False)

    ap_re = ap_ref[0, 0:1, :]
    ap_im = ap_ref[0, 1:2, :]
    t_re = jnp.zeros((1, ns), F32)
    t_im = jnp.zeros((1, ns), F32)
    init_ref[0:1, :] = jnp.zeros((1, 2 * ns), F32)
    for s in range(1, nseg):
        p_re, p_im = e_re[s - 1:s], e_im[s - 1:s]
        t_re, t_im = (ap_re * t_re - ap_im * t_im + p_re,
                      ap_re * t_im + ap_im * t_re + p_im)
        init_ref[s:s + 1, 0:ns] = t_re
        init_ref[s:s + 1, ns:2 * ns] = t_im

    sweep((init_ref[:, 0:ns], init_ref[:, ns:2 * ns]), True)


def _s5_discretise(lam_re, lam_im, log_step, b_re, b_im, c_re, c_im, seg_len):
    g, n = lam_re.shape
    p = S5_GROUP
    lr = jnp.minimum(lam_re.astype(F32), S5_LAMBDA_RE_MAX)
    li = lam_im.astype(F32)
    step = jnp.exp(log_step.astype(F32))[:, None]
    mag = jnp.exp(lr * step)
    ab_re = mag * jnp.cos(li * step)
    ab_im = mag * jnp.sin(li * step)
    den = lr * lr + li * li
    f_re = ((ab_re - 1.0) * lr + ab_im * li) / den
    f_im = (ab_im * lr - (ab_re - 1.0) * li) / den
    br = b_re.astype(F32)
    bi = b_im.astype(F32)
    bb_re = f_re[..., None] * br - f_im[..., None] * bi
    bb_im = f_re[..., None] * bi + f_im[..., None] * br
    pw_re, pw_im = jnp.ones_like(ab_re), jnp.zeros_like(ab_im)
    sq_re, sq_im = ab_re, ab_im
    e = seg_len
    while e:
        if e & 1:
            pw_re, pw_im = pw_re * sq_re - pw_im * sq_im, pw_re * sq_im + pw_im * sq_re
        sq_re, sq_im = sq_re * sq_re - sq_im * sq_im, 2.0 * sq_re * sq_im
        e >>= 1
    gb = S5_CBLK // p
    nb = g // gb
    eye = jnp.eye(gb, dtype=F32)

    def blockdiag_in(bb):
        x = bb.reshape(nb, gb, n, p)
        return jnp.einsum("bgnp,gh->bgphn", x, eye).reshape(nb, gb * p, gb * n)

    def blockdiag_out(c):
        x = c.astype(F32).reshape(nb, gb, p, n)
        return jnp.einsum("bgpn,gh->bgnhp", x, eye).reshape(nb, gb * n, gb * p)

    bbd = jnp.concatenate([blockdiag_in(bb_re), blockdiag_in(bb_im)], axis=2).astype(BF16)
    cbd = jnp.concatenate([blockdiag_out(c_re), -blockdiag_out(c_im)], axis=1).astype(BF16)
    a = jnp.stack([ab_re.reshape(nb, gb * n), ab_im.reshape(nb, gb * n)], axis=1)
    ap = jnp.stack([pw_re.reshape(nb, gb * n), pw_im.reshape(nb, gb * n)], axis=1)
    return bbd, cbd, a, ap


def s5_scan(u, lam_re, lam_im, log_step, b_re, b_im, c_re, c_im, d_skip):
    b, s, w = u.shape
    seg_len = s // S5_NSEG
    lt = min(S5_LT, seg_len)
    assert (seg_len // lt) % 2 == 0
    bbd, cbd, a, ap = _s5_discretise(lam_re, lam_im, log_step, b_re, b_im, c_re, c_im, seg_len)
    nb, cw, ns2 = bbd.shape
    ns = ns2 // 2
    rows = lt * S5_NSEG
    return pl.pallas_call(
        functools.partial(_s5_kernel, seg_len=seg_len, lt=lt),
        grid=(b, nb),
        in_specs=[pl.BlockSpec((1, s, cw), lambda bb, j: (bb, 0, j)),
                  pl.BlockSpec((1, cw, ns2), lambda bb, j: (j, 0, 0)),
                  pl.BlockSpec((1, ns2, cw), lambda bb, j: (j, 0, 0)),
                  pl.BlockSpec((1, 2, ns), lambda bb, j: (j, 0, 0)),
                  pl.BlockSpec((1, 2, ns), lambda bb, j: (j, 0, 0)),
                  pl.BlockSpec((1, 1, cw), lambda bb, j: (j, 0, 0))],
        out_specs=pl.BlockSpec((1, s, cw), lambda bb, j: (bb, 0, j)),
        out_shape=jax.ShapeDtypeStruct((b, s, w), F32),
        scratch_shapes=[pltpu.VMEM((2, rows, cw), F32), pltpu.VMEM((2, rows, ns2), F32),
                        pltpu.VMEM((2, rows, ns2), F32), pltpu.VMEM((S5_NSEG, ns2), F32)],
        compiler_params=_params(("parallel", "parallel")),
        name="s5_scan",
    )(u, bbd, cbd, a, ap, d_skip.astype(F32).reshape(nb, 1, cw))


def _even_lat_kernel(lat_ref, gq_ref, gkv_ref, cos_ref, sin_ref, qn_ref, kvn_ref, kr_ref, *, q_rank, kv_rank):
    lat = lat_ref[...]
    q_lat = lat[:, 0:q_rank]
    kv_lat = lat[:, q_rank:q_rank + kv_rank]
    kr = lat[:, q_rank + kv_rank:q_rank + kv_rank + LANES]
    rq = lax.rsqrt(jnp.mean(q_lat * q_lat, axis=-1, keepdims=True) + EPS)
    qn_ref[...] = (q_lat * rq * gq_ref[...]).astype(qn_ref.dtype)
    rkv = lax.rsqrt(jnp.mean(kv_lat * kv_lat, axis=-1, keepdims=True) + EPS)
    kvn_ref[...] = (kv_lat * rkv * gkv_ref[...]).astype(kvn_ref.dtype)
    kr = _rope_tile(kr, cos_ref[...], sin_ref[...], MLA_ROPE // 2)
    lane = lax.broadcasted_iota(jnp.int32, kr.shape, 1)
    lo = jnp.where(lane < MLA_ROPE, kr, 0.0)
    hi = pltpu.roll(lo, MLA_ROPE, 1)
    kr_ref[:, 0:LANES] = lo.astype(kr_ref.dtype)
    kr_ref[:, LANES:2 * LANES] = hi.astype(kr_ref.dtype)


def even_lat_prep(lat, gq, gkv, cos_t, sin_t):
    m, wl = lat.shape
    q_rank, kv_rank = gq.shape[0], gkv.shape[0]
    tm = _tile(m, 512, SUBLANES)
    return pl.pallas_call(
        functools.partial(_even_lat_kernel, q_rank=q_rank, kv_rank=kv_rank),
        grid=(m // tm,),
        in_specs=[pl.BlockSpec((tm, wl), lambda i: (i, 0)),
                  pl.BlockSpec((1, q_rank), lambda i: (0, 0)),
                  pl.BlockSpec((1, kv_rank), lambda i: (0, 0)),
                  pl.BlockSpec((tm, LANES), lambda i: (i, 0)),
                  pl.BlockSpec((tm, LANES), lambda i: (i, 0))],
        out_specs=[pl.BlockSpec((tm, q_rank), lambda i: (i, 0)),
                   pl.BlockSpec((tm, kv_rank), lambda i: (i, 0)),
                   pl.BlockSpec((tm, 2 * LANES), lambda i: (i, 0))],
        out_shape=[jax.ShapeDtypeStruct((m, q_rank), BF16),
                   jax.ShapeDtypeStruct((m, kv_rank), BF16),
                   jax.ShapeDtypeStruct((m, 2 * LANES), BF16)],
        compiler_params=_params(("parallel",)),
        name="even_lat_prep",
    )(lat, gq.reshape(1, -1).astype(F32), gkv.reshape(1, -1).astype(F32), cos_t, sin_t)


def _mla_q_kernel(qn_ref, w_ref, cos_ref, sin_ref, qf_ref, *, heads, scale):
    q = jnp.dot(qn_ref[...], w_ref[...], preferred_element_type=F32)
    cos_t, sin_t = cos_ref[...], sin_ref[...]
    nope_w = heads * MLA_NOPE
    lane = lax.broadcasted_iota(jnp.int32, (q.shape[0], LANES), 1)
    for pair in range(heads // 2):
        rope = _rope_tile(q[:, nope_w + pair * LANES:nope_w + (pair + 1) * LANES], cos_t, sin_t, MLA_ROPE // 2)
        for sub in range(2):
            h = 2 * pair + sub
            keep = (lane < MLA_ROPE) if sub == 0 else (lane >= MLA_ROPE)
            qf_ref[0, h, :, 0:LANES] = (q[:, h * MLA_NOPE:(h + 1) * MLA_NOPE] * scale).astype(qf_ref.dtype)
            qf_ref[0, h, :, LANES:2 * LANES] = (jnp.where(keep, rope, 0.0) * scale).astype(qf_ref.dtype)


def _mla_kv_kernel(kvn_ref, w_ref, kr_ref, kf_ref, vt_ref, *, heads):
    kv = jnp.dot(kvn_ref[...], w_ref[...], preferred_element_type=F32)
    for h in range(heads):
        c0 = h * (MLA_NOPE + MLA_V)
        kf_ref[0, h, :, 0:LANES] = kv[:, c0:c0 + MLA_NOPE].astype(kf_ref.dtype)
        kf_ref[0, h, :, LANES:2 * LANES] = kr_ref[:, (h % 2) * LANES:(h % 2 + 1) * LANES]
        vt_ref[0, h, 0:MLA_V, :] = kv[:, c0 + MLA_NOPE:c0 + MLA_NOPE + MLA_V].T.astype(vt_ref.dtype)
        vt_ref[0, h, MLA_V:MLA_V + V_ONES_ROWS, :] = jnp.ones((V_ONES_ROWS, kv.shape[0]), vt_ref.dtype)


def mla_projections(qn, kvn, kr2, wq, wkv, cos_t, sin_t, b, s, heads):
    t = qn.shape[0]
    tm = _tile(s, 256, SUBLANES)
    nt = s // tm
    scale = (MLA_NOPE + MLA_ROPE) ** -0.5 * LOG2E
    head_spec = pl.BlockSpec((1, heads, tm, 2 * LANES), lambda i: (i // nt, 0, i % nt, 0))
    qf = pl.pallas_call(
        functools.partial(_mla_q_kernel, heads=heads, scale=scale),
        grid=(t // tm,),
        in_specs=[pl.BlockSpec((tm, qn.shape[1]), lambda i: (i, 0)),
                  pl.BlockSpec(wq.shape, lambda i: (0, 0)),
                  pl.BlockSpec((tm, LANES), lambda i: (i, 0)),
                  pl.BlockSpec((tm, LANES), lambda i: (i, 0))],
        out_specs=head_spec,
        out_shape=jax.ShapeDtypeStruct((b, heads, s, 2 * LANES), BF16),
        compiler_params=_params(("parallel",)),
        name="mla_q_proj",
    )(qn, wq, cos_t, sin_t)
    kf, vt = pl.pallas_call(
        functools.partial(_mla_kv_kernel, heads=heads),
        grid=(t // tm,),
        in_specs=[pl.BlockSpec((tm, kvn.shape[1]), lambda i: (i, 0)),
                  pl.BlockSpec(wkv.shape, lambda i: (0, 0)),
                  pl.BlockSpec((tm, 2 * LANES), lambda i: (i, 0))],
        out_specs=[head_spec,
                   pl.BlockSpec((1, heads, MLA_V + V_ONES_ROWS, tm), lambda i: (i // nt, 0, 0, i % nt))],
        out_shape=[jax.ShapeDtypeStruct((b, heads, s, 2 * LANES), BF16),
                   jax.ShapeDtypeStruct((b, heads, MLA_V + V_ONES_ROWS, s), BF16)],
        compiler_params=_params(("parallel",)),
        name="mla_kv_proj",
    )(kvn, wkv, kr2)
    return qf, kf, vt


def even_mixer(x2, h, b, s, cos_m, sin_m, w_in, lam_re, lam_im, log_step, b_re, b_im, c_re, c_im,
               d_skip, glu_a, glu_b, q_norm_g, w_q_up, kv_norm_g, w_kv_up, w_out):
    t, d = h.shape
    s5_w = d_skip.shape[0]
    q_rank, kv_rank = q_norm_g.shape[0], kv_norm_g.shape[0]
    heads = w_kv_up.shape[1] // (MLA_NOPE + MLA_V)
    lat_w = q_rank + kv_rank + MLA_ROPE
    lat_pad = -(-(q_rank + kv_rank + LANES) // (2 * LANES)) * (2 * LANES)

    w_u = w_in[:, :s5_w].astype(BF16)
    w_lat = jnp.pad(w_in[:, s5_w:], ((0, 0), (0, lat_pad - lat_w))).astype(BF16)
    u = matmul(h, w_u, F32)
    lat = matmul(h, w_lat, F32)

    y = s5_scan(u.reshape(b, s, s5_w), lam_re, lam_im, log_step, b_re, b_im, c_re, c_im, d_skip)
    s5_out = glu_matmul(y.reshape(t, s5_w), glu_a.astype(BF16), glu_b.astype(BF16), BF16)

    qn, kvn, kr2 = even_lat_prep(lat, q_norm_g, kv_norm_g, cos_m, sin_m)
    wq = w_q_up.reshape(q_rank, heads, MLA_NOPE + MLA_ROPE)
    wq = jnp.concatenate([wq[:, :, :MLA_NOPE].reshape(q_rank, heads * MLA_NOPE),
                          wq[:, :, MLA_NOPE:].reshape(q_rank, heads * MLA_ROPE)], axis=1).astype(BF16)
    qf, kf, vf = mla_projections(qn, kvn, kr2, wq, w_kv_up.astype(BF16), cos_m, sin_m, b, s, heads)
    tq = 2 * _tile(s, FLASH_TKB)
    mla_out = flash_attention(qf.reshape(b, heads, s // tq, tq, 2 * LANES), kf, vf, None,
                              tq=tq, rep=1, out_dtype=BF16)
    mixed = jnp.concatenate([s5_out, mla_out.reshape(t, -1)], axis=-1)
    return matmul(mixed, w_out.astype(BF16), F32, kind="residual", res=x2)


def _odd_lat_kernel(z_ref, gq_ref, lng_ref, lnb_ref, cos_ref, sin_ref,
                    qn_ref, k_ref, v_ref, kidx_ref, w_ref, *, q_rank, w_scale):
    cos_t, sin_t = cos_ref[...], sin_ref[...]
    half = ROT_DIM // 2
    z = z_ref[0]
    q_lat = z[:, 0:q_rank]
    rq = lax.rsqrt(jnp.mean(q_lat * q_lat, axis=-1, keepdims=True) + EPS)
    qn_ref[0] = (q_lat * rq * gq_ref[...]).astype(qn_ref.dtype)
    kvh = DSA_KV_HEADS
    d = DSA_HEAD_DIM
    for g in range(kvh):
        kh = z[:, q_rank + g * d:q_rank + (g + 1) * d]
        k_ref[0, g] = _rope_tile(kh, cos_t, sin_t, half).astype(k_ref.dtype)
        v_ref[0, g, 0:d, :] = z[:, q_rank + (kvh + g) * d:q_rank + (kvh + g + 1) * d].T.astype(v_ref.dtype)
        v_ref[0, g, d:d + V_ONES_ROWS, :] = jnp.ones((V_ONES_ROWS, z.shape[0]), v_ref.dtype)
    off = q_rank + 2 * kvh * d
    ki = z[:, off:off + IDX_DIM]
    kc = ki - jnp.mean(ki, axis=-1, keepdims=True)
    var = jnp.mean(kc * kc, axis=-1, keepdims=True)
    ki = kc * lax.rsqrt(var + EPS) * lng_ref[...] + lnb_ref[...]
    kidx_ref[0] = _rope_tile(ki, cos_t, sin_t, half).astype(kidx_ref.dtype)
    w_ref[0] = (z[:, off + IDX_DIM:off + IDX_DIM + LANES] * w_scale).T


def odd_lat_prep(z, gq, ln_g, ln_b, cos_t, sin_t, idx_heads):
    b, s, zw = z.shape
    q_rank = gq.shape[0]
    tm = _tile(s, 512, SUBLANES)
    nt = s // tm
    w_scale = idx_heads ** -0.5 * IDX_DIM ** -0.5
    kvh, d = DSA_KV_HEADS, DSA_HEAD_DIM
    return pl.pallas_call(
        functools.partial(_odd_lat_kernel, q_rank=q_rank, w_scale=w_scale),
        grid=(b, nt),
        in_specs=[pl.BlockSpec((1, tm, zw), lambda bb, i: (bb, i, 0)),
                  pl.BlockSpec((1, q_rank), lambda bb, i: (0, 0)),
                  pl.BlockSpec((1, IDX_DIM), lambda bb, i: (0, 0)),
                  pl.BlockSpec((1, IDX_DIM), lambda bb, i: (0, 0)),
                  pl.BlockSpec((tm, LANES), lambda bb, i: (bb * nt + i, 0)),
                  pl.BlockSpec((tm, LANES), lambda bb, i: (bb * nt + i, 0))],
        out_specs=[pl.BlockSpec((1, tm, q_rank), lambda bb, i: (bb, i, 0)),
                   pl.BlockSpec((1, kvh, tm, d), lambda bb, i: (bb, 0, i, 0)),
                   pl.BlockSpec((1, kvh, d + V_ONES_ROWS, tm), lambda bb, i: (bb, 0, 0, i)),
                   pl.BlockSpec((1, tm, IDX_DIM), lambda bb, i: (bb, i, 0)),
                   pl.BlockSpec((1, LANES, tm), lambda bb, i: (bb, 0, i))],
        out_shape=[jax.ShapeDtypeStruct((b, s, q_rank), BF16),
                   jax.ShapeDtypeStruct((b, kvh, s, d), BF16),
                   jax.ShapeDtypeStruct((b, kvh, d + V_ONES_ROWS, s), BF16),
                   jax.ShapeDtypeStruct((b, s, IDX_DIM), BF16),
                   jax.ShapeDtypeStruct((b, LANES, s), F32)],
        compiler_params=_params(("parallel", "parallel")),
        name="odd_lat_prep",
    )(z, gq.reshape(1, -1).astype(F32), ln_g.reshape(1, -1).astype(F32),
      ln_b.reshape(1, -1).astype(F32), cos_t, sin_t)


def _mm_rope_kernel(a_ref, w_ref, cos_ref, sin_ref, o_ref, *, n_heads, scale, stack_rows):
    acc = jnp.dot(a_ref[...], w_ref[...], preferred_element_type=F32)
    cos_t, sin_t = cos_ref[...], sin_ref[...]
    d = DSA_HEAD_DIM
    for r in range(n_heads):
        xh = _rope_tile(acc[:, r * d:(r + 1) * d], cos_t, sin_t, ROT_DIM // 2) * scale
        if stack_rows:
            o_ref[0, 0, 0, r * stack_rows:(r + 1) * stack_rows, :] = xh.astype(o_ref.dtype)
        else:
            o_ref[:, r * d:(r + 1) * d] = xh.astype(o_ref.dtype)


def dsa_q_proj(qn, w_q, cos_t, sin_t, b, s, tq):
    t, kq = qn.shape
    d = DSA_HEAD_DIM
    heads = w_q.shape[1] // d
    rep = heads // DSA_KV_HEADS
    nt = s // tq
    return pl.pallas_call(
        functools.partial(_mm_rope_kernel, n_heads=rep, scale=d ** -0.5 * LOG2E, stack_rows=tq),
        grid=(t // tq, DSA_KV_HEADS),
        in_specs=[pl.BlockSpec((tq, kq), lambda i, g: (i, 0)),
                  pl.BlockSpec((kq, rep * d), lambda i, g: (0, g)),
                  pl.BlockSpec((tq, LANES), lambda i, g: (i, 0)),
                  pl.BlockSpec((tq, LANES), lambda i, g: (i, 0))],
        out_specs=pl.BlockSpec((1, 1, 1, rep * tq, d), lambda i, g: (i // nt, g, i % nt, 0, 0)),
        out_shape=jax.ShapeDtypeStruct((b, DSA_KV_HEADS, nt, rep * tq, d), BF16),
        compiler_params=_params(("parallel", "parallel")),
        name="dsa_q_proj",
    )(qn, w_q, cos_t, sin_t)


def idx_q_proj(qn, w_qi, cos_t, sin_t):
    t, kq = qn.shape
    n = w_qi.shape[1]
    tm = _tile(t, 1024, SUBLANES)
    tn = _tile(n, 512)
    return pl.pallas_call(
        functools.partial(_mm_rope_kernel, n_heads=tn // IDX_DIM, scale=1.0, stack_rows=0),
        grid=(t // tm, n // tn),
        in_specs=[pl.BlockSpec((tm, kq), lambda i, j: (i, 0)),
                  pl.BlockSpec((kq, tn), lambda i, j: (0, j)),
                  pl.BlockSpec((tm, LANES), lambda i, j: (i, 0)),
                  pl.BlockSpec((tm, LANES), lambda i, j: (i, 0))],
        out_specs=pl.BlockSpec((tm, tn), lambda i, j: (i, j)),
        out_shape=jax.ShapeDtypeStruct((t, n), BF16),
        compiler_params=_params(("parallel", "parallel")),
        name="idx_q_proj",
    )(qn, w_qi, cos_t, sin_t)


def _ukey_to_float(u):
    bits = jnp.where(u < 0, u & jnp.int32(0x7FFFFFFF), ~u)
    return lax.bitcast_convert_type(bits, F32)


def _threshold_of_key(u):
    thr = _ukey_to_float(u)
    return jnp.where((thr != thr) & (u >= 0), jnp.float32(-jnp.inf), thr)


def _indexer_kernel(qi_ref, wt_ref, kidx_ref, bias_ref, score_ref, lim_ref, *, tq, tkb, idx_heads, top_k):
    i = pl.program_id(1)
    s_len = kidx_ref.shape[1]
    n_blocks = ((i + 1) * tq + tkb - 1) // tkb
    neg_inf = jnp.float32(-jnp.inf)
    qpos = i * tq + lax.broadcasted_iota(jnp.int32, (1, tq), 1)

    def causal(start):
        kpos = start + lax.broadcasted_iota(jnp.int32, (tkb, tq), 0)
        return kpos <= qpos

    def score_block(j, c):
        start = pl.multiple_of(j * tkb, tkb)
        kb = kidx_ref[0, pl.ds(start, tkb), :]
        acc = jnp.zeros((tkb, tq), F32)
        for h in range(idx_heads):
            logits = lax.dot_general(kb, qi_ref[0, :, h * IDX_DIM:(h + 1) * IDX_DIM],
                                     (((1,), (1,)), ((), ())), preferred_element_type=F32)
            acc = acc + jnp.maximum(logits, 0.0) * wt_ref[0, h:h + 1, :]
        score_ref[pl.ds(start, tkb), :] = jnp.where(causal(start), acc, neg_inf)
        return c

    lax.fori_loop(0, n_blocks, score_block, 0)

    def count_where(pred):
        def body(j, cnt):
            start = pl.multiple_of(j * tkb, tkb)
            ind = jnp.where(pred(score_ref[pl.ds(start, tkb), :], start), 1.0, 0.0)
            part = jnp.sum(ind.reshape(tkb // (8 * SUBLANES), 8, SUBLANES, tq), axis=1)
            return cnt + jnp.sum(part, axis=0)
        cnt = lax.fori_loop(0, n_blocks, body, jnp.zeros((SUBLANES, tq), F32))
        return jnp.sum(cnt, axis=0, keepdims=True)

    def count_ge(thr):
        return count_where(lambda blk, start: blk >= thr)

    k_f = jnp.float32(top_k)
    few = qpos < top_k

    def search_cond(st):
        bi, _, cnt_u = st
        pending = jnp.where(few | (cnt_u == k_f), 0.0, 1.0)
        return (bi < 32) & (jnp.max(pending) > 0.0)

    def search_step(st):
        bi, u, cnt_u = st
        cand = u | (jnp.int32(1) << (31 - bi))
        cnt = count_ge(_threshold_of_key(cand))
        take = cnt >= k_f
        return bi + 1, jnp.where(take, cand, u), jnp.where(take, cnt, cnt_u)

    total = (n_blocks * tkb).astype(F32)
    _, u, cnt_u = lax.while_loop(search_cond, search_step,
                                 (jnp.int32(0), jnp.zeros((1, tq), jnp.int32), jnp.full((1, tq), total, F32)))
    thr = jnp.where(few, neg_inf, _threshold_of_key(u))

    def key_pos(start):
        return start + lax.broadcasted_iota(jnp.int32, (tkb, tq), 0)

    excess = jnp.logical_not(few) & (cnt_u > k_f)
    lim_ref[...] = jnp.full((1, tq), s_len, jnp.int32)

    @pl.when(jnp.max(jnp.where(excess, 1.0, 0.0)) > 0.0)
    def _():
        need = k_f - count_where(lambda blk, start: blk > thr)
        nbits = s_len.bit_length()

        def tie_step(bi, p):
            cand = p | (jnp.int32(1) << (nbits - 1 - bi))
            cnt = count_where(lambda blk, start: (blk == thr) & (key_pos(start) < cand))
            return jnp.where(cnt <= need, cand, p)

        p = lax.fori_loop(0, nbits, tie_step, jnp.zeros((1, tq), jnp.int32))
        lim_ref[...] = jnp.where(excess, p, s_len)

    lim = lim_ref[...]

    def write_block(j, c):
        start = pl.multiple_of(j * tkb, tkb)
        blk = score_ref[pl.ds(start, tkb), :]
        keep = ((blk > thr) | ((blk == thr) & (key_pos(start) < lim))) & causal(start)
        bias_ref[0, pl.ds(start, tkb), :] = jnp.where(keep, 0.0, MASK_VALUE).astype(bias_ref.dtype)
        return c

    lax.fori_loop(0, n_blocks, write_block, 0)

    def fill_block(j, c):
        start = pl.multiple_of(j * tkb, tkb)
        bias_ref[0, pl.ds(start, tkb), :] = jnp.full((tkb, tq), MASK_VALUE, bias_ref.dtype)
        return c

    lax.fori_loop(n_blocks, s_len // tkb, fill_block, 0)


IDX_TQ = 512


def dsa_indexer(qi, wt, kidx, idx_heads, top_k):
    b, s, _ = qi.shape
    tq = _tile(s, IDX_TQ)
    tkb = _tile(s, 512)
    assert tkb >= top_k or tkb == s
    return pl.pallas_call(
        functools.partial(_indexer_kernel, tq=tq, tkb=tkb, idx_heads=idx_heads, top_k=top_k),
        grid=(b, s // tq),
        in_specs=[pl.BlockSpec((1, tq, idx_heads * IDX_DIM), lambda bb, i: (bb, i, 0)),
                  pl.BlockSpec((1, LANES, tq), lambda bb, i: (bb, 0, i)),
                  pl.BlockSpec((1, s, IDX_DIM), lambda bb, i: (bb, 0, 0))],
        out_specs=pl.BlockSpec((1, s, tq), lambda bb, i: (bb, 0, i)),
        out_shape=jax.ShapeDtypeStruct((b, s, s), BF16),
        scratch_shapes=[pltpu.VMEM((s, tq), F32), pltpu.VMEM((1, tq), jnp.int32)],
        compiler_params=_params(("parallel", "parallel")),
        name="dsa_indexer",
    )(qi, wt, kidx)


def odd_mixer(x2, h, b, s, cos_p, sin_p, w_in, q_norm_g, w_q_up, w_idx_q, k_ln_g, k_ln_b, w_out):
    t, d = h.shape
    q_rank = q_norm_g.shape[0]
    heads = w_q_up.shape[1] // DSA_HEAD_DIM
    idx_heads = w_idx_q.shape[1] // IDX_DIM
    in_w = w_in.shape[1]
    z_w = -(-(in_w - idx_heads + LANES) // (2 * LANES)) * (2 * LANES)
    w_z = jnp.pad(w_in, ((0, 0), (0, z_w - in_w))).astype(BF16)
    z = matmul(h, w_z, F32)
    qn, k, vt, kidx, wt = odd_lat_prep(z.reshape(b, s, z_w), q_norm_g, k_ln_g, k_ln_b, cos_p, sin_p, idx_heads)
    tq = _tile(s, 256)
    qn2 = qn.reshape(t, q_rank)
    q = dsa_q_proj(qn2, w_q_up.astype(BF16), cos_p, sin_p, b, s, tq)
    qi = idx_q_proj(qn2, w_idx_q.astype(BF16), cos_p, sin_p).reshape(b, s, -1)
    top_k = min(IDX_TOPK_MAX, s // 4)
    bias = dsa_indexer(qi, wt, kidx, idx_heads, top_k)
    o = flash_attention(q, k, vt, bias, tq=tq, rep=heads // DSA_KV_HEADS, out_dtype=BF16)
    return matmul(o.reshape(t, -1), w_out.astype(BF16), F32, kind="residual", res=x2)


def sq_relu_mlp(x2, h, w_up, w_down, layer):
    a = matmul(h, w_up, BF16, kind="relu2", layer=layer)
    return matmul(a, w_down, F32, kind="residual", res=x2, layer=layer)


def kernel(x, positions, norm_mix_g, norm_mlp_g, final_norm_g, even_w_in, s5_lam_re, s5_lam_im, s5_log_step, s5_b_re, s5_b_im, s5_c_re, s5_c_im, s5_d, s5_glu_a, s5_glu_b, mla_q_norm_g, mla_w_q_up, mla_kv_norm_g, mla_w_kv_up, even_w_out, odd_w_in, dsa_q_norm_g, dsa_w_q_up, idx_w_q, idx_k_ln_g, idx_k_ln_b, odd_w_out, mlp_w_up, mlp_w_down):
    b, s, d = x.shape
    depth = norm_mix_g.shape[0]
    cos_m, sin_m = _rope_tables(positions, MLA_ROPE, LANES)
    cos_p, sin_p = _rope_tables(positions, ROT_DIM, ROT_DIM)
    x2 = x.reshape(b * s, d)
    for layer in range(depth):
        i = layer // 2
        h = rmsnorm(x2, norm_mix_g[layer], BF16)
        if layer % 2 == 0:
            x2 = even_mixer(x2, h, b, s, cos_m, sin_m, even_w_in[i], s5_lam_re[i], s5_lam_im[i],
                            s5_log_step[i], s5_b_re[i], s5_b_im[i], s5_c_re[i], s5_c_im[i], s5_d[i],
                            s5_glu_a[i], s5_glu_b[i], mla_q_norm_g[i], mla_w_q_up[i],
                            mla_kv_norm_g[i], mla_w_kv_up[i], even_w_out[i])
        else:
            x2 = odd_mixer(x2, h, b, s, cos_p, sin_p, odd_w_in[i], dsa_q_norm_g[i], dsa_w_q_up[i],
                           idx_w_q[i], idx_k_ln_g[i], idx_k_ln_b[i], odd_w_out[i])
        h = rmsnorm(x2, norm_mlp_g[layer], BF16)
        x2 = sq_relu_mlp(x2, h, mlp_w_up, mlp_w_down, layer)
    return rmsnorm(x2, final_norm_g, x.dtype).reshape(b, s, d)
```

```python
import functools
import math

import jax
import jax.numpy as jnp
from jax import lax
from jax.experimental import pallas as pl
from jax.experimental.pallas import tpu as pltpu

F32 = jnp.float32
BF16 = jnp.bfloat16

EPS = 1e-6
ROPE_THETA = 500000.0
S5_GROUP = 16
S5_LAMBDA_RE_MAX = -1e-4
MLA_NOPE = 128
MLA_ROPE = 64
MLA_V = 128
DSA_HEAD_DIM = 128
DSA_KV_HEADS = 4
IDX_DIM = 128
IDX_TOPK_MAX = 256
ROT_DIM = DSA_HEAD_DIM // 4

LANES = 128
SUBLANES = 8
VMEM_LIMIT_BYTES = 56 * 1024 * 1024
MASK_VALUE = -1e30


def _params(semantics):
    return pltpu.CompilerParams(dimension_semantics=semantics, vmem_limit_bytes=VMEM_LIMIT_BYTES)


def _tile(dim, pref, align=LANES):
    if dim <= pref:
        return dim
    t = (pref // align) * align
    while t >= align:
        if dim % t == 0:
            return t
        t -= align
    return dim


def _rmsnorm_kernel(x_ref, g_ref, o_ref):
    x = x_ref[...].astype(F32)
    r = lax.rsqrt(jnp.mean(x * x, axis=-1, keepdims=True) + EPS)
    o_ref[...] = (x * r * g_ref[...]).astype(o_ref.dtype)


def rmsnorm(x, g, out_dtype):
    m, d = x.shape
    tm = _tile(m, 256, SUBLANES)
    return pl.pallas_call(
        _rmsnorm_kernel,
        grid=(m // tm,),
        in_specs=[pl.BlockSpec((tm, d), lambda i: (i, 0)),
                  pl.BlockSpec((1, d), lambda i: (0, 0))],
        out_specs=pl.BlockSpec((tm, d), lambda i: (i, 0)),
        out_shape=jax.ShapeDtypeStruct((m, d), out_dtype),
        compiler_params=_params(("parallel",)),
        name="rmsnorm",
    )(x, g.reshape(1, d).astype(F32))


def _mm_epilogue(acc, kind, res_ref):
    if kind == "relu2":
        a = jnp.maximum(acc, 0.0)
        return a * a
    if kind == "residual":
        return res_ref[...].astype(F32) + acc
    return acc


def _mm_kernel_single(*refs, kind):
    if kind == "residual":
        a_ref, w_ref, res_ref, o_ref = refs
    else:
        a_ref, w_ref, o_ref = refs
        res_ref = None
    acc = jnp.dot(a_ref[...].astype(BF16), w_ref[...].astype(BF16), preferred_element_type=F32)
    o_ref[...] = _mm_epilogue(acc, kind, res_ref).astype(o_ref.dtype)


def _mm_kernel_ksplit(*refs, kind, nk):
    if kind == "residual":
        a_ref, w_ref, res_ref, o_ref, acc_ref = refs
    else:
        a_ref, w_ref, o_ref, acc_ref = refs
        res_ref = None
    k = pl.program_id(2)

    @pl.when(k == 0)
    def _():
        acc_ref[...] = jnp.zeros_like(acc_ref)

    acc_ref[...] += jnp.dot(a_ref[...].astype(BF16), w_ref[...].astype(BF16), preferred_element_type=F32)

    @pl.when(k == nk - 1)
    def _():
        o_ref[...] = _mm_epilogue(acc_ref[...], kind, res_ref).astype(o_ref.dtype)


def matmul(a, w, out_dtype, kind="none", res=None, layer=None):
    m, k = a.shape
    n = w.shape[-1]
    tm_pref, tn_pref, tk_pref = (1024, 512, k) if k <= 4096 else (1024, 1024, 2048)
    tm = _tile(m, tm_pref, SUBLANES)
    tn = _tile(n, tn_pref)
    tk = _tile(k, tk_pref)

    def w_spec(tk_, tn_, index):
        if layer is None:
            return pl.BlockSpec((tk_, tn_), index)
        return pl.BlockSpec((None, tk_, tn_), lambda *g: (layer,) + index(*g))

    if tk == k:
        in_specs = [pl.BlockSpec((tm, k), lambda i, j: (i, 0)),
                    w_spec(k, tn, lambda i, j: (0, j))]
        args = [a, w]
        if kind == "residual":
            in_specs.append(pl.BlockSpec((tm, tn), lambda i, j: (i, j)))
            args.append(res)
        return pl.pallas_call(
            functools.partial(_mm_kernel_single, kind=kind),
            grid=(m // tm, n // tn),
            in_specs=in_specs,
            out_specs=pl.BlockSpec((tm, tn), lambda i, j: (i, j)),
            out_shape=jax.ShapeDtypeStruct((m, n), out_dtype),
            compiler_params=_params(("parallel", "parallel")),
            name="matmul",
        )(*args)
    nk = k // tk
    in_specs = [pl.BlockSpec((tm, tk), lambda i, j, kk: (i, kk)),
                w_spec(tk, tn, lambda i, j, kk: (kk, j))]
    args = [a, w]
    if kind == "residual":
        in_specs.append(pl.BlockSpec((tm, tn), lambda i, j, kk: (i, j)))
        args.append(res)
    return pl.pallas_call(
        functools.partial(_mm_kernel_ksplit, kind=kind, nk=nk),
        grid=(m // tm, n // tn, nk),
        in_specs=in_specs,
        out_specs=pl.BlockSpec((tm, tn), lambda i, j, kk: (i, j)),
        out_shape=jax.ShapeDtypeStruct((m, n), out_dtype),
        scratch_shapes=[pltpu.VMEM((tm, tn), F32)],
        compiler_params=_params(("parallel", "parallel", "arbitrary")),
        name="matmul_ksplit",
    )(*args)


def _norm_mm_kernel(x_ref, g_ref, w_ref, o_ref, xn_ref):
    @pl.when(pl.program_id(1) == 0)
    def _():
        x = x_ref[...]
        r = lax.rsqrt(jnp.mean(x * x, axis=-1, keepdims=True) + EPS)
        xn_ref[...] = (x * r * g_ref[...]).astype(xn_ref.dtype)

    o_ref[...] = jnp.dot(xn_ref[...], w_ref[...], preferred_element_type=F32).astype(o_ref.dtype)


def norm_matmul(x, g, w, out_dtype):
    m, k = x.shape
    n = w.shape[1]
    tm = _tile(m, 512, SUBLANES)
    tn = _tile(n, 512)
    return pl.pallas_call(
        _norm_mm_kernel,
        grid=(m // tm, n // tn),
        in_specs=[pl.BlockSpec((tm, k), lambda i, j: (i, 0)),
                  pl.BlockSpec((1, k), lambda i, j: (0, 0)),
                  pl.BlockSpec((k, tn), lambda i, j: (0, j))],
        out_specs=pl.BlockSpec((tm, tn), lambda i, j: (i, j)),
        out_shape=jax.ShapeDtypeStruct((m, n), out_dtype),
        scratch_shapes=[pltpu.VMEM((tm, k), BF16)],
        compiler_params=_params(("parallel", "arbitrary")),
        name="norm_matmul",
    )(x, g.reshape(1, k).astype(F32), w)


def _glu_kernel(a_ref, wa_ref, wb_ref, o_ref):
    a = a_ref[...].astype(BF16)
    ya = jnp.dot(a, wa_ref[...], preferred_element_type=F32)
    yb = jnp.dot(a, wb_ref[...], preferred_element_type=F32)
    o_ref[...] = (ya * jax.nn.sigmoid(yb)).astype(o_ref.dtype)


def glu_matmul(a, wa, wb, out_dtype):
    m, k = a.shape
    _, n = wa.shape
    tm = _tile(m, 1024, SUBLANES)
    tn = _tile(n, 512)
    return pl.pallas_call(
        _glu_kernel,
        grid=(m // tm, n // tn),
        in_specs=[pl.BlockSpec((tm, k), lambda i, j: (i, 0)),
                  pl.BlockSpec((k, tn), lambda i, j: (0, j)),
                  pl.BlockSpec((k, tn), lambda i, j: (0, j))],
        out_specs=pl.BlockSpec((tm, tn), lambda i, j: (i, j)),
        out_shape=jax.ShapeDtypeStruct((m, n), out_dtype),
        compiler_params=_params(("parallel", "parallel")),
        name="glu_matmul",
    )(a, wa, wb)


def _rope_tile(x, cos_t, sin_t, half):
    lane = lax.broadcasted_iota(jnp.int32, x.shape, 1)
    first = (lane % (2 * half)) < half
    partner = jnp.where(first, pltpu.roll(x, LANES - half, 1), pltpu.roll(x, half, 1))
    return x * cos_t + partner * sin_t


def _rope_tables(positions, dim, pad_to):
    inv_freq = ROPE_THETA ** (-jnp.arange(0, dim, 2, dtype=F32) / dim)
    ang = positions.astype(F32).reshape(-1)[:, None] * inv_freq
    c, s = jnp.cos(ang), jnp.sin(ang)
    reps = pad_to // dim
    cos_t = jnp.tile(jnp.concatenate([c, c], axis=-1), (1, reps))
    sin_t = jnp.tile(jnp.concatenate([-s, s], axis=-1), (1, reps))
    t = c.shape[0]
    cos_t = jnp.concatenate([cos_t, jnp.ones((t, LANES - pad_to), F32)], axis=-1)
    sin_t = jnp.concatenate([sin_t, jnp.zeros((t, LANES - pad_to), F32)], axis=-1)
    return cos_t, sin_t


V_ONES_ROWS = 16
FLASH_TKB = 512
LOG2E = math.log2(math.e)


def _flash_kernel(*refs, tq, tkb, rep, dv, has_bias):
    if has_bias:
        q_ref, k_ref, vt_ref, b_ref, o_ref, m_ref, acc_ref, sa_ref, sb_ref = refs
    else:
        q_ref, k_ref, vt_ref, o_ref, m_ref, acc_ref, sa_ref, sb_ref = refs
        b_ref = None
    i = pl.program_id(2)
    rows = rep * tq

    m_ref[...] = jnp.full(m_ref.shape, MASK_VALUE, F32)
    acc_ref[...] = jnp.zeros(acc_ref.shape, F32)
    q = q_ref[0, 0, 0]

    def qk(j, dst_ref):
        start = pl.multiple_of(j * tkb, tkb)
        kb = k_ref[0, 0, pl.ds(start, tkb), :]
        dst_ref[...] = lax.dot_general(kb, q, (((1,), (1,)), ((), ())),
                                       preferred_element_type=F32)

    def softmax_pv(j, src_ref, masked):
        start = pl.multiple_of(j * tkb, tkb)
        vtb = vt_ref[0, 0, :, pl.ds(start, tkb)]
        s = src_ref[...]
        if has_bias:
            bias = b_ref[0, pl.ds(start, tkb), :].astype(F32)
            s = s + jnp.tile(bias, (1, rep))
        if masked:
            kpos = start + lax.broadcasted_iota(jnp.int32, (tkb, rows), 0)
            qlane = lax.broadcasted_iota(jnp.int32, (tkb, rows), 1)
            qpos = i * tq + (qlane % tq if rep > 1 else qlane)
            s = jnp.where(kpos <= qpos, s, MASK_VALUE)
        m_prev = m_ref[...]
        m_new = jnp.maximum(m_prev, jnp.max(s, axis=0, keepdims=True))
        p = jnp.exp2(s - m_new)
        alpha = jnp.exp2(m_prev - m_new)
        acc_ref[...] = alpha * acc_ref[...] + jnp.dot(vtb, p.astype(BF16), preferred_element_type=F32)
        m_ref[...] = m_new

    if has_bias:
        qk(0, sa_ref)
        n_blocks = ((i + 1) * tq + tkb - 1) // tkb

        def pair(jj, c):
            j0 = 2 * jj
            qk(j0 + 1, sb_ref)
            softmax_pv(j0, sa_ref, False)
            qk(jnp.minimum(j0 + 2, n_blocks - 1), sa_ref)
            softmax_pv(j0 + 1, sb_ref, False)
            return c

        lax.fori_loop(0, n_blocks // 2, pair, 0)

        @pl.when(n_blocks % 2 == 1)
        def _():
            softmax_pv(n_blocks - 1, sa_ref, False)
    else:
        qk(2 * i, sa_ref)
        qk(2 * i + 1, sb_ref)
        softmax_pv(2 * i, sa_ref, True)
        qk(0, sa_ref)
        softmax_pv(2 * i + 1, sb_ref, True)

        def pair(jj, c):
            j0 = 2 * jj
            qk(j0 + 1, sb_ref)
            softmax_pv(j0, sa_ref, False)
            qk(jnp.minimum(j0 + 2, 2 * i - 1), sa_ref)
            softmax_pv(j0 + 1, sb_ref, False)
            return c

        lax.fori_loop(0, i, pair, 0)

    acc = acc_ref[...]
    out = (acc[0:dv] * (1.0 / acc[dv:dv + 1])).T
    for r in range(rep):
        o_ref[0, :, r * dv:(r + 1) * dv] = out[r * tq:(r + 1) * tq].astype(o_ref.dtype)


def flash_attention(q, k, vt, bias, *, tq, rep, out_dtype):
    b, hk, nq, rows, dq = q.shape
    s = k.shape[2]
    dve = vt.shape[2]
    dv = dve - V_ONES_ROWS
    tkb = _tile(s, FLASH_TKB)
    has_bias = bias is not None
    assert has_bias or tq == 2 * tkb
    in_specs = [pl.BlockSpec((1, 1, 1, rows, dq), lambda bb, h, i: (bb, h, i, 0, 0)),
                pl.BlockSpec((1, 1, s, dq), lambda bb, h, i: (bb, h, 0, 0)),
                pl.BlockSpec((1, 1, dve, s), lambda bb, h, i: (bb, h, 0, 0))]
    args = [q, k, vt]
    if has_bias:
        in_specs.append(pl.BlockSpec((1, s, tq), lambda bb, h, i: (bb, 0, i)))
        args.append(bias)
    return pl.pallas_call(
        functools.partial(_flash_kernel, tq=tq, tkb=tkb, rep=rep, dv=dv, has_bias=has_bias),
        grid=(b, hk, nq),
        in_specs=in_specs,
        out_specs=pl.BlockSpec((1, tq, rep * dv), lambda bb, h, i: (bb, i, h)),
        out_shape=jax.ShapeDtypeStruct((b, s, hk * rep * dv), out_dtype),
        scratch_shapes=[pltpu.VMEM((1, rows), F32), pltpu.VMEM((dve, rows), F32),
                        pltpu.VMEM((tkb, rows), F32), pltpu.VMEM((tkb, rows), F32)],
        compiler_params=_params(("parallel", "parallel", "parallel")),
        name="flash_attention",
    )(*args)


S5_NSEG = 2 * SUBLANES
S5_CBLK = LANES
S5_LT = 16


def _s5_kernel(u_ref, bbd_ref, cbd_ref, a_ref, ap_ref, d_ref, y_ref,
               ug_ref, bu_ref, xs_ref, init_ref, *, seg_len, lt):
    nseg = S5_NSEG
    ns = a_ref.shape[-1]
    a_re = jnp.broadcast_to(a_ref[0, 0:1, :], (nseg, ns))
    a_im = jnp.broadcast_to(a_ref[0, 1:2, :], (nseg, ns))
    ntiles = seg_len // lt

    def load_inputs(t, slot):
        for i in range(lt):
            ug_ref[slot, i * nseg:(i + 1) * nseg, :] = u_ref[0, pl.ds(t * lt + i, nseg, stride=seg_len), :]
        bu_ref[slot] = jnp.dot(ug_ref[slot].astype(BF16), bbd_ref[0], preferred_element_type=F32)

    def scan_tile(carry, slot, store):
        x_re, x_im = carry
        for i in range(lt):
            r0 = i * nseg
            b_re = bu_ref[slot, r0:r0 + nseg, 0:ns]
            b_im = bu_ref[slot, r0:r0 + nseg, ns:2 * ns]
            n_re = a_re * x_re - a_im * x_im + b_re
            n_im = a_re * x_im + a_im * x_re + b_im
            x_re, x_im = n_re, n_im
            if store:
                xs_ref[slot, r0:r0 + nseg, 0:ns] = x_re
                xs_ref[slot, r0:r0 + nseg, ns:2 * ns] = x_im
        return x_re, x_im

    def emit(t, slot):
        y = jnp.dot(xs_ref[slot].astype(BF16), cbd_ref[0], preferred_element_type=F32)
        y = jax.nn.gelu(y + d_ref[0] * ug_ref[slot])
        for i in range(lt):
            y_ref[0, pl.ds(t * lt + i, nseg, stride=seg_len), :] = y[i * nseg:(i + 1) * nseg]

    def sweep(carry, store):
        load_inputs(0, 0)

        def pair(tt, c):
            t0 = 2 * tt
            load_inputs(t0 + 1, 1)
            c = scan_tile(c, 0, store)
            if store:
                emit(t0, 0)
            load_inputs(jnp.minimum(t0 + 2, ntiles - 1), 0)
            c = scan_tile(c, 1, store)
            if store:
                emit(t0 + 1, 1)
            return c

        return lax.fori_loop(0, ntiles // 2, pair, carry)

    zeros = jnp.zeros((nseg, ns), F32)
    e_re, e_im = sweep((zeros, zeros), False)

    ap_re = ap_ref[0, 0:1, :]
    ap_im = ap_ref[0, 1:2, :]
    t_re = jnp.zeros((1, ns), F32)
    t_im = jnp.zeros((1, ns), F32)
    init_ref[0:1, :] = jnp.zeros((1, 2 * ns), F32)
    for s in range(1, nseg):
        p_re, p_im = e_re[s - 1:s], e_im[s - 1:s]
        t_re, t_im = (ap_re * t_re - ap_im * t_im + p_re,
                      ap_re * t_im + ap_im * t_re + p_im)
        init_ref[s:s + 1, 0:ns] = t_re
        init_ref[s:s + 1, ns:2 * ns] = t_im

    sweep((init_ref[:, 0:ns], init_ref[:, ns:2 * ns]), True)


def _s5_discretise(lam_re, lam_im, log_step, b_re, b_im, c_re, c_im, seg_len):
    g, n = lam_re.shape
    p = S5_GROUP
    lr = jnp.minimum(lam_re.astype(F32), S5_LAMBDA_RE_MAX)
    li = lam_im.astype(F32)
    step = jnp.exp(log_step.astype(F32))[:, None]
    mag = jnp.exp(lr * step)
    ab_re = mag * jnp.cos(li * step)
    ab_im = mag * jnp.sin(li * step)
    den = lr * lr + li * li
    f_re = ((ab_re - 1.0) * lr + ab_im * li) / den
    f_im = (ab_im * lr - (ab_re - 1.0) * li) / den
    br = b_re.astype(F32)
    bi = b_im.astype(F32)
    bb_re = f_re[..., None] * br - f_im[..., None] * bi
    bb_im = f_re[..., None] * bi + f_im[..., None] * br
    pw_re, pw_im = jnp.ones_like(ab_re), jnp.zeros_like(ab_im)
    sq_re, sq_im = ab_re, ab_im
    e = seg_len
    while e:
        if e & 1:
            pw_re, pw_im = pw_re * sq_re - pw_im * sq_im, pw_re * sq_im + pw_im * sq_re
        sq_re, sq_im = sq_re * sq_re - sq_im * sq_im, 2.0 * sq_re * sq_im
        e >>= 1
    gb = S5_CBLK // p
    nb = g // gb
    eye = jnp.eye(gb, dtype=F32)

    def blockdiag_in(bb):
        x = bb.reshape(nb, gb, n, p)
        return jnp.einsum("bgnp,gh->bgphn", x, eye).reshape(nb, gb * p, gb * n)

    def blockdiag_out(c):
        x = c.astype(F32).reshape(nb, gb, p, n)
        return jnp.einsum("bgpn,gh->bgnhp", x, eye).reshape(nb, gb * n, gb * p)

    bbd = jnp.concatenate([blockdiag_in(bb_re), blockdiag_in(bb_im)], axis=2).astype(BF16)
    cbd = jnp.concatenate([blockdiag_out(c_re), -blockdiag_out(c_im)], axis=1).astype(BF16)
    a = jnp.stack([ab_re.reshape(nb, gb * n), ab_im.reshape(nb, gb * n)], axis=1)
    ap = jnp.stack([pw_re.reshape(nb, gb * n), pw_im.reshape(nb, gb * n)], axis=1)
    return bbd, cbd, a, ap


def s5_scan(u, lam_re, lam_im, log_step, b_re, b_im, c_re, c_im, d_skip, col0=0):
    b, s, _ = u.shape
    w = d_skip.shape[0]
    assert col0 % S5_CBLK == 0
    blk0 = col0 // S5_CBLK
    seg_len = s // S5_NSEG
    lt = min(S5_LT, seg_len)
    assert (seg_len // lt) % 2 == 0
    bbd, cbd, a, ap = _s5_discretise(lam_re, lam_im, log_step, b_re, b_im, c_re, c_im, seg_len)
    nb, cw, ns2 = bbd.shape
    ns = ns2 // 2
    rows = lt * S5_NSEG
    return pl.pallas_call(
        functools.partial(_s5_kernel, seg_len=seg_len, lt=lt),
        grid=(b, nb),
        in_specs=[pl.BlockSpec((1, s, cw), lambda bb, j: (bb, 0, blk0 + j)),
                  pl.BlockSpec((1, cw, ns2), lambda bb, j: (j, 0, 0)),
                  pl.BlockSpec((1, ns2, cw), lambda bb, j: (j, 0, 0)),
                  pl.BlockSpec((1, 2, ns), lambda bb, j: (j, 0, 0)),
                  pl.BlockSpec((1, 2, ns), lambda bb, j: (j, 0, 0)),
                  pl.BlockSpec((1, 1, cw), lambda bb, j: (j, 0, 0))],
        out_specs=pl.BlockSpec((1, s, cw), lambda bb, j: (bb, 0, j)),
        out_shape=jax.ShapeDtypeStruct((b, s, w), F32),
        scratch_shapes=[pltpu.VMEM((2, rows, cw), F32), pltpu.VMEM((2, rows, ns2), F32),
                        pltpu.VMEM((2, rows, ns2), F32), pltpu.VMEM((S5_NSEG, ns2), F32)],
        compiler_params=_params(("parallel", "parallel")),
        name="s5_scan",
    )(u, bbd, cbd, a, ap, d_skip.astype(F32).reshape(nb, 1, cw))


def _even_lat_kernel(lat_ref, gq_ref, gkv_ref, cos_ref, sin_ref, qn_ref, kvn_ref, kr_ref, *, q_rank, kv_rank):
    lat = lat_ref[...]
    q_lat = lat[:, 0:q_rank]
    kv_lat = lat[:, q_rank:q_rank + kv_rank]
    kr = lat[:, q_rank + kv_rank:q_rank + kv_rank + LANES]
    rq = lax.rsqrt(jnp.mean(q_lat * q_lat, axis=-1, keepdims=True) + EPS)
    qn_ref[...] = (q_lat * rq * gq_ref[...]).astype(qn_ref.dtype)
    rkv = lax.rsqrt(jnp.mean(kv_lat * kv_lat, axis=-1, keepdims=True) + EPS)
    kvn_ref[...] = (kv_lat * rkv * gkv_ref[...]).astype(kvn_ref.dtype)
    kr = _rope_tile(kr, cos_ref[...], sin_ref[...], MLA_ROPE // 2)
    lane = lax.broadcasted_iota(jnp.int32, kr.shape, 1)
    lo = jnp.where(lane < MLA_ROPE, kr, 0.0)
    hi = pltpu.roll(lo, MLA_ROPE, 1)
    kr_ref[:, 0:LANES] = lo.astype(kr_ref.dtype)
    kr_ref[:, LANES:2 * LANES] = hi.astype(kr_ref.dtype)


def even_lat_prep(lat, wl, gq, gkv, cos_t, sin_t):
    m = lat.shape[0]
    q_rank, kv_rank = gq.shape[0], gkv.shape[0]
    tm = _tile(m, 512, SUBLANES)
    return pl.pallas_call(
        functools.partial(_even_lat_kernel, q_rank=q_rank, kv_rank=kv_rank),
        grid=(m // tm,),
        in_specs=[pl.BlockSpec((tm, wl), lambda i: (i, 0)),
                  pl.BlockSpec((1, q_rank), lambda i: (0, 0)),
                  pl.BlockSpec((1, kv_rank), lambda i: (0, 0)),
                  pl.BlockSpec((tm, LANES), lambda i: (i, 0)),
                  pl.BlockSpec((tm, LANES), lambda i: (i, 0))],
        out_specs=[pl.BlockSpec((tm, q_rank), lambda i: (i, 0)),
                   pl.BlockSpec((tm, kv_rank), lambda i: (i, 0)),
                   pl.BlockSpec((tm, 2 * LANES), lambda i: (i, 0))],
        out_shape=[jax.ShapeDtypeStruct((m, q_rank), BF16),
                   jax.ShapeDtypeStruct((m, kv_rank), BF16),
                   jax.ShapeDtypeStruct((m, 2 * LANES), BF16)],
        compiler_params=_params(("parallel",)),
        name="even_lat_prep",
    )(lat, gq.reshape(1, -1).astype(F32), gkv.reshape(1, -1).astype(F32), cos_t, sin_t)


def _mla_q_kernel(qn_ref, w_ref, cos_ref, sin_ref, qf_ref, *, heads, scale):
    q = jnp.dot(qn_ref[...], w_ref[...], preferred_element_type=F32)
    cos_t, sin_t = cos_ref[...], sin_ref[...]
    nope_w = heads * MLA_NOPE
    lane = lax.broadcasted_iota(jnp.int32, (q.shape[0], LANES), 1)
    for pair in range(heads // 2):
        rope = _rope_tile(q[:, nope_w + pair * LANES:nope_w + (pair + 1) * LANES], cos_t, sin_t, MLA_ROPE // 2)
        for sub in range(2):
            h = 2 * pair + sub
            keep = (lane < MLA_ROPE) if sub == 0 else (lane >= MLA_ROPE)
            qf_ref[0, h, :, 0:LANES] = (q[:, h * MLA_NOPE:(h + 1) * MLA_NOPE] * scale).astype(qf_ref.dtype)
            qf_ref[0, h, :, LANES:2 * LANES] = (jnp.where(keep, rope, 0.0) * scale).astype(qf_ref.dtype)


def _mla_kv_kernel(kvn_ref, w_ref, kr_ref, kf_ref, vt_ref, *, heads):
    kv = jnp.dot(kvn_ref[...], w_ref[...], preferred_element_type=F32)
    for h in range(heads):
        c0 = h * (MLA_NOPE + MLA_V)
        kf_ref[0, h, :, 0:LANES] = kv[:, c0:c0 + MLA_NOPE].astype(kf_ref.dtype)
        kf_ref[0, h, :, LANES:2 * LANES] = kr_ref[:, (h % 2) * LANES:(h % 2 + 1) * LANES]
        vt_ref[0, h, 0:MLA_V, :] = kv[:, c0 + MLA_NOPE:c0 + MLA_NOPE + MLA_V].T.astype(vt_ref.dtype)
        vt_ref[0, h, MLA_V:MLA_V + V_ONES_ROWS, :] = jnp.ones((V_ONES_ROWS, kv.shape[0]), vt_ref.dtype)


def mla_projections(qn, kvn, kr2, wq, wkv, cos_t, sin_t, b, s, heads):
    t = qn.shape[0]
    tm = _tile(s, 256, SUBLANES)
    nt = s // tm
    scale = (MLA_NOPE + MLA_ROPE) ** -0.5 * LOG2E
    head_spec = pl.BlockSpec((1, heads, tm, 2 * LANES), lambda i: (i // nt, 0, i % nt, 0))
    qf = pl.pallas_call(
        functools.partial(_mla_q_kernel, heads=heads, scale=scale),
        grid=(t // tm,),
        in_specs=[pl.BlockSpec((tm, qn.shape[1]), lambda i: (i, 0)),
                  pl.BlockSpec(wq.shape, lambda i: (0, 0)),
                  pl.BlockSpec((tm, LANES), lambda i: (i, 0)),
                  pl.BlockSpec((tm, LANES), lambda i: (i, 0))],
        out_specs=head_spec,
        out_shape=jax.ShapeDtypeStruct((b, heads, s, 2 * LANES), BF16),
        compiler_params=_params(("parallel",)),
        name="mla_q_proj",
    )(qn, wq, cos_t, sin_t)
    kf, vt = pl.pallas_call(
        functools.partial(_mla_kv_kernel, heads=heads),
        grid=(t // tm,),
        in_specs=[pl.BlockSpec((tm, kvn.shape[1]), lambda i: (i, 0)),
                  pl.BlockSpec(wkv.shape, lambda i: (0, 0)),
                  pl.BlockSpec((tm, 2 * LANES), lambda i: (i, 0))],
        out_specs=[head_spec,
                   pl.BlockSpec((1, heads, MLA_V + V_ONES_ROWS, tm), lambda i: (i // nt, 0, 0, i % nt))],
        out_shape=[jax.ShapeDtypeStruct((b, heads, s, 2 * LANES), BF16),
                   jax.ShapeDtypeStruct((b, heads, MLA_V + V_ONES_ROWS, s), BF16)],
        compiler_params=_params(("parallel",)),
        name="mla_kv_proj",
    )(kvn, wkv, kr2)
    return qf, kf, vt


def even_mixer(x2, norm_g, b, s, cos_m, sin_m, w_in, lam_re, lam_im, log_step, b_re, b_im, c_re, c_im,
               d_skip, glu_a, glu_b, q_norm_g, w_q_up, kv_norm_g, w_kv_up, w_out):
    t, d = x2.shape
    s5_w = d_skip.shape[0]
    q_rank, kv_rank = q_norm_g.shape[0], kv_norm_g.shape[0]
    heads = w_kv_up.shape[1] // (MLA_NOPE + MLA_V)
    lat_w = q_rank + kv_rank + MLA_ROPE
    lat_pad = -(-(q_rank + kv_rank + LANES) // (2 * LANES)) * (2 * LANES)

    w_z = jnp.concatenate([jnp.pad(w_in[:, s5_w:], ((0, 0), (0, lat_pad - lat_w))), w_in[:, :s5_w]],
                          axis=1).astype(BF16)
    z = norm_matmul(x2, norm_g, w_z, F32)

    y = s5_scan(z.reshape(b, s, -1), lam_re, lam_im, log_step, b_re, b_im, c_re, c_im, d_skip, col0=lat_pad)
    s5_out = glu_matmul(y.reshape(t, s5_w), glu_a.astype(BF16), glu_b.astype(BF16), BF16)

    qn, kvn, kr2 = even_lat_prep(z, lat_pad, q_norm_g, kv_norm_g, cos_m, sin_m)
    wq = w_q_up.reshape(q_rank, heads, MLA_NOPE + MLA_ROPE)
    wq = jnp.concatenate([wq[:, :, :MLA_NOPE].reshape(q_rank, heads * MLA_NOPE),
                          wq[:, :, MLA_NOPE:].reshape(q_rank, heads * MLA_ROPE)], axis=1).astype(BF16)
    qf, kf, vf = mla_projections(qn, kvn, kr2, wq, w_kv_up.astype(BF16), cos_m, sin_m, b, s, heads)
    tq = 2 * _tile(s, FLASH_TKB)
    mla_out = flash_attention(qf.reshape(b, heads, s // tq, tq, 2 * LANES), kf, vf, None,
                              tq=tq, rep=1, out_dtype=BF16)
    mixed = jnp.concatenate([s5_out, mla_out.reshape(t, -1)], axis=-1)
    return matmul(mixed, w_out.astype(BF16), F32, kind="residual", res=x2)


def _odd_lat_kernel(z_ref, gq_ref, lng_ref, lnb_ref, cos_ref, sin_ref,
                    qn_ref, k_ref, v_ref, kidx_ref, w_ref, *, q_rank, w_scale):
    cos_t, sin_t = cos_ref[...], sin_ref[...]
    half = ROT_DIM // 2
    z = z_ref[0]
    q_lat = z[:, 0:q_rank]
    rq = lax.rsqrt(jnp.mean(q_lat * q_lat, axis=-1, keepdims=True) + EPS)
    qn_ref[0] = (q_lat * rq * gq_ref[...]).astype(qn_ref.dtype)
    kvh = DSA_KV_HEADS
    d = DSA_HEAD_DIM
    for g in range(kvh):
        kh = z[:, q_rank + g * d:q_rank + (g + 1) * d]
        k_ref[0, g] = _rope_tile(kh, cos_t, sin_t, half).astype(k_ref.dtype)
        v_ref[0, g, 0:d, :] = z[:, q_rank + (kvh + g) * d:q_rank + (kvh + g + 1) * d].T.astype(v_ref.dtype)
        v_ref[0, g, d:d + V_ONES_ROWS, :] = jnp.ones((V_ONES_ROWS, z.shape[0]), v_ref.dtype)
    off = q_rank + 2 * kvh * d
    ki = z[:, off:off + IDX_DIM]
    kc = ki - jnp.mean(ki, axis=-1, keepdims=True)
    var = jnp.mean(kc * kc, axis=-1, keepdims=True)
    ki = kc * lax.rsqrt(var + EPS) * lng_ref[...] + lnb_ref[...]
    kidx_ref[0] = _rope_tile(ki, cos_t, sin_t, half).astype(kidx_ref.dtype)
    w_ref[0] = (z[:, off + IDX_DIM:off + IDX_DIM + LANES] * w_scale).T


def odd_lat_prep(z, gq, ln_g, ln_b, cos_t, sin_t, idx_heads):
    b, s, zw = z.shape
    q_rank = gq.shape[0]
    tm = _tile(s, 512, SUBLANES)
    nt = s // tm
    w_scale = idx_heads ** -0.5 * IDX_DIM ** -0.5
    kvh, d = DSA_KV_HEADS, DSA_HEAD_DIM
    return pl.pallas_call(
        functools.partial(_odd_lat_kernel, q_rank=q_rank, w_scale=w_scale),
        grid=(b, nt),
        in_specs=[pl.BlockSpec((1, tm, zw), lambda bb, i: (bb, i, 0)),
                  pl.BlockSpec((1, q_rank), lambda bb, i: (0, 0)),
                  pl.BlockSpec((1, IDX_DIM), lambda bb, i: (0, 0)),
                  pl.BlockSpec((1, IDX_DIM), lambda bb, i: (0, 0)),
                  pl.BlockSpec((tm, LANES), lambda bb, i: (bb * nt + i, 0)),
                  pl.BlockSpec((tm, LANES), lambda bb, i: (bb * nt + i, 0))],
        out_specs=[pl.BlockSpec((1, tm, q_rank), lambda bb, i: (bb, i, 0)),
                   pl.BlockSpec((1, kvh, tm, d), lambda bb, i: (bb, 0, i, 0)),
                   pl.BlockSpec((1, kvh, d + V_ONES_ROWS, tm), lambda bb, i: (bb, 0, 0, i)),
                   pl.BlockSpec((1, tm, IDX_DIM), lambda bb, i: (bb, i, 0)),
                   pl.BlockSpec((1, LANES, tm), lambda bb, i: (bb, 0, i))],
        out_shape=[jax.ShapeDtypeStruct((b, s, q_rank), BF16),
                   jax.ShapeDtypeStruct((b, kvh, s, d), BF16),
                   jax.ShapeDtypeStruct((b, kvh, d + V_ONES_ROWS, s), BF16),
                   jax.ShapeDtypeStruct((b, s, IDX_DIM), BF16),
                   jax.ShapeDtypeStruct((b, LANES, s), F32)],
        compiler_params=_params(("parallel", "parallel")),
        name="odd_lat_prep",
    )(z, gq.reshape(1, -1).astype(F32), ln_g.reshape(1, -1).astype(F32),
      ln_b.reshape(1, -1).astype(F32), cos_t, sin_t)


def _mm_rope_kernel(a_ref, w_ref, cos_ref, sin_ref, o_ref, *, n_heads, scale, stack_rows):
    acc = jnp.dot(a_ref[...], w_ref[...], preferred_element_type=F32)
    cos_t, sin_t = cos_ref[...], sin_ref[...]
    d = DSA_HEAD_DIM
    for r in range(n_heads):
        xh = _rope_tile(acc[:, r * d:(r + 1) * d], cos_t, sin_t, ROT_DIM // 2) * scale
        if stack_rows:
            o_ref[0, 0, 0, r * stack_rows:(r + 1) * stack_rows, :] = xh.astype(o_ref.dtype)
        else:
            o_ref[:, r * d:(r + 1) * d] = xh.astype(o_ref.dtype)


def dsa_q_proj(qn, w_q, cos_t, sin_t, b, s, tq):
    t, kq = qn.shape
    d = DSA_HEAD_DIM
    heads = w_q.shape[1] // d
    rep = heads // DSA_KV_HEADS
    nt = s // tq
    return pl.pallas_call(
        functools.partial(_mm_rope_kernel, n_heads=rep, scale=d ** -0.5 * LOG2E, stack_rows=tq),
        grid=(t // tq, DSA_KV_HEADS),
        in_specs=[pl.BlockSpec((tq, kq), lambda i, g: (i, 0)),
                  pl.BlockSpec((kq, rep * d), lambda i, g: (0, g)),
                  pl.BlockSpec((tq, LANES), lambda i, g: (i, 0)),
                  pl.BlockSpec((tq, LANES), lambda i, g: (i, 0))],
        out_specs=pl.BlockSpec((1, 1, 1, rep * tq, d), lambda i, g: (i // nt, g, i % nt, 0, 0)),
        out_shape=jax.ShapeDtypeStruct((b, DSA_KV_HEADS, nt, rep * tq, d), BF16),
        compiler_params=_params(("parallel", "parallel")),
        name="dsa_q_proj",
    )(qn, w_q, cos_t, sin_t)


def idx_q_proj(qn, w_qi, cos_t, sin_t):
    t, kq = qn.shape
    n = w_qi.shape[1]
    tm = _tile(t, 1024, SUBLANES)
    tn = _tile(n, 512)
    return pl.pallas_call(
        functools.partial(_mm_rope_kernel, n_heads=tn // IDX_DIM, scale=1.0, stack_rows=0),
        grid=(t // tm, n // tn),
        in_specs=[pl.BlockSpec((tm, kq), lambda i, j: (i, 0)),
                  pl.BlockSpec((kq, tn), lambda i, j: (0, j)),
                  pl.BlockSpec((tm, LANES), lambda i, j: (i, 0)),
                  pl.BlockSpec((tm, LANES), lambda i, j: (i, 0))],
        out_specs=pl.BlockSpec((tm, tn), lambda i, j: (i, j)),
        out_shape=jax.ShapeDtypeStruct((t, n), BF16),
        compiler_params=_params(("parallel", "parallel")),
        name="idx_q_proj",
    )(qn, w_qi, cos_t, sin_t)


def _ukey_to_float(u):
    bits = jnp.where(u < 0, u & jnp.int32(0x7FFFFFFF), ~u)
    return lax.bitcast_convert_type(bits, F32)


def _threshold_of_key(u):
    thr = _ukey_to_float(u)
    return jnp.where((thr != thr) & (u >= 0), jnp.float32(-jnp.inf), thr)


def _indexer_kernel(qi_ref, wt_ref, kidx_ref, bias_ref, score_ref, lim_ref, *, tq, tkb, idx_heads, top_k):
    i = pl.program_id(1)
    s_len = kidx_ref.shape[1]
    n_blocks = ((i + 1) * tq + tkb - 1) // tkb
    neg_inf = jnp.float32(-jnp.inf)
    qpos = i * tq + lax.broadcasted_iota(jnp.int32, (1, tq), 1)

    def causal(start):
        kpos = start + lax.broadcasted_iota(jnp.int32, (tkb, tq), 0)
        return kpos <= qpos

    def score_block(j, c):
        start = pl.multiple_of(j * tkb, tkb)
        kb = kidx_ref[0, pl.ds(start, tkb), :]
        acc = jnp.zeros((tkb, tq), F32)
        for h in range(idx_heads):
            logits = lax.dot_general(kb, qi_ref[0, :, h * IDX_DIM:(h + 1) * IDX_DIM],
                                     (((1,), (1,)), ((), ())), preferred_element_type=F32)
            acc = acc + jnp.maximum(logits, 0.0) * wt_ref[0, h:h + 1, :]
        score_ref[pl.ds(start, tkb), :] = jnp.where(causal(start), acc, neg_inf)
        return c

    lax.fori_loop(0, n_blocks, score_block, 0)

    def count_where(pred):
        def body(j, cnt):
            start = pl.multiple_of(j * tkb, tkb)
            ind = jnp.where(pred(score_ref[pl.ds(start, tkb), :], start), 1.0, 0.0)
            part = jnp.sum(ind.reshape(tkb // (8 * SUBLANES), 8, SUBLANES, tq), axis=1)
            return cnt + jnp.sum(part, axis=0)
        cnt = lax.fori_loop(0, n_blocks, body, jnp.zeros((SUBLANES, tq), F32))
        return jnp.sum(cnt, axis=0, keepdims=True)

    def count_ge(thr):
        return count_where(lambda blk, start: blk >= thr)

    k_f = jnp.float32(top_k)
    few = qpos < top_k

    def search_cond(st):
        bi, _, cnt_u = st
        pending = jnp.where(few | (cnt_u == k_f), 0.0, 1.0)
        return (bi < 32) & (jnp.max(pending) > 0.0)

    def search_step(st):
        bi, u, cnt_u = st
        cand = u | (jnp.int32(1) << (31 - bi))
        cnt = count_ge(_threshold_of_key(cand))
        take = cnt >= k_f
        return bi + 1, jnp.where(take, cand, u), jnp.where(take, cnt, cnt_u)

    total = (n_blocks * tkb).astype(F32)
    _, u, cnt_u = lax.while_loop(search_cond, search_step,
                                 (jnp.int32(0), jnp.zeros((1, tq), jnp.int32), jnp.full((1, tq), total, F32)))
    thr = jnp.where(few, neg_inf, _threshold_of_key(u))

    def key_pos(start):
        return start + lax.broadcasted_iota(jnp.int32, (tkb, tq), 0)

    excess = jnp.logical_not(few) & (cnt_u > k_f)
    lim_ref[...] = jnp.full((1, tq), s_len, jnp.int32)

    @pl.when(jnp.max(jnp.where(excess, 1.0, 0.0)) > 0.0)
    def _():
        need = k_f - count_where(lambda blk, start: blk > thr)
        nbits = s_len.bit_length()

        def tie_step(bi, p):
            cand = p | (jnp.int32(1) << (nbits - 1 - bi))
            cnt = count_where(lambda blk, start: (blk == thr) & (key_pos(start) < cand))
            return jnp.where(cnt <= need, cand, p)

        p = lax.fori_loop(0, nbits, tie_step, jnp.zeros((1, tq), jnp.int32))
        lim_ref[...] = jnp.where(excess, p, s_len)

    lim = lim_ref[...]

    def write_block(j, c):
        start = pl.multiple_of(j * tkb, tkb)
        blk = score_ref[pl.ds(start, tkb), :]
        keep = ((blk > thr) | ((blk == thr) & (key_pos(start) < lim))) & causal(start)
        bias_ref[0, pl.ds(start, tkb), :] = jnp.where(keep, 0.0, MASK_VALUE).astype(bias_ref.dtype)
        return c

    lax.fori_loop(0, n_blocks, write_block, 0)

    def fill_block(j, c):
        start = pl.multiple_of(j * tkb, tkb)
        bias_ref[0, pl.ds(start, tkb), :] = jnp.full((tkb, tq), MASK_VALUE, bias_ref.dtype)
        return c

    lax.fori_loop(n_blocks, s_len // tkb, fill_block, 0)


IDX_TQ = 512


def dsa_indexer(qi, wt, kidx, idx_heads, top_k):
    b, s, _ = qi.shape
    tq = _tile(s, IDX_TQ)
    tkb = _tile(s, 512)
    assert tkb >= top_k or tkb == s
    return pl.pallas_call(
        functools.partial(_indexer_kernel, tq=tq, tkb=tkb, idx_heads=idx_heads, top_k=top_k),
        grid=(b, s // tq),
        in_specs=[pl.BlockSpec((1, tq, idx_heads * IDX_DIM), lambda bb, i: (bb, i, 0)),
                  pl.BlockSpec((1, LANES, tq), lambda bb, i: (bb, 0, i)),
                  pl.BlockSpec((1, s, IDX_DIM), lambda bb, i: (bb, 0, 0))],
        out_specs=pl.BlockSpec((1, s, tq), lambda bb, i: (bb, 0, i)),
        out_shape=jax.ShapeDtypeStruct((b, s, s), BF16),
        scratch_shapes=[pltpu.VMEM((s, tq), F32), pltpu.VMEM((1, tq), jnp.int32)],
        compiler_params=_params(("parallel", "parallel")),
        name="dsa_indexer",
    )(qi, wt, kidx)


def odd_mixer(x2, norm_g, b, s, cos_p, sin_p, w_in, q_norm_g, w_q_up, w_idx_q, k_ln_g, k_ln_b, w_out):
    t, d = x2.shape
    q_rank = q_norm_g.shape[0]
    heads = w_q_up.shape[1] // DSA_HEAD_DIM
    idx_heads = w_idx_q.shape[1] // IDX_DIM
    in_w = w_in.shape[1]
    z_w = -(-(in_w - idx_heads + LANES) // (2 * LANES)) * (2 * LANES)
    w_z = jnp.pad(w_in, ((0, 0), (0, z_w - in_w))).astype(BF16)
    z = norm_matmul(x2, norm_g, w_z, F32)
    qn, k, vt, kidx, wt = odd_lat_prep(z.reshape(b, s, z_w), q_norm_g, k_ln_g, k_ln_b, cos_p, sin_p, idx_heads)
    tq = _tile(s, 256)
    qn2 = qn.reshape(t, q_rank)
    q = dsa_q_proj(qn2, w_q_up.astype(BF16), cos_p, sin_p, b, s, tq)
    qi = idx_q_proj(qn2, w_idx_q.astype(BF16), cos_p, sin_p).reshape(b, s, -1)
    top_k = min(IDX_TOPK_MAX, s // 4)
    bias = dsa_indexer(qi, wt, kidx, idx_heads, top_k)
    o = flash_attention(q, k, vt, bias, tq=tq, rep=heads // DSA_KV_HEADS, out_dtype=BF16)
    return matmul(o.reshape(t, -1), w_out.astype(BF16), F32, kind="residual", res=x2)


def sq_relu_mlp(x2, h, w_up, w_down, layer):
    a = matmul(h, w_up, BF16, kind="relu2", layer=layer)
    return matmul(a, w_down, F32, kind="residual", res=x2, layer=layer)


def kernel(x, positions, norm_mix_g, norm_mlp_g, final_norm_g, even_w_in, s5_lam_re, s5_lam_im, s5_log_step, s5_b_re, s5_b_im, s5_c_re, s5_c_im, s5_d, s5_glu_a, s5_glu_b, mla_q_norm_g, mla_w_q_up, mla_kv_norm_g, mla_w_kv_up, even_w_out, odd_w_in, dsa_q_norm_g, dsa_w_q_up, idx_w_q, idx_k_ln_g, idx_k_ln_b, odd_w_out, mlp_w_up, mlp_w_down):
    b, s, d = x.shape
    depth = norm_mix_g.shape[0]
    cos_m, sin_m = _rope_tables(positions, MLA_ROPE, LANES)
    cos_p, sin_p = _rope_tables(positions, ROT_DIM, ROT_DIM)
    x2 = x.reshape(b * s, d)
    for layer in range(depth):
        i = layer // 2
        if layer % 2 == 0:
            x2 = even_mixer(x2, norm_mix_g[layer], b, s, cos_m, sin_m, even_w_in[i], s5_lam_re[i], s5_lam_im[i],
                            s5_log_step[i], s5_b_re[i], s5_b_im[i], s5_c_re[i], s5_c_im[i], s5_d[i],
                            s5_glu_a[i], s5_glu_b[i], mla_q_norm_g[i], mla_w_q_up[i],
                            mla_kv_norm_g[i], mla_w_kv_up[i], even_w_out[i])
        else:
            x2 = odd_mixer(x2, norm_mix_g[layer], b, s, cos_p, sin_p, odd_w_in[i], dsa_q_norm_g[i], dsa_w_q_up[i],
                           idx_w_q[i], idx_k_ln_g[i], idx_k_ln_b[i], odd_w_out[i])
        h = rmsnorm(x2, norm_mlp_g[layer], BF16)
        x2 = sq_relu_mlp(x2, h, mlp_w_up, mlp_w_down, layer)
    return rmsnorm(x2, final_norm_g, x.dtype).reshape(b, s, d)
```

```python
import functools
import math

import jax
import jax.numpy as jnp
from jax import lax
from jax.experimental import pallas as pl
from jax.experimental.pallas import tpu as pltpu

F32 = jnp.float32
BF16 = jnp.bfloat16

EPS = 1e-6
ROPE_THETA = 500000.0
S5_GROUP = 16
S5_LAMBDA_RE_MAX = -1e-4
MLA_NOPE = 128
MLA_ROPE = 64
MLA_V = 128
DSA_HEAD_DIM = 128
DSA_KV_HEADS = 4
IDX_DIM = 128
IDX_TOPK_MAX = 256
ROT_DIM = DSA_HEAD_DIM // 4

LANES = 128
SUBLANES = 8
VMEM_LIMIT_BYTES = 56 * 1024 * 1024
MASK_VALUE = -1e30


def _params(semantics):
    return pltpu.CompilerParams(dimension_semantics=semantics, vmem_limit_bytes=VMEM_LIMIT_BYTES)


def _tile(dim, pref, align=LANES):
    if dim <= pref:
        return dim
    t = (pref // align) * align
    while t >= align:
        if dim % t == 0:
            return t
        t -= align
    return dim


def _rmsnorm_kernel(x_ref, g_ref, o_ref):
    x = x_ref[...].astype(F32)
    r = lax.rsqrt(jnp.mean(x * x, axis=-1, keepdims=True) + EPS)
    o_ref[...] = (x * r * g_ref[...]).astype(o_ref.dtype)


def rmsnorm(x, g, out_dtype):
    m, d = x.shape
    tm = _tile(m, 256, SUBLANES)
    return pl.pallas_call(
        _rmsnorm_kernel,
        grid=(m // tm,),
        in_specs=[pl.BlockSpec((tm, d), lambda i: (i, 0)),
                  pl.BlockSpec((1, d), lambda i: (0, 0))],
        out_specs=pl.BlockSpec((tm, d), lambda i: (i, 0)),
        out_shape=jax.ShapeDtypeStruct((m, d), out_dtype),
        compiler_params=_params(("parallel",)),
        name="rmsnorm",
    )(x, g.reshape(1, d).astype(F32))


def _mm_epilogue(acc, kind, res_ref):
    if kind == "relu2":
        a = jnp.maximum(acc, 0.0)
        return a * a
    if kind == "residual":
        return res_ref[...].astype(F32) + acc
    return acc


def _mm_kernel_single(*refs, kind):
    if kind == "residual":
        a_ref, w_ref, res_ref, o_ref = refs
    else:
        a_ref, w_ref, o_ref = refs
        res_ref = None
    acc = jnp.dot(a_ref[...].astype(BF16), w_ref[...].astype(BF16), preferred_element_type=F32)
    o_ref[...] = _mm_epilogue(acc, kind, res_ref).astype(o_ref.dtype)


def _mm_kernel_ksplit(*refs, kind, nk):
    if kind == "residual":
        a_ref, w_ref, res_ref, o_ref, acc_ref = refs
    else:
        a_ref, w_ref, o_ref, acc_ref = refs
        res_ref = None
    k = pl.program_id(2)

    @pl.when(k == 0)
    def _():
        acc_ref[...] = jnp.zeros_like(acc_ref)

    acc_ref[...] += jnp.dot(a_ref[...].astype(BF16), w_ref[...].astype(BF16), preferred_element_type=F32)

    @pl.when(k == nk - 1)
    def _():
        o_ref[...] = _mm_epilogue(acc_ref[...], kind, res_ref).astype(o_ref.dtype)


def matmul(a, w, out_dtype, kind="none", res=None, layer=None):
    m, k = a.shape
    n = w.shape[-1]
    tm_pref, tn_pref, tk_pref = (1024, 512, k) if k <= 4096 else (1024, 1024, 2048)
    tm = _tile(m, tm_pref, SUBLANES)
    tn = _tile(n, tn_pref)
    tk = _tile(k, tk_pref)

    def w_spec(tk_, tn_, index):
        if layer is None:
            return pl.BlockSpec((tk_, tn_), index)
        return pl.BlockSpec((None, tk_, tn_), lambda *g: (layer,) + index(*g))

    if tk == k:
        in_specs = [pl.BlockSpec((tm, k), lambda i, j: (i, 0)),
                    w_spec(k, tn, lambda i, j: (0, j))]
        args = [a, w]
        if kind == "residual":
            in_specs.append(pl.BlockSpec((tm, tn), lambda i, j: (i, j)))
            args.append(res)
        return pl.pallas_call(
            functools.partial(_mm_kernel_single, kind=kind),
            grid=(m // tm, n // tn),
            in_specs=in_specs,
            out_specs=pl.BlockSpec((tm, tn), lambda i, j: (i, j)),
            out_shape=jax.ShapeDtypeStruct((m, n), out_dtype),
            compiler_params=_params(("parallel", "parallel")),
            name="matmul",
        )(*args)
    nk = k // tk
    in_specs = [pl.BlockSpec((tm, tk), lambda i, j, kk: (i, kk)),
                w_spec(tk, tn, lambda i, j, kk: (kk, j))]
    args = [a, w]
    if kind == "residual":
        in_specs.append(pl.BlockSpec((tm, tn), lambda i, j, kk: (i, j)))
        args.append(res)
    return pl.pallas_call(
        functools.partial(_mm_kernel_ksplit, kind=kind, nk=nk),
        grid=(m // tm, n // tn, nk),
        in_specs=in_specs,
        out_specs=pl.BlockSpec((tm, tn), lambda i, j, kk: (i, j)),
        out_shape=jax.ShapeDtypeStruct((m, n), out_dtype),
        scratch_shapes=[pltpu.VMEM((tm, tn), F32)],
        compiler_params=_params(("parallel", "parallel", "arbitrary")),
        name="matmul_ksplit",
    )(*args)


def _glu_kernel(a_ref, wa_ref, wb_ref, o_ref):
    a = a_ref[...].astype(BF16)
    ya = jnp.dot(a, wa_ref[...], preferred_element_type=F32)
    yb = jnp.dot(a, wb_ref[...], preferred_element_type=F32)
    o_ref[...] = (ya * jax.nn.sigmoid(yb)).astype(o_ref.dtype)


def glu_matmul(a, wa, wb, out_dtype):
    m, k = a.shape
    _, n = wa.shape
    tm = _tile(m, 1024, SUBLANES)
    tn = _tile(n, 512)
    return pl.pallas_call(
        _glu_kernel,
        grid=(m // tm, n // tn),
        in_specs=[pl.BlockSpec((tm, k), lambda i, j: (i, 0)),
                  pl.BlockSpec((k, tn), lambda i, j: (0, j)),
                  pl.BlockSpec((k, tn), lambda i, j: (0, j))],
        out_specs=pl.BlockSpec((tm, tn), lambda i, j: (i, j)),
        out_shape=jax.ShapeDtypeStruct((m, n), out_dtype),
        compiler_params=_params(("parallel", "parallel")),
        name="glu_matmul",
    )(a, wa, wb)


def _rope_tile(x, cos_t, sin_t, half):
    lane = lax.broadcasted_iota(jnp.int32, x.shape, 1)
    first = (lane % (2 * half)) < half
    partner = jnp.where(first, pltpu.roll(x, LANES - half, 1), pltpu.roll(x, half, 1))
    return x * cos_t + partner * sin_t


def _rope_tables(positions, dim, pad_to):
    inv_freq = ROPE_THETA ** (-jnp.arange(0, dim, 2, dtype=F32) / dim)
    ang = positions.astype(F32).reshape(-1)[:, None] * inv_freq
    c, s = jnp.cos(ang), jnp.sin(ang)
    reps = pad_to // dim
    cos_t = jnp.tile(jnp.concatenate([c, c], axis=-1), (1, reps))
    sin_t = jnp.tile(jnp.concatenate([-s, s], axis=-1), (1, reps))
    t = c.shape[0]
    cos_t = jnp.concatenate([cos_t, jnp.ones((t, LANES - pad_to), F32)], axis=-1)
    sin_t = jnp.concatenate([sin_t, jnp.zeros((t, LANES - pad_to), F32)], axis=-1)
    return cos_t, sin_t


V_ONES_ROWS = 16
FLASH_TKB = 512
LOG2E = math.log2(math.e)


def _flash_kernel(*refs, tq, tkb, rep, dv, has_bias):
    if has_bias:
        q_ref, k_ref, vt_ref, b_ref, o_ref, m_ref, acc_ref, sa, sb, bmax_a, bmax_b = refs
    else:
        q_ref, k_ref, vt_ref, o_ref, m_ref, acc_ref, sa, sb, bmax_a, bmax_b = refs
        b_ref = None
    sa_ref, sb_ref = (sa, bmax_a), (sb, bmax_b)
    i = pl.program_id(2)
    rows = rep * tq

    m_ref[...] = jnp.full(m_ref.shape, MASK_VALUE, F32)
    acc_ref[...] = jnp.zeros(acc_ref.shape, F32)
    q = q_ref[0, 0, 0]

    def qk(j, slot):
        dst_ref, bmax_ref = slot
        start = pl.multiple_of(j * tkb, tkb)
        kb = k_ref[0, 0, pl.ds(start, tkb), :]
        s = lax.dot_general(kb, q, (((1,), (1,)), ((), ())),
                            preferred_element_type=F32)
        if has_bias:
            bias = b_ref[0, pl.ds(start, tkb), :].astype(F32)
            s = s + jnp.tile(bias, (1, rep))
        dst_ref[...] = s
        bmax_ref[...] = jnp.max(s, axis=0, keepdims=True)

    def softmax_pv(j, slot, masked):
        src_ref, bmax_ref = slot
        start = pl.multiple_of(j * tkb, tkb)
        vtb = vt_ref[0, 0, :, pl.ds(start, tkb)]
        s = src_ref[...]
        if masked:
            kpos = start + lax.broadcasted_iota(jnp.int32, (tkb, rows), 0)
            qlane = lax.broadcasted_iota(jnp.int32, (tkb, rows), 1)
            qpos = i * tq + (qlane % tq if rep > 1 else qlane)
            s = jnp.where(kpos <= qpos, s, MASK_VALUE)
            m_blk = jnp.max(s, axis=0, keepdims=True)
        else:
            m_blk = bmax_ref[...]
        m_prev = m_ref[...]
        m_new = jnp.maximum(m_prev, m_blk)
        p = jnp.exp2(s - m_new)
        alpha = jnp.exp2(m_prev - m_new)
        acc_ref[...] = alpha * acc_ref[...] + jnp.dot(vtb, p.astype(BF16), preferred_element_type=F32)
        m_ref[...] = m_new

    if has_bias:
        qk(0, sa_ref)
        n_blocks = ((i + 1) * tq + tkb - 1) // tkb

        def pair(jj, c):
            j0 = 2 * jj
            qk(j0 + 1, sb_ref)
            softmax_pv(j0, sa_ref, False)
            qk(jnp.minimum(j0 + 2, n_blocks - 1), sa_ref)
            softmax_pv(j0 + 1, sb_ref, False)
            return c

        lax.fori_loop(0, n_blocks // 2, pair, 0)

        @pl.when(n_blocks % 2 == 1)
        def _():
            softmax_pv(n_blocks - 1, sa_ref, False)
    else:
        qk(2 * i, sa_ref)
        qk(2 * i + 1, sb_ref)
        softmax_pv(2 * i, sa_ref, True)
        qk(0, sa_ref)
        softmax_pv(2 * i + 1, sb_ref, True)

        def pair(jj, c):
            j0 = 2 * jj
            qk(j0 + 1, sb_ref)
            softmax_pv(j0, sa_ref, False)
            qk(jnp.minimum(j0 + 2, 2 * i - 1), sa_ref)
            softmax_pv(j0 + 1, sb_ref, False)
            return c

        lax.fori_loop(0, i, pair, 0)

    acc = acc_ref[...]
    out = (acc[0:dv] * (1.0 / acc[dv:dv + 1])).T
    for r in range(rep):
        o_ref[0, :, r * dv:(r + 1) * dv] = out[r * tq:(r + 1) * tq].astype(o_ref.dtype)


def flash_attention(q, k, vt, bias, *, tq, rep, out_dtype):
    b, hk, nq, rows, dq = q.shape
    s = k.shape[2]
    dve = vt.shape[2]
    dv = dve - V_ONES_ROWS
    tkb = _tile(s, FLASH_TKB)
    has_bias = bias is not None
    assert has_bias or tq == 2 * tkb
    in_specs = [pl.BlockSpec((1, 1, 1, rows, dq), lambda bb, h, i: (bb, h, i, 0, 0)),
                pl.BlockSpec((1, 1, s, dq), lambda bb, h, i: (bb, h, 0, 0)),
                pl.BlockSpec((1, 1, dve, s), lambda bb, h, i: (bb, h, 0, 0))]
    args = [q, k, vt]
    if has_bias:
        in_specs.append(pl.BlockSpec((1, s, tq), lambda bb, h, i: (bb, 0, i)))
        args.append(bias)
    return pl.pallas_call(
        functools.partial(_flash_kernel, tq=tq, tkb=tkb, rep=rep, dv=dv, has_bias=has_bias),
        grid=(b, hk, nq),
        in_specs=in_specs,
        out_specs=pl.BlockSpec((1, tq, rep * dv), lambda bb, h, i: (bb, i, h)),
        out_shape=jax.ShapeDtypeStruct((b, s, hk * rep * dv), out_dtype),
        scratch_shapes=[pltpu.VMEM((1, rows), F32), pltpu.VMEM((dve, rows), F32),
                        pltpu.VMEM((tkb, rows), F32), pltpu.VMEM((tkb, rows), F32),
                        pltpu.VMEM((1, rows), F32), pltpu.VMEM((1, rows), F32)],
        compiler_params=_params(("parallel", "parallel", "parallel")),
        name="flash_attention",
    )(*args)


S5_NSEG = 2 * SUBLANES
S5_CBLK = LANES
S5_LT = 16


def _s5_kernel(u_ref, bbd_ref, cbd_ref, a_ref, ap_ref, d_ref, y_ref,
               ug_ref, bu_ref, xs_ref, init_ref, *, seg_len, lt):
    nseg = S5_NSEG
    ns = a_ref.shape[-1]
    a_re = jnp.broadcast_to(a_ref[0, 0:1, :], (nseg, ns))
    a_im = jnp.broadcast_to(a_ref[0, 1:2, :], (nseg, ns))
    ntiles = seg_len // lt

    def load_inputs(t, slot):
        for i in range(lt):
            ug_ref[slot, i * nseg:(i + 1) * nseg, :] = u_ref[0, pl.ds(t * lt + i, nseg, stride=seg_len), :]
        bu_ref[slot] = jnp.dot(ug_ref[slot].astype(BF16), bbd_ref[0], preferred_element_type=F32)

    def scan_tile(carry, slot, store):
        x_re, x_im = carry
        for i in range(lt):
            r0 = i * nseg
            b_re = bu_ref[slot, r0:r0 + nseg, 0:ns]
            b_im = bu_ref[slot, r0:r0 + nseg, ns:2 * ns]
            n_re = a_re * x_re - a_im * x_im + b_re
            n_im = a_re * x_im + a_im * x_re + b_im
            x_re, x_im = n_re, n_im
            if store:
                xs_ref[slot, r0:r0 + nseg, 0:ns] = x_re
                xs_ref[slot, r0:r0 + nseg, ns:2 * ns] = x_im
        return x_re, x_im

    def emit(t, slot):
        y = jnp.dot(xs_ref[slot].astype(BF16), cbd_ref[0], preferred_element_type=F32)
        y = jax.nn.gelu(y + d_ref[0] * ug_ref[slot])
        for i in range(lt):
            y_ref[0, pl.ds(t * lt + i, nseg, stride=seg_len), :] = y[i * nseg:(i + 1) * nseg]

    def sweep(carry, store):
        load_inputs(0, 0)

        def pair(tt, c):
            t0 = 2 * tt
            load_inputs(t0 + 1, 1)
            c = scan_tile(c, 0, store)
            if store:
                emit(t0, 0)
            load_inputs(jnp.minimum(t0 + 2, ntiles - 1), 0)
            c = scan_tile(c, 1, store)
            if store:
                emit(t0 + 1, 1)
            return c

        return lax.fori_loop(0, ntiles // 2, pair, carry)

    zeros = jnp.zeros((nseg, ns), F32)
    e_re, e_im = sweep((zeros, zeros), False)

    ap_re = ap_ref[0, 0:1, :]
    ap_im = ap_ref[0, 1:2, :]
    t_re = jnp.zeros((1, ns), F32)
    t_im = jnp.zeros((1, ns), F32)
    init_ref[0:1, :] = jnp.zeros((1, 2 * ns), F32)
    for s in range(1, nseg):
        p_re, p_im = e_re[s - 1:s], e_im[s - 1:s]
        t_re, t_im = (ap_re * t_re - ap_im * t_im + p_re,
                      ap_re * t_im + ap_im * t_re + p_im)
        init_ref[s:s + 1, 0:ns] = t_re
        init_ref[s:s + 1, ns:2 * ns] = t_im

    sweep((init_ref[:, 0:ns], init_ref[:, ns:2 * ns]), True)


def _s5_discretise(lam_re, lam_im, log_step, b_re, b_im, c_re, c_im, seg_len):
    g, n = lam_re.shape
    p = S5_GROUP
    lr = jnp.minimum(lam_re.astype(F32), S5_LAMBDA_RE_MAX)
    li = lam_im.astype(F32)
    step = jnp.exp(log_step.astype(F32))[:, None]
    mag = jnp.exp(lr * step)
    ab_re = mag * jnp.cos(li * step)
    ab_im = mag * jnp.sin(li * step)
    den = lr * lr + li * li
    f_re = ((ab_re - 1.0) * lr + ab_im * li) / den
    f_im = (ab_im * lr - (ab_re - 1.0) * li) / den
    br = b_re.astype(F32)
    bi = b_im.astype(F32)
    bb_re = f_re[..., None] * br - f_im[..., None] * bi
    bb_im = f_re[..., None] * bi + f_im[..., None] * br
    pw_re, pw_im = jnp.ones_like(ab_re), jnp.zeros_like(ab_im)
    sq_re, sq_im = ab_re, ab_im
    e = seg_len
    while e:
        if e & 1:
            pw_re, pw_im = pw_re * sq_re - pw_im * sq_im, pw_re * sq_im + pw_im * sq_re
        sq_re, sq_im = sq_re * sq_re - sq_im * sq_im, 2.0 * sq_re * sq_im
        e >>= 1
    gb = S5_CBLK // p
    nb = g // gb
    eye = jnp.eye(gb, dtype=F32)

    def blockdiag_in(bb):
        x = bb.reshape(nb, gb, n, p)
        return jnp.einsum("bgnp,gh->bgphn", x, eye).reshape(nb, gb * p, gb * n)

    def blockdiag_out(c):
        x = c.astype(F32).reshape(nb, gb, p, n)
        return jnp.einsum("bgpn,gh->bgnhp", x, eye).reshape(nb, gb * n, gb * p)

    bbd = jnp.concatenate([blockdiag_in(bb_re), blockdiag_in(bb_im)], axis=2).astype(BF16)
    cbd = jnp.concatenate([blockdiag_out(c_re), -blockdiag_out(c_im)], axis=1).astype(BF16)
    a = jnp.stack([ab_re.reshape(nb, gb * n), ab_im.reshape(nb, gb * n)], axis=1)
    ap = jnp.stack([pw_re.reshape(nb, gb * n), pw_im.reshape(nb, gb * n)], axis=1)
    return bbd, cbd, a, ap


def s5_scan(u, lam_re, lam_im, log_step, b_re, b_im, c_re, c_im, d_skip):
    b, s, w = u.shape
    seg_len = s // S5_NSEG
    lt = min(S5_LT, seg_len)
    assert (seg_len // lt) % 2 == 0
    bbd, cbd, a, ap = _s5_discretise(lam_re, lam_im, log_step, b_re, b_im, c_re, c_im, seg_len)
    nb, cw, ns2 = bbd.shape
    ns = ns2 // 2
    rows = lt * S5_NSEG
    return pl.pallas_call(
        functools.partial(_s5_kernel, seg_len=seg_len, lt=lt),
        grid=(b, nb),
        in_specs=[pl.BlockSpec((1, s, cw), lambda bb, j: (bb, 0, j)),
                  pl.BlockSpec((1, cw, ns2), lambda bb, j: (j, 0, 0)),
                  pl.BlockSpec((1, ns2, cw), lambda bb, j: (j, 0, 0)),
                  pl.BlockSpec((1, 2, ns), lambda bb, j: (j, 0, 0)),
                  pl.BlockSpec((1, 2, ns), lambda bb, j: (j, 0, 0)),
                  pl.BlockSpec((1, 1, cw), lambda bb, j: (j, 0, 0))],
        out_specs=pl.BlockSpec((1, s, cw), lambda bb, j: (bb, 0, j)),
        out_shape=jax.ShapeDtypeStruct((b, s, w), F32),
        scratch_shapes=[pltpu.VMEM((2, rows, cw), F32), pltpu.VMEM((2, rows, ns2), F32),
                        pltpu.VMEM((2, rows, ns2), F32), pltpu.VMEM((S5_NSEG, ns2), F32)],
        compiler_params=_params(("parallel", "parallel")),
        name="s5_scan",
    )(u, bbd, cbd, a, ap, d_skip.astype(F32).reshape(nb, 1, cw))


def _even_lat_kernel(lat_ref, gq_ref, gkv_ref, cos_ref, sin_ref, qn_ref, kvn_ref, kr_ref, *, q_rank, kv_rank):
    lat = lat_ref[...]
    q_lat = lat[:, 0:q_rank]
    kv_lat = lat[:, q_rank:q_rank + kv_rank]
    kr = lat[:, q_rank + kv_rank:q_rank + kv_rank + LANES]
    rq = lax.rsqrt(jnp.mean(q_lat * q_lat, axis=-1, keepdims=True) + EPS)
    qn_ref[...] = (q_lat * rq * gq_ref[...]).astype(qn_ref.dtype)
    rkv = lax.rsqrt(jnp.mean(kv_lat * kv_lat, axis=-1, keepdims=True) + EPS)
    kvn_ref[...] = (kv_lat * rkv * gkv_ref[...]).astype(kvn_ref.dtype)
    kr = _rope_tile(kr, cos_ref[...], sin_ref[...], MLA_ROPE // 2)
    lane = lax.broadcasted_iota(jnp.int32, kr.shape, 1)
    lo = jnp.where(lane < MLA_ROPE, kr, 0.0)
    hi = pltpu.roll(lo, MLA_ROPE, 1)
    kr_ref[:, 0:LANES] = lo.astype(kr_ref.dtype)
    kr_ref[:, LANES:2 * LANES] = hi.astype(kr_ref.dtype)


def even_lat_prep(lat, gq, gkv, cos_t, sin_t):
    m, wl = lat.shape
    q_rank, kv_rank = gq.shape[0], gkv.shape[0]
    tm = _tile(m, 512, SUBLANES)
    return pl.pallas_call(
        functools.partial(_even_lat_kernel, q_rank=q_rank, kv_rank=kv_rank),
        grid=(m // tm,),
        in_specs=[pl.BlockSpec((tm, wl), lambda i: (i, 0)),
                  pl.BlockSpec((1, q_rank), lambda i: (0, 0)),
                  pl.BlockSpec((1, kv_rank), lambda i: (0, 0)),
                  pl.BlockSpec((tm, LANES), lambda i: (i, 0)),
                  pl.BlockSpec((tm, LANES), lambda i: (i, 0))],
        out_specs=[pl.BlockSpec((tm, q_rank), lambda i: (i, 0)),
                   pl.BlockSpec((tm, kv_rank), lambda i: (i, 0)),
                   pl.BlockSpec((tm, 2 * LANES), lambda i: (i, 0))],
        out_shape=[jax.ShapeDtypeStruct((m, q_rank), BF16),
                   jax.ShapeDtypeStruct((m, kv_rank), BF16),
                   jax.ShapeDtypeStruct((m, 2 * LANES), BF16)],
        compiler_params=_params(("parallel",)),
        name="even_lat_prep",
    )(lat, gq.reshape(1, -1).astype(F32), gkv.reshape(1, -1).astype(F32), cos_t, sin_t)


def _mla_q_kernel(qn_ref, w_ref, cos_ref, sin_ref, qf_ref, *, heads, scale):
    q = jnp.dot(qn_ref[...], w_ref[...], preferred_element_type=F32)
    cos_t, sin_t = cos_ref[...], sin_ref[...]
    nope_w = heads * MLA_NOPE
    lane = lax.broadcasted_iota(jnp.int32, (q.shape[0], LANES), 1)
    for pair in range(heads // 2):
        rope = _rope_tile(q[:, nope_w + pair * LANES:nope_w + (pair + 1) * LANES], cos_t, sin_t, MLA_ROPE // 2)
        for sub in range(2):
            h = 2 * pair + sub
            keep = (lane < MLA_ROPE) if sub == 0 else (lane >= MLA_ROPE)
            qf_ref[0, h, :, 0:LANES] = (q[:, h * MLA_NOPE:(h + 1) * MLA_NOPE] * scale).astype(qf_ref.dtype)
            qf_ref[0, h, :, LANES:2 * LANES] = (jnp.where(keep, rope, 0.0) * scale).astype(qf_ref.dtype)


def _mla_kv_kernel(kvn_ref, w_ref, kr_ref, kf_ref, vt_ref, *, heads):
    kv = jnp.dot(kvn_ref[...], w_ref[...], preferred_element_type=F32)
    for h in range(heads):
        c0 = h * (MLA_NOPE + MLA_V)
        kf_ref[0, h, :, 0:LANES] = kv[:, c0:c0 + MLA_NOPE].astype(kf_ref.dtype)
        kf_ref[0, h, :, LANES:2 * LANES] = kr_ref[:, (h % 2) * LANES:(h % 2 + 1) * LANES]
        vt_ref[0, h, 0:MLA_V, :] = kv[:, c0 + MLA_NOPE:c0 + MLA_NOPE + MLA_V].T.astype(vt_ref.dtype)
        vt_ref[0, h, MLA_V:MLA_V + V_ONES_ROWS, :] = jnp.ones((V_ONES_ROWS, kv.shape[0]), vt_ref.dtype)


def mla_projections(qn, kvn, kr2, wq, wkv, cos_t, sin_t, b, s, heads):
    t = qn.shape[0]
    tm = _tile(s, 256, SUBLANES)
    nt = s // tm
    scale = (MLA_NOPE + MLA_ROPE) ** -0.5 * LOG2E
    head_spec = pl.BlockSpec((1, heads, tm, 2 * LANES), lambda i: (i // nt, 0, i % nt, 0))
    qf = pl.pallas_call(
        functools.partial(_mla_q_kernel, heads=heads, scale=scale),
        grid=(t // tm,),
        in_specs=[pl.BlockSpec((tm, qn.shape[1]), lambda i: (i, 0)),
                  pl.BlockSpec(wq.shape, lambda i: (0, 0)),
                  pl.BlockSpec((tm, LANES), lambda i: (i, 0)),
                  pl.BlockSpec((tm, LANES), lambda i: (i, 0))],
        out_specs=head_spec,
        out_shape=jax.ShapeDtypeStruct((b, heads, s, 2 * LANES), BF16),
        compiler_params=_params(("parallel",)),
        name="mla_q_proj",
    )(qn, wq, cos_t, sin_t)
    kf, vt = pl.pallas_call(
        functools.partial(_mla_kv_kernel, heads=heads),
        grid=(t // tm,),
        in_specs=[pl.BlockSpec((tm, kvn.shape[1]), lambda i: (i, 0)),
                  pl.BlockSpec(wkv.shape, lambda i: (0, 0)),
                  pl.BlockSpec((tm, 2 * LANES), lambda i: (i, 0))],
        out_specs=[head_spec,
                   pl.BlockSpec((1, heads, MLA_V + V_ONES_ROWS, tm), lambda i: (i // nt, 0, 0, i % nt))],
        out_shape=[jax.ShapeDtypeStruct((b, heads, s, 2 * LANES), BF16),
                   jax.ShapeDtypeStruct((b, heads, MLA_V + V_ONES_ROWS, s), BF16)],
        compiler_params=_params(("parallel",)),
        name="mla_kv_proj",
    )(kvn, wkv, kr2)
    return qf, kf, vt


def even_mixer(x2, h, b, s, cos_m, sin_m, w_in, lam_re, lam_im, log_step, b_re, b_im, c_re, c_im,
               d_skip, glu_a, glu_b, q_norm_g, w_q_up, kv_norm_g, w_kv_up, w_out):
    t, d = h.shape
    s5_w = d_skip.shape[0]
    q_rank, kv_rank = q_norm_g.shape[0], kv_norm_g.shape[0]
    heads = w_kv_up.shape[1] // (MLA_NOPE + MLA_V)
    lat_w = q_rank + kv_rank + MLA_ROPE
    lat_pad = -(-(q_rank + kv_rank + LANES) // (2 * LANES)) * (2 * LANES)

    w_u = w_in[:, :s5_w].astype(BF16)
    w_lat = jnp.pad(w_in[:, s5_w:], ((0, 0), (0, lat_pad - lat_w))).astype(BF16)
    u = matmul(h, w_u, F32)
    lat = matmul(h, w_lat, F32)

    y = s5_scan(u.reshape(b, s, s5_w), lam_re, lam_im, log_step, b_re, b_im, c_re, c_im, d_skip)
    s5_out = glu_matmul(y.reshape(t, s5_w), glu_a.astype(BF16), glu_b.astype(BF16), BF16)

    qn, kvn, kr2 = even_lat_prep(lat, q_norm_g, kv_norm_g, cos_m, sin_m)
    wq = w_q_up.reshape(q_rank, heads, MLA_NOPE + MLA_ROPE)
    wq = jnp.concatenate([wq[:, :, :MLA_NOPE].reshape(q_rank, heads * MLA_NOPE),
                          wq[:, :, MLA_NOPE:].reshape(q_rank, heads * MLA_ROPE)], axis=1).astype(BF16)
    qf, kf, vf = mla_projections(qn, kvn, kr2, wq, w_kv_up.astype(BF16), cos_m, sin_m, b, s, heads)
    tq = 2 * _tile(s, FLASH_TKB)
    mla_out = flash_attention(qf.reshape(b, heads, s // tq, tq, 2 * LANES), kf, vf, None,
                              tq=tq, rep=1, out_dtype=BF16)
    mixed = jnp.concatenate([s5_out, mla_out.reshape(t, -1)], axis=-1)
    return matmul(mixed, w_out.astype(BF16), F32, kind="residual", res=x2)


def _odd_lat_kernel(z_ref, gq_ref, lng_ref, lnb_ref, cos_ref, sin_ref,
                    qn_ref, k_ref, v_ref, kidx_ref, w_ref, *, q_rank, w_scale):
    cos_t, sin_t = cos_ref[...], sin_ref[...]
    half = ROT_DIM // 2
    z = z_ref[0]
    q_lat = z[:, 0:q_rank]
    rq = lax.rsqrt(jnp.mean(q_lat * q_lat, axis=-1, keepdims=True) + EPS)
    qn_ref[0] = (q_lat * rq * gq_ref[...]).astype(qn_ref.dtype)
    kvh = DSA_KV_HEADS
    d = DSA_HEAD_DIM
    for g in range(kvh):
        kh = z[:, q_rank + g * d:q_rank + (g + 1) * d]
        k_ref[0, g] = _rope_tile(kh, cos_t, sin_t, half).astype(k_ref.dtype)
        v_ref[0, g, 0:d, :] = z[:, q_rank + (kvh + g) * d:q_rank + (kvh + g + 1) * d].T.astype(v_ref.dtype)
        v_ref[0, g, d:d + V_ONES_ROWS, :] = jnp.ones((V_ONES_ROWS, z.shape[0]), v_ref.dtype)
    off = q_rank + 2 * kvh * d
    ki = z[:, off:off + IDX_DIM]
    kc = ki - jnp.mean(ki, axis=-1, keepdims=True)
    var = jnp.mean(kc * kc, axis=-1, keepdims=True)
    ki = kc * lax.rsqrt(var + EPS) * lng_ref[...] + lnb_ref[...]
    kidx_ref[0] = _rope_tile(ki, cos_t, sin_t, half).astype(kidx_ref.dtype)
    w_ref[0] = (z[:, off + IDX_DIM:off + IDX_DIM + LANES] * w_scale).T


def odd_lat_prep(z, gq, ln_g, ln_b, cos_t, sin_t, idx_heads):
    b, s, zw = z.shape
    q_rank = gq.shape[0]
    tm = _tile(s, 512, SUBLANES)
    nt = s // tm
    w_scale = idx_heads ** -0.5 * IDX_DIM ** -0.5
    kvh, d = DSA_KV_HEADS, DSA_HEAD_DIM
    return pl.pallas_call(
        functools.partial(_odd_lat_kernel, q_rank=q_rank, w_scale=w_scale),
        grid=(b, nt),
        in_specs=[pl.BlockSpec((1, tm, zw), lambda bb, i: (bb, i, 0)),
                  pl.BlockSpec((1, q_rank), lambda bb, i: (0, 0)),
                  pl.BlockSpec((1, IDX_DIM), lambda bb, i: (0, 0)),
                  pl.BlockSpec((1, IDX_DIM), lambda bb, i: (0, 0)),
                  pl.BlockSpec((tm, LANES), lambda bb, i: (bb * nt + i, 0)),
                  pl.BlockSpec((tm, LANES), lambda bb, i: (bb * nt + i, 0))],
        out_specs=[pl.BlockSpec((1, tm, q_rank), lambda bb, i: (bb, i, 0)),
                   pl.BlockSpec((1, kvh, tm, d), lambda bb, i: (bb, 0, i, 0)),
                   pl.BlockSpec((1, kvh, d + V_ONES_ROWS, tm), lambda bb, i: (bb, 0, 0, i)),
                   pl.BlockSpec((1, tm, IDX_DIM), lambda bb, i: (bb, i, 0)),
                   pl.BlockSpec((1, LANES, tm), lambda bb, i: (bb, 0, i))],
        out_shape=[jax.ShapeDtypeStruct((b, s, q_rank), BF16),
                   jax.ShapeDtypeStruct((b, kvh, s, d), BF16),
                   jax.ShapeDtypeStruct((b, kvh, d + V_ONES_ROWS, s), BF16),
                   jax.ShapeDtypeStruct((b, s, IDX_DIM), BF16),
                   jax.ShapeDtypeStruct((b, LANES, s), F32)],
        compiler_params=_params(("parallel", "parallel")),
        name="odd_lat_prep",
    )(z, gq.reshape(1, -1).astype(F32), ln_g.reshape(1, -1).astype(F32),
      ln_b.reshape(1, -1).astype(F32), cos_t, sin_t)


def _mm_rope_kernel(a_ref, w_ref, cos_ref, sin_ref, o_ref, *, n_heads, scale, stack_rows):
    acc = jnp.dot(a_ref[...], w_ref[...], preferred_element_type=F32)
    cos_t, sin_t = cos_ref[...], sin_ref[...]
    d = DSA_HEAD_DIM
    for r in range(n_heads):
        xh = _rope_tile(acc[:, r * d:(r + 1) * d], cos_t, sin_t, ROT_DIM // 2) * scale
        if stack_rows:
            o_ref[0, 0, 0, r * stack_rows:(r + 1) * stack_rows, :] = xh.astype(o_ref.dtype)
        else:
            o_ref[:, r * d:(r + 1) * d] = xh.astype(o_ref.dtype)


def dsa_q_proj(qn, w_q, cos_t, sin_t, b, s, tq):
    t, kq = qn.shape
    d = DSA_HEAD_DIM
    heads = w_q.shape[1] // d
    rep = heads // DSA_KV_HEADS
    nt = s // tq
    return pl.pallas_call(
        functools.partial(_mm_rope_kernel, n_heads=rep, scale=d ** -0.5 * LOG2E, stack_rows=tq),
        grid=(t // tq, DSA_KV_HEADS),
        in_specs=[pl.BlockSpec((tq, kq), lambda i, g: (i, 0)),
                  pl.BlockSpec((kq, rep * d), lambda i, g: (0, g)),
                  pl.BlockSpec((tq, LANES), lambda i, g: (i, 0)),
                  pl.BlockSpec((tq, LANES), lambda i, g: (i, 0))],
        out_specs=pl.BlockSpec((1, 1, 1, rep * tq, d), lambda i, g: (i // nt, g, i % nt, 0, 0)),
        out_shape=jax.ShapeDtypeStruct((b, DSA_KV_HEADS, nt, rep * tq, d), BF16),
        compiler_params=_params(("parallel", "parallel")),
        name="dsa_q_proj",
    )(qn, w_q, cos_t, sin_t)


def idx_q_proj(qn, w_qi, cos_t, sin_t):
    t, kq = qn.shape
    n = w_qi.shape[1]
    tm = _tile(t, 1024, SUBLANES)
    tn = _tile(n, 512)
    return pl.pallas_call(
        functools.partial(_mm_rope_kernel, n_heads=tn // IDX_DIM, scale=1.0, stack_rows=0),
        grid=(t // tm, n // tn),
        in_specs=[pl.BlockSpec((tm, kq), lambda i, j: (i, 0)),
                  pl.BlockSpec((kq, tn), lambda i, j: (0, j)),
                  pl.BlockSpec((tm, LANES), lambda i, j: (i, 0)),
                  pl.BlockSpec((tm, LANES), lambda i, j: (i, 0))],
        out_specs=pl.BlockSpec((tm, tn), lambda i, j: (i, j)),
        out_shape=jax.ShapeDtypeStruct((t, n), BF16),
        compiler_params=_params(("parallel", "parallel")),
        name="idx_q_proj",
    )(qn, w_qi, cos_t, sin_t)


def _ukey_to_float(u):
    bits = jnp.where(u < 0, u & jnp.int32(0x7FFFFFFF), ~u)
    return lax.bitcast_convert_type(bits, F32)


def _threshold_of_key(u):
    thr = _ukey_to_float(u)
    return jnp.where((thr != thr) & (u >= 0), jnp.float32(-jnp.inf), thr)


def _indexer_kernel(qi_ref, wt_ref, kidx_ref, bias_ref, score_ref, lim_ref, *, tq, tkb, idx_heads, top_k):
    i = pl.program_id(1)
    s_len = kidx_ref.shape[1]
    n_blocks = ((i + 1) * tq + tkb - 1) // tkb
    neg_inf = jnp.float32(-jnp.inf)
    qpos = i * tq + lax.broadcasted_iota(jnp.int32, (1, tq), 1)

    def causal(start):
        kpos = start + lax.broadcasted_iota(jnp.int32, (tkb, tq), 0)
        return kpos <= qpos

    def score_block(j, c):
        start = pl.multiple_of(j * tkb, tkb)
        kb = kidx_ref[0, pl.ds(start, tkb), :]
        acc = jnp.zeros((tkb, tq), F32)
        for h in range(idx_heads):
            logits = lax.dot_general(kb, qi_ref[0, :, h * IDX_DIM:(h + 1) * IDX_DIM],
                                     (((1,), (1,)), ((), ())), preferred_element_type=F32)
            acc = acc + jnp.maximum(logits, 0.0) * wt_ref[0, h:h + 1, :]
        score_ref[pl.ds(start, tkb), :] = jnp.where(causal(start), acc, neg_inf)
        return c

    lax.fori_loop(0, n_blocks, score_block, 0)

    def count_where(pred):
        def body(j, cnt):
            start = pl.multiple_of(j * tkb, tkb)
            ind = jnp.where(pred(score_ref[pl.ds(start, tkb), :], start), 1.0, 0.0)
            part = jnp.sum(ind.reshape(tkb // (8 * SUBLANES), 8, SUBLANES, tq), axis=1)
            return cnt + jnp.sum(part, axis=0)
        cnt = lax.fori_loop(0, n_blocks, body, jnp.zeros((SUBLANES, tq), F32))
        return jnp.sum(cnt, axis=0, keepdims=True)

    def count_ge(thr):
        return count_where(lambda blk, start: blk >= thr)

    k_f = jnp.float32(top_k)
    few = qpos < top_k

    def search_cond(st):
        bi, _, cnt_u = st
        pending = jnp.where(few | (cnt_u == k_f), 0.0, 1.0)
        return (bi < 32) & (jnp.max(pending) > 0.0)

    def search_step(st):
        bi, u, cnt_u = st
        cand = u | (jnp.int32(1) << (31 - bi))
        cnt = count_ge(_threshold_of_key(cand))
        take = cnt >= k_f
        return bi + 1, jnp.where(take, cand, u), jnp.where(take, cnt, cnt_u)

    total = (n_blocks * tkb).astype(F32)
    _, u, cnt_u = lax.while_loop(search_cond, search_step,
                                 (jnp.int32(0), jnp.zeros((1, tq), jnp.int32), jnp.full((1, tq), total, F32)))
    thr = jnp.where(few, neg_inf, _threshold_of_key(u))

    def key_pos(start):
        return start + lax.broadcasted_iota(jnp.int32, (tkb, tq), 0)

    excess = jnp.logical_not(few) & (cnt_u > k_f)
    lim_ref[...] = jnp.full((1, tq), s_len, jnp.int32)

    @pl.when(jnp.max(jnp.where(excess, 1.0, 0.0)) > 0.0)
    def _():
        need = k_f - count_where(lambda blk, start: blk > thr)
        nbits = s_len.bit_length()

        def tie_step(bi, p):
            cand = p | (jnp.int32(1) << (nbits - 1 - bi))
            cnt = count_where(lambda blk, start: (blk == thr) & (key_pos(start) < cand))
            return jnp.where(cnt <= need, cand, p)

        p = lax.fori_loop(0, nbits, tie_step, jnp.zeros((1, tq), jnp.int32))
        lim_ref[...] = jnp.where(excess, p, s_len)

    lim = lim_ref[...]

    def write_block(j, c):
        start = pl.multiple_of(j * tkb, tkb)
        blk = score_ref[pl.ds(start, tkb), :]
        keep = ((blk > thr) | ((blk == thr) & (key_pos(start) < lim))) & causal(start)
        bias_ref[0, pl.ds(start, tkb), :] = jnp.where(keep, 0.0, MASK_VALUE).astype(bias_ref.dtype)
        return c

    lax.fori_loop(0, n_blocks, write_block, 0)

    def fill_block(j, c):
        start = pl.multiple_of(j * tkb, tkb)
        bias_ref[0, pl.ds(start, tkb), :] = jnp.full((tkb, tq), MASK_VALUE, bias_ref.dtype)
        return c

    lax.fori_loop(n_blocks, s_len // tkb, fill_block, 0)


IDX_TQ = 512


def dsa_indexer(qi, wt, kidx, idx_heads, top_k):
    b, s, _ = qi.shape
    tq = _tile(s, IDX_TQ)
    tkb = _tile(s, 512)
    assert tkb >= top_k or tkb == s
    return pl.pallas_call(
        functools.partial(_indexer_kernel, tq=tq, tkb=tkb, idx_heads=idx_heads, top_k=top_k),
        grid=(b, s // tq),
        in_specs=[pl.BlockSpec((1, tq, idx_heads * IDX_DIM), lambda bb, i: (bb, i, 0)),
                  pl.BlockSpec((1, LANES, tq), lambda bb, i: (bb, 0, i)),
                  pl.BlockSpec((1, s, IDX_DIM), lambda bb, i: (bb, 0, 0))],
        out_specs=pl.BlockSpec((1, s, tq), lambda bb, i: (bb, 0, i)),
        out_shape=jax.ShapeDtypeStruct((b, s, s), BF16),
        scratch_shapes=[pltpu.VMEM((s, tq), F32), pltpu.VMEM((1, tq), jnp.int32)],
        compiler_params=_params(("parallel", "parallel")),
        name="dsa_indexer",
    )(qi, wt, kidx)


def odd_mixer(x2, h, b, s, cos_p, sin_p, w_in, q_norm_g, w_q_up, w_idx_q, k_ln_g, k_ln_b, w_out):
    t, d = h.shape
    q_rank = q_norm_g.shape[0]
    heads = w_q_up.shape[1] // DSA_HEAD_DIM
    idx_heads = w_idx_q.shape[1] // IDX_DIM
    in_w = w_in.shape[1]
    z_w = -(-(in_w - idx_heads + LANES) // (2 * LANES)) * (2 * LANES)
    w_z = jnp.pad(w_in, ((0, 0), (0, z_w - in_w))).astype(BF16)
    z = matmul(h, w_z, F32)
    qn, k, vt, kidx, wt = odd_lat_prep(z.reshape(b, s, z_w), q_norm_g, k_ln_g, k_ln_b, cos_p, sin_p, idx_heads)
    tq = _tile(s, 256)
    qn2 = qn.reshape(t, q_rank)
    q = dsa_q_proj(qn2, w_q_up.astype(BF16), cos_p, sin_p, b, s, tq)
    qi = idx_q_proj(qn2, w_idx_q.astype(BF16), cos_p, sin_p).reshape(b, s, -1)
    top_k = min(IDX_TOPK_MAX, s // 4)
    bias = dsa_indexer(qi, wt, kidx, idx_heads, top_k)
    o = flash_attention(q, k, vt, bias, tq=tq, rep=heads // DSA_KV_HEADS, out_dtype=BF16)
    return matmul(o.reshape(t, -1), w_out.astype(BF16), F32, kind="residual", res=x2)


def sq_relu_mlp(x2, h, w_up, w_down, layer):
    a = matmul(h, w_up, BF16, kind="relu2", layer=layer)
    return matmul(a, w_down, F32, kind="residual", res=x2, layer=layer)


def kernel(x, positions, norm_mix_g, norm_mlp_g, final_norm_g, even_w_in, s5_lam_re, s5_lam_im, s5_log_step, s5_b_re, s5_b_im, s5_c_re, s5_c_im, s5_d, s5_glu_a, s5_glu_b, mla_q_norm_g, mla_w_q_up, mla_kv_norm_g, mla_w_kv_up, even_w_out, odd_w_in, dsa_q_norm_g, dsa_w_q_up, idx_w_q, idx_k_ln_g, idx_k_ln_b, odd_w_out, mlp_w_up, mlp_w_down):
    b, s, d = x.shape
    depth = norm_mix_g.shape[0]
    cos_m, sin_m = _rope_tables(positions, MLA_ROPE, LANES)
    cos_p, sin_p = _rope_tables(positions, ROT_DIM, ROT_DIM)
    x2 = x.reshape(b * s, d)
    for layer in range(depth):
        i = layer // 2
        h = rmsnorm(x2, norm_mix_g[layer], BF16)
        if layer % 2 == 0:
            x2 = even_mixer(x2, h, b, s, cos_m, sin_m, even_w_in[i], s5_lam_re[i], s5_lam_im[i],
                            s5_log_step[i], s5_b_re[i], s5_b_im[i], s5_c_re[i], s5_c_im[i], s5_d[i],
                            s5_glu_a[i], s5_glu_b[i], mla_q_norm_g[i], mla_w_q_up[i],
                            mla_kv_norm_g[i], mla_w_kv_up[i], even_w_out[i])
        else:
            x2 = odd_mixer(x2, h, b, s, cos_p, sin_p, odd_w_in[i], dsa_q_norm_g[i], dsa_w_q_up[i],
                           idx_w_q[i], idx_k_ln_g[i], idx_k_ln_b[i], odd_w_out[i])
        h = rmsnorm(x2, norm_mlp_g[layer], BF16)
        x2 = sq_relu_mlp(x2, h, mlp_w_up, mlp_w_down, layer)
    return rmsnorm(x2, final_norm_g, x.dtype).reshape(b, s, d)
```

```python
import functools
import math

import jax
import jax.numpy as jnp
from jax import lax
from jax.experimental import pallas as pl
from jax.experimental.pallas import tpu as pltpu

F32 = jnp.float32
BF16 = jnp.bfloat16

EPS = 1e-6
ROPE_THETA = 500000.0
S5_GROUP = 16
S5_LAMBDA_RE_MAX = -1e-4
MLA_NOPE = 128
MLA_ROPE = 64
MLA_V = 128
DSA_HEAD_DIM = 128
DSA_KV_HEADS = 4
IDX_DIM = 128
IDX_TOPK_MAX = 256
ROT_DIM = DSA_HEAD_DIM // 4

LANES = 128
SUBLANES = 8
VMEM_LIMIT_BYTES = 56 * 1024 * 1024
MASK_VALUE = -1e30


def _params(semantics):
    return pltpu.CompilerParams(dimension_semantics=semantics, vmem_limit_bytes=VMEM_LIMIT_BYTES)


def _tile(dim, pref, align=LANES):
    if dim <= pref:
        return dim
    t = (pref // align) * align
    while t >= align:
        if dim % t == 0:
            return t
        t -= align
    return dim


def _rmsnorm_kernel(x_ref, g_ref, o_ref):
    x = x_ref[...].astype(F32)
    r = lax.rsqrt(jnp.mean(x * x, axis=-1, keepdims=True) + EPS)
    o_ref[...] = (x * r * g_ref[...]).astype(o_ref.dtype)


def rmsnorm(x, g, out_dtype):
    m, d = x.shape
    tm = _tile(m, 256, SUBLANES)
    return pl.pallas_call(
        _rmsnorm_kernel,
        grid=(m // tm,),
        in_specs=[pl.BlockSpec((tm, d), lambda i: (i, 0)),
                  pl.BlockSpec((1, d), lambda i: (0, 0))],
        out_specs=pl.BlockSpec((tm, d), lambda i: (i, 0)),
        out_shape=jax.ShapeDtypeStruct((m, d), out_dtype),
        compiler_params=_params(("parallel",)),
        name="rmsnorm",
    )(x, g.reshape(1, d).astype(F32))


def _mm_epilogue(acc, kind, res_ref):
    if kind == "relu2":
        a = jnp.maximum(acc, 0.0)
        return a * a
    if kind == "residual":
        return res_ref[...].astype(F32) + acc
    return acc


def _mm_kernel_single(*refs, kind):
    if kind == "residual":
        a_ref, w_ref, res_ref, o_ref = refs
    else:
        a_ref, w_ref, o_ref = refs
        res_ref = None
    acc = jnp.dot(a_ref[...].astype(BF16), w_ref[...].astype(BF16), preferred_element_type=F32)
    o_ref[...] = _mm_epilogue(acc, kind, res_ref).astype(o_ref.dtype)


def _mm_kernel_ksplit(*refs, kind, nk):
    if kind == "residual":
        a_ref, w_ref, res_ref, o_ref, acc_ref = refs
    else:
        a_ref, w_ref, o_ref, acc_ref = refs
        res_ref = None
    k = pl.program_id(2)

    @pl.when(k == 0)
    def _():
        acc_ref[...] = jnp.zeros_like(acc_ref)

    acc_ref[...] += jnp.dot(a_ref[...].astype(BF16), w_ref[...].astype(BF16), preferred_element_type=F32)

    @pl.when(k == nk - 1)
    def _():
        o_ref[...] = _mm_epilogue(acc_ref[...], kind, res_ref).astype(o_ref.dtype)


def matmul(a, w, out_dtype, kind="none", res=None, layer=None):
    m, k = a.shape
    n = w.shape[-1]
    tm_pref, tn_pref, tk_pref = (1024, 512, k) if k <= 4096 else (1024, 1024, 2048)
    tm = _tile(m, tm_pref, SUBLANES)
    tn = _tile(n, tn_pref)
    tk = _tile(k, tk_pref)

    def w_spec(tk_, tn_, index):
        if layer is None:
            return pl.BlockSpec((tk_, tn_), index)
        return pl.BlockSpec((None, tk_, tn_), lambda *g: (layer,) + index(*g))

    if tk == k:
        in_specs = [pl.BlockSpec((tm, k), lambda i, j: (i, 0)),
                    w_spec(k, tn, lambda i, j: (0, j))]
        args = [a, w]
        if kind == "residual":
            in_specs.append(pl.BlockSpec((tm, tn), lambda i, j: (i, j)))
            args.append(res)
        return pl.pallas_call(
            functools.partial(_mm_kernel_single, kind=kind),
            grid=(m // tm, n // tn),
            in_specs=in_specs,
            out_specs=pl.BlockSpec((tm, tn), lambda i, j: (i, j)),
            out_shape=jax.ShapeDtypeStruct((m, n), out_dtype),
            compiler_params=_params(("parallel", "parallel")),
            name="matmul",
        )(*args)
    nk = k // tk
    in_specs = [pl.BlockSpec((tm, tk), lambda i, j, kk: (i, kk)),
                w_spec(tk, tn, lambda i, j, kk: (kk, j))]
    args = [a, w]
    if kind == "residual":
        in_specs.append(pl.BlockSpec((tm, tn), lambda i, j, kk: (i, j)))
        args.append(res)
    return pl.pallas_call(
        functools.partial(_mm_kernel_ksplit, kind=kind, nk=nk),
        grid=(m // tm, n // tn, nk),
        in_specs=in_specs,
        out_specs=pl.BlockSpec((tm, tn), lambda i, j, kk: (i, j)),
        out_shape=jax.ShapeDtypeStruct((m, n), out_dtype),
        scratch_shapes=[pltpu.VMEM((tm, tn), F32)],
        compiler_params=_params(("parallel", "parallel", "arbitrary")),
        name="matmul_ksplit",
    )(*args)


def _glu_kernel(a_ref, wa_ref, wb_ref, o_ref):
    a = a_ref[...].astype(BF16)
    ya = jnp.dot(a, wa_ref[...], preferred_element_type=F32)
    yb = jnp.dot(a, wb_ref[...], preferred_element_type=F32)
    o_ref[...] = (ya * jax.nn.sigmoid(yb)).astype(o_ref.dtype)


def glu_matmul(a, wa, wb, out_dtype):
    m, k = a.shape
    _, n = wa.shape
    tm = _tile(m, 1024, SUBLANES)
    tn = _tile(n, 512)
    return pl.pallas_call(
        _glu_kernel,
        grid=(m // tm, n // tn),
        in_specs=[pl.BlockSpec((tm, k), lambda i, j: (i, 0)),
                  pl.BlockSpec((k, tn), lambda i, j: (0, j)),
                  pl.BlockSpec((k, tn), lambda i, j: (0, j))],
        out_specs=pl.BlockSpec((tm, tn), lambda i, j: (i, j)),
        out_shape=jax.ShapeDtypeStruct((m, n), out_dtype),
        compiler_params=_params(("parallel", "parallel")),
        name="glu_matmul",
    )(a, wa, wb)


def _rope_tile(x, cos_t, sin_t, half):
    lane = lax.broadcasted_iota(jnp.int32, x.shape, 1)
    first = (lane % (2 * half)) < half
    partner = jnp.where(first, pltpu.roll(x, LANES - half, 1), pltpu.roll(x, half, 1))
    return x * cos_t + partner * sin_t


def _rope_tables(positions, dim, pad_to):
    inv_freq = ROPE_THETA ** (-jnp.arange(0, dim, 2, dtype=F32) / dim)
    ang = positions.astype(F32).reshape(-1)[:, None] * inv_freq
    c, s = jnp.cos(ang), jnp.sin(ang)
    reps = pad_to // dim
    cos_t = jnp.tile(jnp.concatenate([c, c], axis=-1), (1, reps))
    sin_t = jnp.tile(jnp.concatenate([-s, s], axis=-1), (1, reps))
    t = c.shape[0]
    cos_t = jnp.concatenate([cos_t, jnp.ones((t, LANES - pad_to), F32)], axis=-1)
    sin_t = jnp.concatenate([sin_t, jnp.zeros((t, LANES - pad_to), F32)], axis=-1)
    return cos_t, sin_t


V_ONES_ROWS = 16
FLASH_TKB = 512
LOG2E = math.log2(math.e)


def _flash_kernel(*refs, tq, tkb, rep, dv, has_bias):
    if has_bias:
        q_ref, k_ref, vt_ref, b_ref, o_ref, m_ref, acc_ref, sa, sb, bmax_a, bmax_b = refs
    else:
        q_ref, k_ref, vt_ref, o_ref, m_ref, acc_ref, sa, sb, bmax_a, bmax_b = refs
        b_ref = None
    sa_ref, sb_ref = (sa, bmax_a), (sb, bmax_b)
    i = pl.program_id(2)
    rows = rep * tq

    m_ref[...] = jnp.full(m_ref.shape, MASK_VALUE, F32)
    acc_ref[...] = jnp.zeros(acc_ref.shape, F32)
    q = q_ref[0, 0, 0]

    def qk(j, slot):
        dst_ref, bmax_ref = slot
        start = pl.multiple_of(j * tkb, tkb)
        kb = k_ref[0, 0, pl.ds(start, tkb), :]
        s = lax.dot_general(kb, q, (((1,), (1,)), ((), ())),
                            preferred_element_type=F32)
        if has_bias:
            bias = b_ref[0, pl.ds(start, tkb), :].astype(F32)
            s = s + jnp.tile(bias, (1, rep))
        dst_ref[...] = s
        bmax_ref[...] = jnp.max(s, axis=0, keepdims=True)

    def softmax_pv(j, slot, masked):
        src_ref, bmax_ref = slot
        start = pl.multiple_of(j * tkb, tkb)
        vtb = vt_ref[0, 0, :, pl.ds(start, tkb)]
        s = src_ref[...]
        if masked:
            kpos = start + lax.broadcasted_iota(jnp.int32, (tkb, rows), 0)
            qlane = lax.broadcasted_iota(jnp.int32, (tkb, rows), 1)
            qpos = i * tq + (qlane % tq if rep > 1 else qlane)
            s = jnp.where(kpos <= qpos, s, MASK_VALUE)
            m_blk = jnp.max(s, axis=0, keepdims=True)
        else:
            m_blk = bmax_ref[...]
        m_prev = m_ref[...]
        m_new = jnp.maximum(m_prev, m_blk)
        p = jnp.exp2(s - m_new)
        alpha = jnp.exp2(m_prev - m_new)
        acc_ref[...] = alpha * acc_ref[...] + jnp.dot(vtb, p.astype(BF16), preferred_element_type=F32)
        m_ref[...] = m_new

    if has_bias:
        qk(0, sa_ref)
        n_blocks = ((i + 1) * tq + tkb - 1) // tkb

        def pair(jj, c):
            j0 = 2 * jj
            qk(j0 + 1, sb_ref)
            softmax_pv(j0, sa_ref, False)
            qk(jnp.minimum(j0 + 2, n_blocks - 1), sa_ref)
            softmax_pv(j0 + 1, sb_ref, False)
            return c

        lax.fori_loop(0, n_blocks // 2, pair, 0)

        @pl.when(n_blocks % 2 == 1)
        def _():
            softmax_pv(n_blocks - 1, sa_ref, False)
    else:
        qk(2 * i, sa_ref)
        qk(2 * i + 1, sb_ref)
        softmax_pv(2 * i, sa_ref, True)
        qk(0, sa_ref)
        softmax_pv(2 * i + 1, sb_ref, True)

        def pair(jj, c):
            j0 = 2 * jj
            qk(j0 + 1, sb_ref)
            softmax_pv(j0, sa_ref, False)
            qk(jnp.minimum(j0 + 2, 2 * i - 1), sa_ref)
            softmax_pv(j0 + 1, sb_ref, False)
            return c

        lax.fori_loop(0, i, pair, 0)

    acc = acc_ref[...]
    out = (acc[0:dv] * (1.0 / acc[dv:dv + 1])).T
    for r in range(rep):
        o_ref[0, :, r * dv:(r + 1) * dv] = out[r * tq:(r + 1) * tq].astype(o_ref.dtype)


def flash_attention(q, k, vt, bias, *, tq, rep, out_dtype):
    b, hk, nq, rows, dq = q.shape
    s = k.shape[2]
    dve = vt.shape[2]
    dv = dve - V_ONES_ROWS
    tkb = _tile(s, FLASH_TKB)
    has_bias = bias is not None
    assert has_bias or tq == 2 * tkb
    in_specs = [pl.BlockSpec((1, 1, 1, rows, dq), lambda bb, h, i: (bb, h, i, 0, 0)),
                pl.BlockSpec((1, 1, s, dq), lambda bb, h, i: (bb, h, 0, 0)),
                pl.BlockSpec((1, 1, dve, s), lambda bb, h, i: (bb, h, 0, 0))]
    args = [q, k, vt]
    if has_bias:
        in_specs.append(pl.BlockSpec((1, s, tq), lambda bb, h, i: (bb, 0, i)))
        args.append(bias)
    return pl.pallas_call(
        functools.partial(_flash_kernel, tq=tq, tkb=tkb, rep=rep, dv=dv, has_bias=has_bias),
        grid=(b, hk, nq),
        in_specs=in_specs,
        out_specs=pl.BlockSpec((1, tq, rep * dv), lambda bb, h, i: (bb, i, h)),
        out_shape=jax.ShapeDtypeStruct((b, s, hk * rep * dv), out_dtype),
        scratch_shapes=[pltpu.VMEM((1, rows), F32), pltpu.VMEM((dve, rows), F32),
                        pltpu.VMEM((tkb, rows), F32), pltpu.VMEM((tkb, rows), F32),
                        pltpu.VMEM((1, rows), F32), pltpu.VMEM((1, rows), F32)],
        compiler_params=_params(("parallel", "parallel", "parallel")),
        name="flash_attention",
    )(*args)


S5_NSEG = 2 * SUBLANES
S5_CBLK = LANES
S5_LT = 16


def _s5_kernel(u_ref, bbd_ref, cbd_ref, a_ref, ap_ref, d_ref, y_ref,
               ug_ref, bu_ref, xs_ref, init_ref, up_ref, yp_ref, *, seg_len, lt, pitch):
    nseg = S5_NSEG
    ns = a_ref.shape[-1]
    a_re = jnp.broadcast_to(a_ref[0, 0:1, :], (nseg, ns))
    a_im = jnp.broadcast_to(a_ref[0, 1:2, :], (nseg, ns))
    ntiles = seg_len // lt
    for k in range(nseg):
        up_ref[k * pitch:k * pitch + seg_len, :] = u_ref[0, k * seg_len:(k + 1) * seg_len, :]

    def load_inputs(t, slot):
        for i in range(lt):
            ug_ref[slot, i * nseg:(i + 1) * nseg, :] = up_ref[pl.ds(t * lt + i, nseg, stride=pitch), :]
        bu_ref[slot] = jnp.dot(ug_ref[slot].astype(BF16), bbd_ref[0], preferred_element_type=F32)

    def scan_tile(carry, slot, store):
        x_re, x_im = carry
        for i in range(lt):
            r0 = i * nseg
            b_re = bu_ref[slot, r0:r0 + nseg, 0:ns]
            b_im = bu_ref[slot, r0:r0 + nseg, ns:2 * ns]
            n_re = a_re * x_re - a_im * x_im + b_re
            n_im = a_re * x_im + a_im * x_re + b_im
            x_re, x_im = n_re, n_im
            if store:
                xs_ref[slot, r0:r0 + nseg, 0:ns] = x_re
                xs_ref[slot, r0:r0 + nseg, ns:2 * ns] = x_im
        return x_re, x_im

    def emit(t, slot):
        y = jnp.dot(xs_ref[slot].astype(BF16), cbd_ref[0], preferred_element_type=F32)
        y = jax.nn.gelu(y + d_ref[0] * ug_ref[slot])
        for i in range(lt):
            yp_ref[pl.ds(t * lt + i, nseg, stride=pitch), :] = y[i * nseg:(i + 1) * nseg]

    def sweep(carry, store):
        load_inputs(0, 0)

        def pair(tt, c):
            t0 = 2 * tt
            load_inputs(t0 + 1, 1)
            c = scan_tile(c, 0, store)
            if store:
                emit(t0, 0)
            load_inputs(jnp.minimum(t0 + 2, ntiles - 1), 0)
            c = scan_tile(c, 1, store)
            if store:
                emit(t0 + 1, 1)
            return c

        return lax.fori_loop(0, ntiles // 2, pair, carry)

    zeros = jnp.zeros((nseg, ns), F32)
    e_re, e_im = sweep((zeros, zeros), False)

    ap_re = ap_ref[0, 0:1, :]
    ap_im = ap_ref[0, 1:2, :]
    t_re = jnp.zeros((1, ns), F32)
    t_im = jnp.zeros((1, ns), F32)
    init_ref[0:1, :] = jnp.zeros((1, 2 * ns), F32)
    for s in range(1, nseg):
        p_re, p_im = e_re[s - 1:s], e_im[s - 1:s]
        t_re, t_im = (ap_re * t_re - ap_im * t_im + p_re,
                      ap_re * t_im + ap_im * t_re + p_im)
        init_ref[s:s + 1, 0:ns] = t_re
        init_ref[s:s + 1, ns:2 * ns] = t_im

    sweep((init_ref[:, 0:ns], init_ref[:, ns:2 * ns]), True)
    for k in range(nseg):
        y_ref[0, k * seg_len:(k + 1) * seg_len, :] = yp_ref[k * pitch:k * pitch + seg_len, :]


def _s5_discretise(lam_re, lam_im, log_step, b_re, b_im, c_re, c_im, seg_len):
    g, n = lam_re.shape
    p = S5_GROUP
    lr = jnp.minimum(lam_re.astype(F32), S5_LAMBDA_RE_MAX)
    li = lam_im.astype(F32)
    step = jnp.exp(log_step.astype(F32))[:, None]
    mag = jnp.exp(lr * step)
    ab_re = mag * jnp.cos(li * step)
    ab_im = mag * jnp.sin(li * step)
    den = lr * lr + li * li
    f_re = ((ab_re - 1.0) * lr + ab_im * li) / den
    f_im = (ab_im * lr - (ab_re - 1.0) * li) / den
    br = b_re.astype(F32)
    bi = b_im.astype(F32)
    bb_re = f_re[..., None] * br - f_im[..., None] * bi
    bb_im = f_re[..., None] * bi + f_im[..., None] * br
    pw_re, pw_im = jnp.ones_like(ab_re), jnp.zeros_like(ab_im)
    sq_re, sq_im = ab_re, ab_im
    e = seg_len
    while e:
        if e & 1:
            pw_re, pw_im = pw_re * sq_re - pw_im * sq_im, pw_re * sq_im + pw_im * sq_re
        sq_re, sq_im = sq_re * sq_re - sq_im * sq_im, 2.0 * sq_re * sq_im
        e >>= 1
    gb = S5_CBLK // p
    nb = g // gb
    eye = jnp.eye(gb, dtype=F32)

    def blockdiag_in(bb):
        x = bb.reshape(nb, gb, n, p)
        return jnp.einsum("bgnp,gh->bgphn", x, eye).reshape(nb, gb * p, gb * n)

    def blockdiag_out(c):
        x = c.astype(F32).reshape(nb, gb, p, n)
        return jnp.einsum("bgpn,gh->bgnhp", x, eye).reshape(nb, gb * n, gb * p)

    bbd = jnp.concatenate([blockdiag_in(bb_re), blockdiag_in(bb_im)], axis=2).astype(BF16)
    cbd = jnp.concatenate([blockdiag_out(c_re), -blockdiag_out(c_im)], axis=1).astype(BF16)
    a = jnp.stack([ab_re.reshape(nb, gb * n), ab_im.reshape(nb, gb * n)], axis=1)
    ap = jnp.stack([pw_re.reshape(nb, gb * n), pw_im.reshape(nb, gb * n)], axis=1)
    return bbd, cbd, a, ap


def s5_scan(u, lam_re, lam_im, log_step, b_re, b_im, c_re, c_im, d_skip):
    b, s, w = u.shape
    seg_len = s // S5_NSEG
    lt = min(S5_LT, seg_len)
    assert (seg_len // lt) % 2 == 0
    bbd, cbd, a, ap = _s5_discretise(lam_re, lam_im, log_step, b_re, b_im, c_re, c_im, seg_len)
    nb, cw, ns2 = bbd.shape
    ns = ns2 // 2
    rows = lt * S5_NSEG
    pitch = seg_len + SUBLANES if (seg_len // SUBLANES) % 2 == 0 else seg_len
    return pl.pallas_call(
        functools.partial(_s5_kernel, seg_len=seg_len, lt=lt, pitch=pitch),
        grid=(b, nb),
        in_specs=[pl.BlockSpec((1, s, cw), lambda bb, j: (bb, 0, j)),
                  pl.BlockSpec((1, cw, ns2), lambda bb, j: (j, 0, 0)),
                  pl.BlockSpec((1, ns2, cw), lambda bb, j: (j, 0, 0)),
                  pl.BlockSpec((1, 2, ns), lambda bb, j: (j, 0, 0)),
                  pl.BlockSpec((1, 2, ns), lambda bb, j: (j, 0, 0)),
                  pl.BlockSpec((1, 1, cw), lambda bb, j: (j, 0, 0))],
        out_specs=pl.BlockSpec((1, s, cw), lambda bb, j: (bb, 0, j)),
        out_shape=jax.ShapeDtypeStruct((b, s, w), F32),
        scratch_shapes=[pltpu.VMEM((2, rows, cw), F32), pltpu.VMEM((2, rows, ns2), F32),
                        pltpu.VMEM((2, rows, ns2), F32), pltpu.VMEM((S5_NSEG, ns2), F32),
                        pltpu.VMEM((S5_NSEG * pitch, cw), F32), pltpu.VMEM((S5_NSEG * pitch, cw), F32)],
        compiler_params=_params(("parallel", "parallel")),
        name="s5_scan",
    )(u, bbd, cbd, a, ap, d_skip.astype(F32).reshape(nb, 1, cw))


def _even_lat_kernel(lat_ref, gq_ref, gkv_ref, cos_ref, sin_ref, qn_ref, kvn_ref, kr_ref, *, q_rank, kv_rank):
    lat = lat_ref[...]
    q_lat = lat[:, 0:q_rank]
    kv_lat = lat[:, q_rank:q_rank + kv_rank]
    kr = lat[:, q_rank + kv_rank:q_rank + kv_rank + LANES]
    rq = lax.rsqrt(jnp.mean(q_lat * q_lat, axis=-1, keepdims=True) + EPS)
    qn_ref[...] = (q_lat * rq * gq_ref[...]).astype(qn_ref.dtype)
    rkv = lax.rsqrt(jnp.mean(kv_lat * kv_lat, axis=-1, keepdims=True) + EPS)
    kvn_ref[...] = (kv_lat * rkv * gkv_ref[...]).astype(kvn_ref.dtype)
    kr = _rope_tile(kr, cos_ref[...], sin_ref[...], MLA_ROPE // 2)
    lane = lax.broadcasted_iota(jnp.int32, kr.shape, 1)
    lo = jnp.where(lane < MLA_ROPE, kr, 0.0)
    hi = pltpu.roll(lo, MLA_ROPE, 1)
    kr_ref[:, 0:LANES] = lo.astype(kr_ref.dtype)
    kr_ref[:, LANES:2 * LANES] = hi.astype(kr_ref.dtype)


def even_lat_prep(lat, gq, gkv, cos_t, sin_t):
    m, wl = lat.shape
    q_rank, kv_rank = gq.shape[0], gkv.shape[0]
    tm = _tile(m, 512, SUBLANES)
    return pl.pallas_call(
        functools.partial(_even_lat_kernel, q_rank=q_rank, kv_rank=kv_rank),
        grid=(m // tm,),
        in_specs=[pl.BlockSpec((tm, wl), lambda i: (i, 0)),
                  pl.BlockSpec((1, q_rank), lambda i: (0, 0)),
                  pl.BlockSpec((1, kv_rank), lambda i: (0, 0)),
                  pl.BlockSpec((tm, LANES), lambda i: (i, 0)),
                  pl.BlockSpec((tm, LANES), lambda i: (i, 0))],
        out_specs=[pl.BlockSpec((tm, q_rank), lambda i: (i, 0)),
                   pl.BlockSpec((tm, kv_rank), lambda i: (i, 0)),
                   pl.BlockSpec((tm, 2 * LANES), lambda i: (i, 0))],
        out_shape=[jax.ShapeDtypeStruct((m, q_rank), BF16),
                   jax.ShapeDtypeStruct((m, kv_rank), BF16),
                   jax.ShapeDtypeStruct((m, 2 * LANES), BF16)],
        compiler_params=_params(("parallel",)),
        name="even_lat_prep",
    )(lat, gq.reshape(1, -1).astype(F32), gkv.reshape(1, -1).astype(F32), cos_t, sin_t)


def _mla_q_kernel(qn_ref, w_ref, cos_ref, sin_ref, qf_ref, *, heads, scale):
    q = jnp.dot(qn_ref[...], w_ref[...], preferred_element_type=F32)
    cos_t, sin_t = cos_ref[...], sin_ref[...]
    nope_w = heads * MLA_NOPE
    lane = lax.broadcasted_iota(jnp.int32, (q.shape[0], LANES), 1)
    for pair in range(heads // 2):
        rope = _rope_tile(q[:, nope_w + pair * LANES:nope_w + (pair + 1) * LANES], cos_t, sin_t, MLA_ROPE // 2)
        for sub in range(2):
            h = 2 * pair + sub
            keep = (lane < MLA_ROPE) if sub == 0 else (lane >= MLA_ROPE)
            qf_ref[0, h, :, 0:LANES] = (q[:, h * MLA_NOPE:(h + 1) * MLA_NOPE] * scale).astype(qf_ref.dtype)
            qf_ref[0, h, :, LANES:2 * LANES] = (jnp.where(keep, rope, 0.0) * scale).astype(qf_ref.dtype)


def _mla_kv_kernel(kvn_ref, w_ref, kr_ref, kf_ref, vt_ref, *, heads):
    kv = jnp.dot(kvn_ref[...], w_ref[...], preferred_element_type=F32)
    for h in range(heads):
        c0 = h * (MLA_NOPE + MLA_V)
        kf_ref[0, h, :, 0:LANES] = kv[:, c0:c0 + MLA_NOPE].astype(kf_ref.dtype)
        kf_ref[0, h, :, LANES:2 * LANES] = kr_ref[:, (h % 2) * LANES:(h % 2 + 1) * LANES]
        vt_ref[0, h, 0:MLA_V, :] = kv[:, c0 + MLA_NOPE:c0 + MLA_NOPE + MLA_V].T.astype(vt_ref.dtype)
        vt_ref[0, h, MLA_V:MLA_V + V_ONES_ROWS, :] = jnp.ones((V_ONES_ROWS, kv.shape[0]), vt_ref.dtype)


def mla_projections(qn, kvn, kr2, wq, wkv, cos_t, sin_t, b, s, heads):
    t = qn.shape[0]
    tm = _tile(s, 256, SUBLANES)
    nt = s // tm
    scale = (MLA_NOPE + MLA_ROPE) ** -0.5 * LOG2E
    head_spec = pl.BlockSpec((1, heads, tm, 2 * LANES), lambda i: (i // nt, 0, i % nt, 0))
    qf = pl.pallas_call(
        functools.partial(_mla_q_kernel, heads=heads, scale=scale),
        grid=(t // tm,),
        in_specs=[pl.BlockSpec((tm, qn.shape[1]), lambda i: (i, 0)),
                  pl.BlockSpec(wq.shape, lambda i: (0, 0)),
                  pl.BlockSpec((tm, LANES), lambda i: (i, 0)),
                  pl.BlockSpec((tm, LANES), lambda i: (i, 0))],
        out_specs=head_spec,
        out_shape=jax.ShapeDtypeStruct((b, heads, s, 2 * LANES), BF16),
        compiler_params=_params(("parallel",)),
        name="mla_q_proj",
    )(qn, wq, cos_t, sin_t)
    kf, vt = pl.pallas_call(
        functools.partial(_mla_kv_kernel, heads=heads),
        grid=(t // tm,),
        in_specs=[pl.BlockSpec((tm, kvn.shape[1]), lambda i: (i, 0)),
                  pl.BlockSpec(wkv.shape, lambda i: (0, 0)),
                  pl.BlockSpec((tm, 2 * LANES), lambda i: (i, 0))],
        out_specs=[head_spec,
                   pl.BlockSpec((1, heads, MLA_V + V_ONES_ROWS, tm), lambda i: (i // nt, 0, 0, i % nt))],
        out_shape=[jax.ShapeDtypeStruct((b, heads, s, 2 * LANES), BF16),
                   jax.ShapeDtypeStruct((b, heads, MLA_V + V_ONES_ROWS, s), BF16)],
        compiler_params=_params(("parallel",)),
        name="mla_kv_proj",
    )(kvn, wkv, kr2)
    return qf, kf, vt


def even_mixer(x2, h, b, s, cos_m, sin_m, w_in, lam_re, lam_im, log_step, b_re, b_im, c_re, c_im,
               d_skip, glu_a, glu_b, q_norm_g, w_q_up, kv_norm_g, w_kv_up, w_out):
    t, d = h.shape
    s5_w = d_skip.shape[0]
    q_rank, kv_rank = q_norm_g.shape[0], kv_norm_g.shape[0]
    heads = w_kv_up.shape[1] // (MLA_NOPE + MLA_V)
    lat_w = q_rank + kv_rank + MLA_ROPE
    lat_pad = -(-(q_rank + kv_rank + LANES) // (2 * LANES)) * (2 * LANES)

    w_u = w_in[:, :s5_w].astype(BF16)
    w_lat = jnp.pad(w_in[:, s5_w:], ((0, 0), (0, lat_pad - lat_w))).astype(BF16)
    u = matmul(h, w_u, F32)
    lat = matmul(h, w_lat, F32)

    y = s5_scan(u.reshape(b, s, s5_w), lam_re, lam_im, log_step, b_re, b_im, c_re, c_im, d_skip)
    s5_out = glu_matmul(y.reshape(t, s5_w), glu_a.astype(BF16), glu_b.astype(BF16), BF16)

    qn, kvn, kr2 = even_lat_prep(lat, q_norm_g, kv_norm_g, cos_m, sin_m)
    wq = w_q_up.reshape(q_rank, heads, MLA_NOPE + MLA_ROPE)
    wq = jnp.concatenate([wq[:, :, :MLA_NOPE].reshape(q_rank, heads * MLA_NOPE),
                          wq[:, :, MLA_NOPE:].reshape(q_rank, heads * MLA_ROPE)], axis=1).astype(BF16)
    qf, kf, vf = mla_projections(qn, kvn, kr2, wq, w_kv_up.astype(BF16), cos_m, sin_m, b, s, heads)
    tq = 2 * _tile(s, FLASH_TKB)
    mla_out = flash_attention(qf.reshape(b, heads, s // tq, tq, 2 * LANES), kf, vf, None,
                              tq=tq, rep=1, out_dtype=BF16)
    mixed = jnp.concatenate([s5_out, mla_out.reshape(t, -1)], axis=-1)
    return matmul(mixed, w_out.astype(BF16), F32, kind="residual", res=x2)


def _odd_lat_kernel(z_ref, gq_ref, lng_ref, lnb_ref, cos_ref, sin_ref,
                    qn_ref, k_ref, v_ref, kidx_ref, w_ref, *, q_rank, w_scale):
    cos_t, sin_t = cos_ref[...], sin_ref[...]
    half = ROT_DIM // 2
    z = z_ref[0]
    q_lat = z[:, 0:q_rank]
    rq = lax.rsqrt(jnp.mean(q_lat * q_lat, axis=-1, keepdims=True) + EPS)
    qn_ref[0] = (q_lat * rq * gq_ref[...]).astype(qn_ref.dtype)
    kvh = DSA_KV_HEADS
    d = DSA_HEAD_DIM
    for g in range(kvh):
        kh = z[:, q_rank + g * d:q_rank + (g + 1) * d]
        k_ref[0, g] = _rope_tile(kh, cos_t, sin_t, half).astype(k_ref.dtype)
        v_ref[0, g, 0:d, :] = z[:, q_rank + (kvh + g) * d:q_rank + (kvh + g + 1) * d].T.astype(v_ref.dtype)
        v_ref[0, g, d:d + V_ONES_ROWS, :] = jnp.ones((V_ONES_ROWS, z.shape[0]), v_ref.dtype)
    off = q_rank + 2 * kvh * d
    ki = z[:, off:off + IDX_DIM]
    kc = ki - jnp.mean(ki, axis=-1, keepdims=True)
    var = jnp.mean(kc * kc, axis=-1, keepdims=True)
    ki = kc * lax.rsqrt(var + EPS) * lng_ref[...] + lnb_ref[...]
    kidx_ref[0] = _rope_tile(ki, cos_t, sin_t, half).astype(kidx_ref.dtype)
    w_ref[0] = (z[:, off + IDX_DIM:off + IDX_DIM + LANES] * w_scale).T


def odd_lat_prep(z, gq, ln_g, ln_b, cos_t, sin_t, idx_heads):
    b, s, zw = z.shape
    q_rank = gq.shape[0]
    tm = _tile(s, 512, SUBLANES)
    nt = s // tm
    w_scale = idx_heads ** -0.5 * IDX_DIM ** -0.5
    kvh, d = DSA_KV_HEADS, DSA_HEAD_DIM
    return pl.pallas_call(
        functools.partial(_odd_lat_kernel, q_rank=q_rank, w_scale=w_scale),
        grid=(b, nt),
        in_specs=[pl.BlockSpec((1, tm, zw), lambda bb, i: (bb, i, 0)),
                  pl.BlockSpec((1, q_rank), lambda bb, i: (0, 0)),
                  pl.BlockSpec((1, IDX_DIM), lambda bb, i: (0, 0)),
                  pl.BlockSpec((1, IDX_DIM), lambda bb, i: (0, 0)),
                  pl.BlockSpec((tm, LANES), lambda bb, i: (bb * nt + i, 0)),
                  pl.BlockSpec((tm, LANES), lambda bb, i: (bb * nt + i, 0))],
        out_specs=[pl.BlockSpec((1, tm, q_rank), lambda bb, i: (bb, i, 0)),
                   pl.BlockSpec((1, kvh, tm, d), lambda bb, i: (bb, 0, i, 0)),
                   pl.BlockSpec((1, kvh, d + V_ONES_ROWS, tm), lambda bb, i: (bb, 0, 0, i)),
                   pl.BlockSpec((1, tm, IDX_DIM), lambda bb, i: (bb, i, 0)),
                   pl.BlockSpec((1, LANES, tm), lambda bb, i: (bb, 0, i))],
        out_shape=[jax.ShapeDtypeStruct((b, s, q_rank), BF16),
                   jax.ShapeDtypeStruct((b, kvh, s, d), BF16),
                   jax.ShapeDtypeStruct((b, kvh, d + V_ONES_ROWS, s), BF16),
                   jax.ShapeDtypeStruct((b, s, IDX_DIM), BF16),
                   jax.ShapeDtypeStruct((b, LANES, s), F32)],
        compiler_params=_params(("parallel", "parallel")),
        name="odd_lat_prep",
    )(z, gq.reshape(1, -1).astype(F32), ln_g.reshape(1, -1).astype(F32),
      ln_b.reshape(1, -1).astype(F32), cos_t, sin_t)


def _mm_rope_kernel(a_ref, w_ref, cos_ref, sin_ref, o_ref, *, n_heads, scale, stack_rows):
    acc = jnp.dot(a_ref[...], w_ref[...], preferred_element_type=F32)
    cos_t, sin_t = cos_ref[...], sin_ref[...]
    d = DSA_HEAD_DIM
    for r in range(n_heads):
        xh = _rope_tile(acc[:, r * d:(r + 1) * d], cos_t, sin_t, ROT_DIM // 2) * scale
        if stack_rows:
            o_ref[0, 0, 0, r * stack_rows:(r + 1) * stack_rows, :] = xh.astype(o_ref.dtype)
        else:
            o_ref[:, r * d:(r + 1) * d] = xh.astype(o_ref.dtype)


def dsa_q_proj(qn, w_q, cos_t, sin_t, b, s, tq):
    t, kq = qn.shape
    d = DSA_HEAD_DIM
    heads = w_q.shape[1] // d
    rep = heads // DSA_KV_HEADS
    nt = s // tq
    return pl.pallas_call(
        functools.partial(_mm_rope_kernel, n_heads=rep, scale=d ** -0.5 * LOG2E, stack_rows=tq),
        grid=(t // tq, DSA_KV_HEADS),
        in_specs=[pl.BlockSpec((tq, kq), lambda i, g: (i, 0)),
                  pl.BlockSpec((kq, rep * d), lambda i, g: (0, g)),
                  pl.BlockSpec((tq, LANES), lambda i, g: (i, 0)),
                  pl.BlockSpec((tq, LANES), lambda i, g: (i, 0))],
        out_specs=pl.BlockSpec((1, 1, 1, rep * tq, d), lambda i, g: (i // nt, g, i % nt, 0, 0)),
        out_shape=jax.ShapeDtypeStruct((b, DSA_KV_HEADS, nt, rep * tq, d), BF16),
        compiler_params=_params(("parallel", "parallel")),
        name="dsa_q_proj",
    )(qn, w_q, cos_t, sin_t)


def idx_q_proj(qn, w_qi, cos_t, sin_t):
    t, kq = qn.shape
    n = w_qi.shape[1]
    tm = _tile(t, 1024, SUBLANES)
    tn = _tile(n, 512)
    return pl.pallas_call(
        functools.partial(_mm_rope_kernel, n_heads=tn // IDX_DIM, scale=1.0, stack_rows=0),
        grid=(t // tm, n // tn),
        in_specs=[pl.BlockSpec((tm, kq), lambda i, j: (i, 0)),
                  pl.BlockSpec((kq, tn), lambda i, j: (0, j)),
                  pl.BlockSpec((tm, LANES), lambda i, j: (i, 0)),
                  pl.BlockSpec((tm, LANES), lambda i, j: (i, 0))],
        out_specs=pl.BlockSpec((tm, tn), lambda i, j: (i, j)),
        out_shape=jax.ShapeDtypeStruct((t, n), BF16),
        compiler_params=_params(("parallel", "parallel")),
        name="idx_q_proj",
    )(qn, w_qi, cos_t, sin_t)


def _ukey_to_float(u):
    bits = jnp.where(u < 0, u & jnp.int32(0x7FFFFFFF), ~u)
    return lax.bitcast_convert_type(bits, F32)


def _threshold_of_key(u):
    thr = _ukey_to_float(u)
    return jnp.where((thr != thr) & (u >= 0), jnp.float32(-jnp.inf), thr)


def _indexer_kernel(qi_ref, wt_ref, kidx_ref, bias_ref, score_ref, lim_ref, *, tq, tkb, idx_heads, top_k):
    i = pl.program_id(1)
    s_len = kidx_ref.shape[1]
    n_blocks = ((i + 1) * tq + tkb - 1) // tkb
    neg_inf = jnp.float32(-jnp.inf)
    qpos = i * tq + lax.broadcasted_iota(jnp.int32, (1, tq), 1)

    def causal(start):
        kpos = start + lax.broadcasted_iota(jnp.int32, (tkb, tq), 0)
        return kpos <= qpos

    def score_block(j, c):
        start = pl.multiple_of(j * tkb, tkb)
        kb = kidx_ref[0, pl.ds(start, tkb), :]
        acc = jnp.zeros((tkb, tq), F32)
        for h in range(idx_heads):
            logits = lax.dot_general(kb, qi_ref[0, :, h * IDX_DIM:(h + 1) * IDX_DIM],
                                     (((1,), (1,)), ((), ())), preferred_element_type=F32)
            acc = acc + jnp.maximum(logits, 0.0) * wt_ref[0, h:h + 1, :]
        score_ref[pl.ds(start, tkb), :] = jnp.where(causal(start), acc, neg_inf)
        return c

    lax.fori_loop(0, n_blocks, score_block, 0)

    def count_where(pred):
        def body(j, cnt):
            start = pl.multiple_of(j * tkb, tkb)
            ind = jnp.where(pred(score_ref[pl.ds(start, tkb), :], start), 1.0, 0.0)
            part = jnp.sum(ind.reshape(tkb // (8 * SUBLANES), 8, SUBLANES, tq), axis=1)
            return cnt + jnp.sum(part, axis=0)
        cnt = lax.fori_loop(0, n_blocks, body, jnp.zeros((SUBLANES, tq), F32))
        return jnp.sum(cnt, axis=0, keepdims=True)

    def count_ge(thr):
        return count_where(lambda blk, start: blk >= thr)

    k_f = jnp.float32(top_k)
    few = qpos < top_k

    def search_cond(st):
        bi, _, cnt_u = st
        pending = jnp.where(few | (cnt_u == k_f), 0.0, 1.0)
        return (bi < 32) & (jnp.max(pending) > 0.0)

    def search_step(st):
        bi, u, cnt_u = st
        cand = u | (jnp.int32(1) << (31 - bi))
        cnt = count_ge(_threshold_of_key(cand))
        take = cnt >= k_f
        return bi + 1, jnp.where(take, cand, u), jnp.where(take, cnt, cnt_u)

    total = (n_blocks * tkb).astype(F32)
    _, u, cnt_u = lax.while_loop(search_cond, search_step,
                                 (jnp.int32(0), jnp.zeros((1, tq), jnp.int32), jnp.full((1, tq), total, F32)))
    thr = jnp.where(few, neg_inf, _threshold_of_key(u))

    def key_pos(start):
        return start + lax.broadcasted_iota(jnp.int32, (tkb, tq), 0)

    excess = jnp.logical_not(few) & (cnt_u > k_f)
    lim_ref[...] = jnp.full((1, tq), s_len, jnp.int32)

    @pl.when(jnp.max(jnp.where(excess, 1.0, 0.0)) > 0.0)
    def _():
        need = k_f - count_where(lambda blk, start: blk > thr)
        nbits = s_len.bit_length()

        def tie_step(bi, p):
            cand = p | (jnp.int32(1) << (nbits - 1 - bi))
            cnt = count_where(lambda blk, start: (blk == thr) & (key_pos(start) < cand))
            return jnp.where(cnt <= need, cand, p)

        p = lax.fori_loop(0, nbits, tie_step, jnp.zeros((1, tq), jnp.int32))
        lim_ref[...] = jnp.where(excess, p, s_len)

    lim = lim_ref[...]

    def write_block(j, c):
        start = pl.multiple_of(j * tkb, tkb)
        blk = score_ref[pl.ds(start, tkb), :]
        keep = ((blk > thr) | ((blk == thr) & (key_pos(start) < lim))) & causal(start)
        bias_ref[0, pl.ds(start, tkb), :] = jnp.where(keep, 0.0, MASK_VALUE).astype(bias_ref.dtype)
        return c

    lax.fori_loop(0, n_blocks, write_block, 0)

    def fill_block(j, c):
        start = pl.multiple_of(j * tkb, tkb)
        bias_ref[0, pl.ds(start, tkb), :] = jnp.full((tkb, tq), MASK_VALUE, bias_ref.dtype)
        return c

    lax.fori_loop(n_blocks, s_len // tkb, fill_block, 0)


IDX_TQ = 512


def dsa_indexer(qi, wt, kidx, idx_heads, top_k):
    b, s, _ = qi.shape
    tq = _tile(s, IDX_TQ)
    tkb = _tile(s, 512)
    assert tkb >= top_k or tkb == s
    return pl.pallas_call(
        functools.partial(_indexer_kernel, tq=tq, tkb=tkb, idx_heads=idx_heads, top_k=top_k),
        grid=(b, s // tq),
        in_specs=[pl.BlockSpec((1, tq, idx_heads * IDX_DIM), lambda bb, i: (bb, i, 0)),
                  pl.BlockSpec((1, LANES, tq), lambda bb, i: (bb, 0, i)),
                  pl.BlockSpec((1, s, IDX_DIM), lambda bb, i: (bb, 0, 0))],
        out_specs=pl.BlockSpec((1, s, tq), lambda bb, i: (bb, 0, i)),
        out_shape=jax.ShapeDtypeStruct((b, s, s), BF16),
        scratch_shapes=[pltpu.VMEM((s, tq), F32), pltpu.VMEM((1, tq), jnp.int32)],
        compiler_params=_params(("parallel", "parallel")),
        name="dsa_indexer",
    )(qi, wt, kidx)


def odd_mixer(x2, h, b, s, cos_p, sin_p, w_in, q_norm_g, w_q_up, w_idx_q, k_ln_g, k_ln_b, w_out):
    t, d = h.shape
    q_rank = q_norm_g.shape[0]
    heads = w_q_up.shape[1] // DSA_HEAD_DIM
    idx_heads = w_idx_q.shape[1] // IDX_DIM
    in_w = w_in.shape[1]
    z_w = -(-(in_w - idx_heads + LANES) // (2 * LANES)) * (2 * LANES)
    w_z = jnp.pad(w_in, ((0, 0), (0, z_w - in_w))).astype(BF16)
    z = matmul(h, w_z, F32)
    qn, k, vt, kidx, wt = odd_lat_prep(z.reshape(b, s, z_w), q_norm_g, k_ln_g, k_ln_b, cos_p, sin_p, idx_heads)
    tq = _tile(s, 256)
    qn2 = qn.reshape(t, q_rank)
    q = dsa_q_proj(qn2, w_q_up.astype(BF16), cos_p, sin_p, b, s, tq)
    qi = idx_q_proj(qn2, w_idx_q.astype(BF16), cos_p, sin_p).reshape(b, s, -1)
    top_k = min(IDX_TOPK_MAX, s // 4)
    bias = dsa_indexer(qi, wt, kidx, idx_heads, top_k)
    o = flash_attention(q, k, vt, bias, tq=tq, rep=heads // DSA_KV_HEADS, out_dtype=BF16)
    return matmul(o.reshape(t, -1), w_out.astype(BF16), F32, kind="residual", res=x2)


def sq_relu_mlp(x2, h, w_up, w_down, layer):
    a = matmul(h, w_up, BF16, kind="relu2", layer=layer)
    return matmul(a, w_down, F32, kind="residual", res=x2, layer=layer)


def kernel(x, positions, norm_mix_g, norm_mlp_g, final_norm_g, even_w_in, s5_lam_re, s5_lam_im, s5_log_step, s5_b_re, s5_b_im, s5_c_re, s5_c_im, s5_d, s5_glu_a, s5_glu_b, mla_q_norm_g, mla_w_q_up, mla_kv_norm_g, mla_w_kv_up, even_w_out, odd_w_in, dsa_q_norm_g, dsa_w_q_up, idx_w_q, idx_k_ln_g, idx_k_ln_b, odd_w_out, mlp_w_up, mlp_w_down):
    b, s, d = x.shape
    depth = norm_mix_g.shape[0]
    cos_m, sin_m = _rope_tables(positions, MLA_ROPE, LANES)
    cos_p, sin_p = _rope_tables(positions, ROT_DIM, ROT_DIM)
    x2 = x.reshape(b * s, d)
    for layer in range(depth):
        i = layer // 2
        h = rmsnorm(x2, norm_mix_g[layer], BF16)
        if layer % 2 == 0:
            x2 = even_mixer(x2, h, b, s, cos_m, sin_m, even_w_in[i], s5_lam_re[i], s5_lam_im[i],
                            s5_log_step[i], s5_b_re[i], s5_b_im[i], s5_c_re[i], s5_c_im[i], s5_d[i],
                            s5_glu_a[i], s5_glu_b[i], mla_q_norm_g[i], mla_w_q_up[i],
                            mla_kv_norm_g[i], mla_w_kv_up[i], even_w_out[i])
        else:
            x2 = odd_mixer(x2, h, b, s, cos_p, sin_p, odd_w_in[i], dsa_q_norm_g[i], dsa_w_q_up[i],
                           idx_w_q[i], idx_k_ln_g[i], idx_k_ln_b[i], odd_w_out[i])
        h = rmsnorm(x2, norm_mlp_g[layer], BF16)
        x2 = sq_relu_mlp(x2, h, mlp_w_up, mlp_w_down, layer)
    return rmsnorm(x2, final_norm_g, x.dtype).reshape(b, s, d)
```

```python
import functools
import math

import jax
import jax.numpy as jnp
from jax import lax
from jax.experimental import pallas as pl
from jax.experimental.pallas import tpu as pltpu

F32 = jnp.float32
BF16 = jnp.bfloat16

EPS = 1e-6
ROPE_THETA = 500000.0
S5_GROUP = 16
S5_LAMBDA_RE_MAX = -1e-4
MLA_NOPE = 128
MLA_ROPE = 64
MLA_V = 128
DSA_HEAD_DIM = 128
DSA_KV_HEADS = 4
IDX_DIM = 128
IDX_TOPK_MAX = 256
ROT_DIM = DSA_HEAD_DIM // 4

LANES = 128
SUBLANES = 8
VMEM_LIMIT_BYTES = 56 * 1024 * 1024
MASK_VALUE = -1e30


def _params(semantics):
    return pltpu.CompilerParams(dimension_semantics=semantics, vmem_limit_bytes=VMEM_LIMIT_BYTES)


def _tile(dim, pref, align=LANES):
    if dim <= pref:
        return dim
    t = (pref // align) * align
    while t >= align:
        if dim % t == 0:
            return t
        t -= align
    return dim


def _rmsnorm_kernel(x_ref, g_ref, o_ref):
    x = x_ref[...].astype(F32)
    r = lax.rsqrt(jnp.mean(x * x, axis=-1, keepdims=True) + EPS)
    o_ref[...] = (x * r * g_ref[...]).astype(o_ref.dtype)


def rmsnorm(x, g, out_dtype):
    m, d = x.shape
    tm = _tile(m, 512, SUBLANES)
    return pl.pallas_call(
        _rmsnorm_kernel,
        grid=(m // tm,),
        in_specs=[pl.BlockSpec((tm, d), lambda i: (i, 0)),
                  pl.BlockSpec((1, d), lambda i: (0, 0))],
        out_specs=pl.BlockSpec((tm, d), lambda i: (i, 0)),
        out_shape=jax.ShapeDtypeStruct((m, d), out_dtype),
        compiler_params=_params(("parallel",)),
        name="rmsnorm",
    )(x, g.reshape(1, d).astype(F32))


def _mm_epilogue(acc, kind, res_ref):
    if kind == "relu2":
        a = jnp.maximum(acc, 0.0)
        return a * a
    if kind == "residual":
        return res_ref[...].astype(F32) + acc
    return acc


def _mm_kernel_single(*refs, kind):
    if kind == "residual":
        a_ref, w_ref, res_ref, o_ref = refs
    else:
        a_ref, w_ref, o_ref = refs
        res_ref = None
    acc = jnp.dot(a_ref[...].astype(BF16), w_ref[...].astype(BF16), preferred_element_type=F32)
    o_ref[...] = _mm_epilogue(acc, kind, res_ref).astype(o_ref.dtype)


def _mm_kernel_ksplit(*refs, kind, nk):
    if kind == "residual":
        a_ref, w_ref, res_ref, o_ref, acc_ref = refs
    else:
        a_ref, w_ref, o_ref, acc_ref = refs
        res_ref = None
    k = pl.program_id(2)

    @pl.when(k == 0)
    def _():
        acc_ref[...] = jnp.zeros_like(acc_ref)

    acc_ref[...] += jnp.dot(a_ref[...].astype(BF16), w_ref[...].astype(BF16), preferred_element_type=F32)

    @pl.when(k == nk - 1)
    def _():
        o_ref[...] = _mm_epilogue(acc_ref[...], kind, res_ref).astype(o_ref.dtype)


def matmul(a, w, out_dtype, kind="none", res=None, layer=None):
    m, k = a.shape
    n = w.shape[-1]
    tm_pref, tn_pref, tk_pref = (1024, 512, k) if k <= 4096 else (1024, 1024, 2048)
    tm = _tile(m, tm_pref, SUBLANES)
    tn = _tile(n, tn_pref)
    tk = _tile(k, tk_pref)

    def w_spec(tk_, tn_, index):
        if layer is None:
            return pl.BlockSpec((tk_, tn_), index)
        return pl.BlockSpec((None, tk_, tn_), lambda *g: (layer,) + index(*g))

    if tk == k:
        in_specs = [pl.BlockSpec((tm, k), lambda i, j: (i, 0)),
                    w_spec(k, tn, lambda i, j: (0, j))]
        args = [a, w]
        if kind == "residual":
            in_specs.append(pl.BlockSpec((tm, tn), lambda i, j: (i, j)))
            args.append(res)
        return pl.pallas_call(
            functools.partial(_mm_kernel_single, kind=kind),
            grid=(m // tm, n // tn),
            in_specs=in_specs,
            out_specs=pl.BlockSpec((tm, tn), lambda i, j: (i, j)),
            out_shape=jax.ShapeDtypeStruct((m, n), out_dtype),
            compiler_params=_params(("parallel", "parallel")),
            name="matmul",
        )(*args)
    nk = k // tk
    in_specs = [pl.BlockSpec((tm, tk), lambda i, j, kk: (i, kk)),
                w_spec(tk, tn, lambda i, j, kk: (kk, j))]
    args = [a, w]
    if kind == "residual":
        in_specs.append(pl.BlockSpec((tm, tn), lambda i, j, kk: (i, j)))
        args.append(res)
    return pl.pallas_call(
        functools.partial(_mm_kernel_ksplit, kind=kind, nk=nk),
        grid=(m // tm, n // tn, nk),
        in_specs=in_specs,
        out_specs=pl.BlockSpec((tm, tn), lambda i, j, kk: (i, j)),
        out_shape=jax.ShapeDtypeStruct((m, n), out_dtype),
        scratch_shapes=[pltpu.VMEM((tm, tn), F32)],
        compiler_params=_params(("parallel", "parallel", "arbitrary")),
        name="matmul_ksplit",
    )(*args)


def _glu_kernel(a_ref, wa_ref, wb_ref, o_ref):
    a = a_ref[...].astype(BF16)
    ya = jnp.dot(a, wa_ref[...], preferred_element_type=F32)
    yb = jnp.dot(a, wb_ref[...], preferred_element_type=F32)
    o_ref[...] = (ya * jax.nn.sigmoid(yb)).astype(o_ref.dtype)


def glu_matmul(a, wa, wb, out_dtype):
    m, k = a.shape
    _, n = wa.shape
    tm = _tile(m, 1024, SUBLANES)
    tn = _tile(n, 512)
    return pl.pallas_call(
        _glu_kernel,
        grid=(m // tm, n // tn),
        in_specs=[pl.BlockSpec((tm, k), lambda i, j: (i, 0)),
                  pl.BlockSpec((k, tn), lambda i, j: (0, j)),
                  pl.BlockSpec((k, tn), lambda i, j: (0, j))],
        out_specs=pl.BlockSpec((tm, tn), lambda i, j: (i, j)),
        out_shape=jax.ShapeDtypeStruct((m, n), out_dtype),
        compiler_params=_params(("parallel", "parallel")),
        name="glu_matmul",
    )(a, wa, wb)


def _rope_tile(x, cos_t, sin_t, half):
    lane = lax.broadcasted_iota(jnp.int32, x.shape, 1)
    first = (lane % (2 * half)) < half
    partner = jnp.where(first, pltpu.roll(x, LANES - half, 1), pltpu.roll(x, half, 1))
    return x * cos_t + partner * sin_t


def _rope_tables(positions, dim, pad_to):
    inv_freq = ROPE_THETA ** (-jnp.arange(0, dim, 2, dtype=F32) / dim)
    ang = positions.astype(F32).reshape(-1)[:, None] * inv_freq
    c, s = jnp.cos(ang), jnp.sin(ang)
    reps = pad_to // dim
    cos_t = jnp.tile(jnp.concatenate([c, c], axis=-1), (1, reps))
    sin_t = jnp.tile(jnp.concatenate([-s, s], axis=-1), (1, reps))
    t = c.shape[0]
    cos_t = jnp.concatenate([cos_t, jnp.ones((t, LANES - pad_to), F32)], axis=-1)
    sin_t = jnp.concatenate([sin_t, jnp.zeros((t, LANES - pad_to), F32)], axis=-1)
    return cos_t, sin_t


V_ONES_ROWS = 16
FLASH_TKB = 512
LOG2E = math.log2(math.e)


def _flash_kernel(*refs, tq, tkb, rep, dv, has_bias):
    if has_bias:
        q_ref, k_ref, vt_ref, b_ref, o_ref, m_ref, acc_ref, sa, sb, bmax_a, bmax_b = refs
    else:
        q_ref, k_ref, vt_ref, o_ref, m_ref, acc_ref, sa, sb, bmax_a, bmax_b = refs
        b_ref = None
    sa_ref, sb_ref = (sa, bmax_a), (sb, bmax_b)
    i = pl.program_id(2)
    rows = rep * tq

    m_ref[...] = jnp.full(m_ref.shape, MASK_VALUE, F32)
    acc_ref[...] = jnp.zeros(acc_ref.shape, F32)
    q = q_ref[0, 0, 0]

    def qk(j, slot):
        dst_ref, bmax_ref = slot
        start = pl.multiple_of(j * tkb, tkb)
        kb = k_ref[0, 0, pl.ds(start, tkb), :]
        s = lax.dot_general(kb, q, (((1,), (1,)), ((), ())),
                            preferred_element_type=F32)
        if has_bias:
            bias = b_ref[0, pl.ds(start, tkb), :].astype(F32)
            s = s + jnp.tile(bias, (1, rep))
        dst_ref[...] = s
        bmax_ref[...] = jnp.max(s, axis=0, keepdims=True)

    def softmax_pv(j, slot, masked):
        src_ref, bmax_ref = slot
        start = pl.multiple_of(j * tkb, tkb)
        vtb = vt_ref[0, 0, :, pl.ds(start, tkb)]
        s = src_ref[...]
        if masked:
            kpos = start + lax.broadcasted_iota(jnp.int32, (tkb, rows), 0)
            qlane = lax.broadcasted_iota(jnp.int32, (tkb, rows), 1)
            qpos = i * tq + (qlane % tq if rep > 1 else qlane)
            s = jnp.where(kpos <= qpos, s, MASK_VALUE)
            m_blk = jnp.max(s, axis=0, keepdims=True)
        else:
            m_blk = bmax_ref[...]
        m_prev = m_ref[...]
        m_new = jnp.maximum(m_prev, m_blk)
        p = jnp.exp2(s - m_new)
        alpha = jnp.exp2(m_prev - m_new)
        acc_ref[...] = alpha * acc_ref[...] + jnp.dot(vtb, p.astype(BF16), preferred_element_type=F32)
        m_ref[...] = m_new

    if has_bias:
        qk(0, sa_ref)
        n_blocks = ((i + 1) * tq + tkb - 1) // tkb

        def pair(jj, c):
            j0 = 2 * jj
            qk(j0 + 1, sb_ref)
            softmax_pv(j0, sa_ref, False)
            qk(jnp.minimum(j0 + 2, n_blocks - 1), sa_ref)
            softmax_pv(j0 + 1, sb_ref, False)
            return c

        lax.fori_loop(0, n_blocks // 2, pair, 0)

        @pl.when(n_blocks % 2 == 1)
        def _():
            softmax_pv(n_blocks - 1, sa_ref, False)
    else:
        qk(2 * i, sa_ref)
        qk(2 * i + 1, sb_ref)
        softmax_pv(2 * i, sa_ref, True)
        qk(0, sa_ref)
        softmax_pv(2 * i + 1, sb_ref, True)

        def pair(jj, c):
            j0 = 2 * jj
            qk(j0 + 1, sb_ref)
            softmax_pv(j0, sa_ref, False)
            qk(jnp.minimum(j0 + 2, 2 * i - 1), sa_ref)
            softmax_pv(j0 + 1, sb_ref, False)
            return c

        lax.fori_loop(0, i, pair, 0)

    acc = acc_ref[...]
    out = (acc[0:dv] * (1.0 / acc[dv:dv + 1])).T
    for r in range(rep):
        o_ref[0, :, r * dv:(r + 1) * dv] = out[r * tq:(r + 1) * tq].astype(o_ref.dtype)


def flash_attention(q, k, vt, bias, *, tq, rep, out_dtype):
    b, hk, nq, rows, dq = q.shape
    s = k.shape[2]
    dve = vt.shape[2]
    dv = dve - V_ONES_ROWS
    tkb = _tile(s, FLASH_TKB)
    has_bias = bias is not None
    assert has_bias or tq == 2 * tkb
    in_specs = [pl.BlockSpec((1, 1, 1, rows, dq), lambda bb, h, i: (bb, h, i, 0, 0)),
                pl.BlockSpec((1, 1, s, dq), lambda bb, h, i: (bb, h, 0, 0)),
                pl.BlockSpec((1, 1, dve, s), lambda bb, h, i: (bb, h, 0, 0))]
    args = [q, k, vt]
    if has_bias:
        in_specs.append(pl.BlockSpec((1, s, tq), lambda bb, h, i: (bb, 0, i)))
        args.append(bias)
    return pl.pallas_call(
        functools.partial(_flash_kernel, tq=tq, tkb=tkb, rep=rep, dv=dv, has_bias=has_bias),
        grid=(b, hk, nq),
        in_specs=in_specs,
        out_specs=pl.BlockSpec((1, tq, rep * dv), lambda bb, h, i: (bb, i, h)),
        out_shape=jax.ShapeDtypeStruct((b, s, hk * rep * dv), out_dtype),
        scratch_shapes=[pltpu.VMEM((1, rows), F32), pltpu.VMEM((dve, rows), F32),
                        pltpu.VMEM((tkb, rows), F32), pltpu.VMEM((tkb, rows), F32),
                        pltpu.VMEM((1, rows), F32), pltpu.VMEM((1, rows), F32)],
        compiler_params=_params(("parallel", "parallel", "parallel")),
        name="flash_attention",
    )(*args)


S5_NSEG = 2 * SUBLANES
S5_CBLK = LANES
S5_LT = 16


def _s5_kernel(u_ref, bbd_ref, cbd_ref, a_ref, ap_ref, d_ref, y_ref,
               ug_ref, bu_ref, xs_ref, init_ref, up_ref, yp_ref, *, seg_len, lt, pitch):
    nseg = S5_NSEG
    ns = a_ref.shape[-1]
    a_re = jnp.broadcast_to(a_ref[0, 0:1, :], (nseg, ns))
    a_im = jnp.broadcast_to(a_ref[0, 1:2, :], (nseg, ns))
    ntiles = seg_len // lt
    for k in range(nseg):
        up_ref[k * pitch:k * pitch + seg_len, :] = u_ref[0, k * seg_len:(k + 1) * seg_len, :]

    def load_inputs(t, slot):
        for i in range(lt):
            ug_ref[slot, i * nseg:(i + 1) * nseg, :] = up_ref[pl.ds(t * lt + i, nseg, stride=pitch), :]
        bu_ref[slot] = jnp.dot(ug_ref[slot].astype(BF16), bbd_ref[0], preferred_element_type=F32)

    def scan_tile(carry, slot, store):
        x_re, x_im = carry
        for i in range(lt):
            r0 = i * nseg
            b_re = bu_ref[slot, r0:r0 + nseg, 0:ns]
            b_im = bu_ref[slot, r0:r0 + nseg, ns:2 * ns]
            n_re = a_re * x_re - a_im * x_im + b_re
            n_im = a_re * x_im + a_im * x_re + b_im
            x_re, x_im = n_re, n_im
            if store:
                xs_ref[slot, r0:r0 + nseg, 0:ns] = x_re
                xs_ref[slot, r0:r0 + nseg, ns:2 * ns] = x_im
        return x_re, x_im

    def emit(t, slot):
        y = jnp.dot(xs_ref[slot].astype(BF16), cbd_ref[0], preferred_element_type=F32)
        y = jax.nn.gelu(y + d_ref[0] * ug_ref[slot])
        for i in range(lt):
            yp_ref[pl.ds(t * lt + i, nseg, stride=pitch), :] = y[i * nseg:(i + 1) * nseg]

    def sweep(carry, store):
        load_inputs(0, 0)

        def pair(tt, c):
            t0 = 2 * tt
            load_inputs(t0 + 1, 1)
            c = scan_tile(c, 0, store)
            if store:
                emit(t0, 0)
            load_inputs(jnp.minimum(t0 + 2, ntiles - 1), 0)
            c = scan_tile(c, 1, store)
            if store:
                emit(t0 + 1, 1)
            return c

        return lax.fori_loop(0, ntiles // 2, pair, carry)

    zeros = jnp.zeros((nseg, ns), F32)
    e_re, e_im = sweep((zeros, zeros), False)

    ap_re = ap_ref[0, 0:1, :]
    ap_im = ap_ref[0, 1:2, :]
    t_re = jnp.zeros((1, ns), F32)
    t_im = jnp.zeros((1, ns), F32)
    init_ref[0:1, :] = jnp.zeros((1, 2 * ns), F32)
    for s in range(1, nseg):
        p_re, p_im = e_re[s - 1:s], e_im[s - 1:s]
        t_re, t_im = (ap_re * t_re - ap_im * t_im + p_re,
                      ap_re * t_im + ap_im * t_re + p_im)
        init_ref[s:s + 1, 0:ns] = t_re
        init_ref[s:s + 1, ns:2 * ns] = t_im

    sweep((init_ref[:, 0:ns], init_ref[:, ns:2 * ns]), True)
    for k in range(nseg):
        y_ref[0, k * seg_len:(k + 1) * seg_len, :] = yp_ref[k * pitch:k * pitch + seg_len, :]


def _s5_discretise(lam_re, lam_im, log_step, b_re, b_im, c_re, c_im, seg_len):
    g, n = lam_re.shape
    p = S5_GROUP
    lr = jnp.minimum(lam_re.astype(F32), S5_LAMBDA_RE_MAX)
    li = lam_im.astype(F32)
    step = jnp.exp(log_step.astype(F32))[:, None]
    mag = jnp.exp(lr * step)
    ab_re = mag * jnp.cos(li * step)
    ab_im = mag * jnp.sin(li * step)
    den = lr * lr + li * li
    f_re = ((ab_re - 1.0) * lr + ab_im * li) / den
    f_im = (ab_im * lr - (ab_re - 1.0) * li) / den
    br = b_re.astype(F32)
    bi = b_im.astype(F32)
    bb_re = f_re[..., None] * br - f_im[..., None] * bi
    bb_im = f_re[..., None] * bi + f_im[..., None] * br
    pw_re, pw_im = jnp.ones_like(ab_re), jnp.zeros_like(ab_im)
    sq_re, sq_im = ab_re, ab_im
    e = seg_len
    while e:
        if e & 1:
            pw_re, pw_im = pw_re * sq_re - pw_im * sq_im, pw_re * sq_im + pw_im * sq_re
        sq_re, sq_im = sq_re * sq_re - sq_im * sq_im, 2.0 * sq_re * sq_im
        e >>= 1
    gb = S5_CBLK // p
    nb = g // gb
    eye = jnp.eye(gb, dtype=F32)

    def blockdiag_in(bb):
        x = bb.reshape(nb, gb, n, p)
        return jnp.einsum("bgnp,gh->bgphn", x, eye).reshape(nb, gb * p, gb * n)

    def blockdiag_out(c):
        x = c.astype(F32).reshape(nb, gb, p, n)
        return jnp.einsum("bgpn,gh->bgnhp", x, eye).reshape(nb, gb * n, gb * p)

    bbd = jnp.concatenate([blockdiag_in(bb_re), blockdiag_in(bb_im)], axis=2).astype(BF16)
    cbd = jnp.concatenate([blockdiag_out(c_re), -blockdiag_out(c_im)], axis=1).astype(BF16)
    a = jnp.stack([ab_re.reshape(nb, gb * n), ab_im.reshape(nb, gb * n)], axis=1)
    ap = jnp.stack([pw_re.reshape(nb, gb * n), pw_im.reshape(nb, gb * n)], axis=1)
    return bbd, cbd, a, ap


def s5_scan(u, lam_re, lam_im, log_step, b_re, b_im, c_re, c_im, d_skip):
    b, s, w = u.shape
    seg_len = s // S5_NSEG
    lt = min(S5_LT, seg_len)
    assert (seg_len // lt) % 2 == 0
    bbd, cbd, a, ap = _s5_discretise(lam_re, lam_im, log_step, b_re, b_im, c_re, c_im, seg_len)
    nb, cw, ns2 = bbd.shape
    ns = ns2 // 2
    rows = lt * S5_NSEG
    pitch = seg_len + SUBLANES if (seg_len // SUBLANES) % 2 == 0 else seg_len
    return pl.pallas_call(
        functools.partial(_s5_kernel, seg_len=seg_len, lt=lt, pitch=pitch),
        grid=(b, nb),
        in_specs=[pl.BlockSpec((1, s, cw), lambda bb, j: (bb, 0, j)),
                  pl.BlockSpec((1, cw, ns2), lambda bb, j: (j, 0, 0)),
                  pl.BlockSpec((1, ns2, cw), lambda bb, j: (j, 0, 0)),
                  pl.BlockSpec((1, 2, ns), lambda bb, j: (j, 0, 0)),
                  pl.BlockSpec((1, 2, ns), lambda bb, j: (j, 0, 0)),
                  pl.BlockSpec((1, 1, cw), lambda bb, j: (j, 0, 0))],
        out_specs=pl.BlockSpec((1, s, cw), lambda bb, j: (bb, 0, j)),
        out_shape=jax.ShapeDtypeStruct((b, s, w), F32),
        scratch_shapes=[pltpu.VMEM((2, rows, cw), F32), pltpu.VMEM((2, rows, ns2), F32),
                        pltpu.VMEM((2, rows, ns2), F32), pltpu.VMEM((S5_NSEG, ns2), F32),
                        pltpu.VMEM((S5_NSEG * pitch, cw), F32), pltpu.VMEM((S5_NSEG * pitch, cw), F32)],
        compiler_params=_params(("parallel", "parallel")),
        name="s5_scan",
    )(u, bbd, cbd, a, ap, d_skip.astype(F32).reshape(nb, 1, cw))


def _even_lat_kernel(lat_ref, gq_ref, gkv_ref, cos_ref, sin_ref, qn_ref, kvn_ref, kr_ref, *, q_rank, kv_rank):
    lat = lat_ref[...]
    q_lat = lat[:, 0:q_rank]
    kv_lat = lat[:, q_rank:q_rank + kv_rank]
    kr = lat[:, q_rank + kv_rank:q_rank + kv_rank + LANES]
    rq = lax.rsqrt(jnp.mean(q_lat * q_lat, axis=-1, keepdims=True) + EPS)
    qn_ref[...] = (q_lat * rq * gq_ref[...]).astype(qn_ref.dtype)
    rkv = lax.rsqrt(jnp.mean(kv_lat * kv_lat, axis=-1, keepdims=True) + EPS)
    kvn_ref[...] = (kv_lat * rkv * gkv_ref[...]).astype(kvn_ref.dtype)
    kr = _rope_tile(kr, cos_ref[...], sin_ref[...], MLA_ROPE // 2)
    lane = lax.broadcasted_iota(jnp.int32, kr.shape, 1)
    lo = jnp.where(lane < MLA_ROPE, kr, 0.0)
    hi = pltpu.roll(lo, MLA_ROPE, 1)
    kr_ref[:, 0:LANES] = lo.astype(kr_ref.dtype)
    kr_ref[:, LANES:2 * LANES] = hi.astype(kr_ref.dtype)


def even_lat_prep(lat, gq, gkv, cos_t, sin_t):
    m, wl = lat.shape
    q_rank, kv_rank = gq.shape[0], gkv.shape[0]
    tm = _tile(m, 512, SUBLANES)
    return pl.pallas_call(
        functools.partial(_even_lat_kernel, q_rank=q_rank, kv_rank=kv_rank),
        grid=(m // tm,),
        in_specs=[pl.BlockSpec((tm, wl), lambda i: (i, 0)),
                  pl.BlockSpec((1, q_rank), lambda i: (0, 0)),
                  pl.BlockSpec((1, kv_rank), lambda i: (0, 0)),
                  pl.BlockSpec((tm, LANES), lambda i: (i, 0)),
                  pl.BlockSpec((tm, LANES), lambda i: (i, 0))],
        out_specs=[pl.BlockSpec((tm, q_rank), lambda i: (i, 0)),
                   pl.BlockSpec((tm, kv_rank), lambda i: (i, 0)),
                   pl.BlockSpec((tm, 2 * LANES), lambda i: (i, 0))],
        out_shape=[jax.ShapeDtypeStruct((m, q_rank), BF16),
                   jax.ShapeDtypeStruct((m, kv_rank), BF16),
                   jax.ShapeDtypeStruct((m, 2 * LANES), BF16)],
        compiler_params=_params(("parallel",)),
        name="even_lat_prep",
    )(lat, gq.reshape(1, -1).astype(F32), gkv.reshape(1, -1).astype(F32), cos_t, sin_t)


def _mla_q_kernel(qn_ref, w_ref, cos_ref, sin_ref, qf_ref, *, heads, scale):
    q = jnp.dot(qn_ref[...], w_ref[...], preferred_element_type=F32)
    cos_t, sin_t = cos_ref[...], sin_ref[...]
    nope_w = heads * MLA_NOPE
    lane = lax.broadcasted_iota(jnp.int32, (q.shape[0], LANES), 1)
    for pair in range(heads // 2):
        rope = _rope_tile(q[:, nope_w + pair * LANES:nope_w + (pair + 1) * LANES], cos_t, sin_t, MLA_ROPE // 2)
        for sub in range(2):
            h = 2 * pair + sub
            keep = (lane < MLA_ROPE) if sub == 0 else (lane >= MLA_ROPE)
            qf_ref[0, h, :, 0:LANES] = (q[:, h * MLA_NOPE:(h + 1) * MLA_NOPE] * scale).astype(qf_ref.dtype)
            qf_ref[0, h, :, LANES:2 * LANES] = (jnp.where(keep, rope, 0.0) * scale).astype(qf_ref.dtype)


def _mla_kv_kernel(kvn_ref, w_ref, kr_ref, kf_ref, vt_ref, *, heads):
    kv = jnp.dot(kvn_ref[...], w_ref[...], preferred_element_type=F32)
    for h in range(heads):
        c0 = h * (MLA_NOPE + MLA_V)
        kf_ref[0, h, :, 0:LANES] = kv[:, c0:c0 + MLA_NOPE].astype(kf_ref.dtype)
        kf_ref[0, h, :, LANES:2 * LANES] = kr_ref[:, (h % 2) * LANES:(h % 2 + 1) * LANES]
        vt_ref[0, h, 0:MLA_V, :] = kv[:, c0 + MLA_NOPE:c0 + MLA_NOPE + MLA_V].T.astype(vt_ref.dtype)
        vt_ref[0, h, MLA_V:MLA_V + V_ONES_ROWS, :] = jnp.ones((V_ONES_ROWS, kv.shape[0]), vt_ref.dtype)


def mla_projections(qn, kvn, kr2, wq, wkv, cos_t, sin_t, b, s, heads):
    t = qn.shape[0]
    tm = _tile(s, 256, SUBLANES)
    nt = s // tm
    scale = (MLA_NOPE + MLA_ROPE) ** -0.5 * LOG2E
    head_spec = pl.BlockSpec((1, heads, tm, 2 * LANES), lambda i: (i // nt, 0, i % nt, 0))
    qf = pl.pallas_call(
        functools.partial(_mla_q_kernel, heads=heads, scale=scale),
        grid=(t // tm,),
        in_specs=[pl.BlockSpec((tm, qn.shape[1]), lambda i: (i, 0)),
                  pl.BlockSpec(wq.shape, lambda i: (0, 0)),
                  pl.BlockSpec((tm, LANES), lambda i: (i, 0)),
                  pl.BlockSpec((tm, LANES), lambda i: (i, 0))],
        out_specs=head_spec,
        out_shape=jax.ShapeDtypeStruct((b, heads, s, 2 * LANES), BF16),
        compiler_params=_params(("parallel",)),
        name="mla_q_proj",
    )(qn, wq, cos_t, sin_t)
    kf, vt = pl.pallas_call(
        functools.partial(_mla_kv_kernel, heads=heads),
        grid=(t // tm,),
        in_specs=[pl.BlockSpec((tm, kvn.shape[1]), lambda i: (i, 0)),
                  pl.BlockSpec(wkv.shape, lambda i: (0, 0)),
                  pl.BlockSpec((tm, 2 * LANES), lambda i: (i, 0))],
        out_specs=[head_spec,
                   pl.BlockSpec((1, heads, MLA_V + V_ONES_ROWS, tm), lambda i: (i // nt, 0, 0, i % nt))],
        out_shape=[jax.ShapeDtypeStruct((b, heads, s, 2 * LANES), BF16),
                   jax.ShapeDtypeStruct((b, heads, MLA_V + V_ONES_ROWS, s), BF16)],
        compiler_params=_params(("parallel",)),
        name="mla_kv_proj",
    )(kvn, wkv, kr2)
    return qf, kf, vt


def even_mixer(x2, h, b, s, cos_m, sin_m, w_in, lam_re, lam_im, log_step, b_re, b_im, c_re, c_im,
               d_skip, glu_a, glu_b, q_norm_g, w_q_up, kv_norm_g, w_kv_up, w_out):
    t, d = h.shape
    s5_w = d_skip.shape[0]
    q_rank, kv_rank = q_norm_g.shape[0], kv_norm_g.shape[0]
    heads = w_kv_up.shape[1] // (MLA_NOPE + MLA_V)
    lat_w = q_rank + kv_rank + MLA_ROPE
    lat_pad = -(-(q_rank + kv_rank + LANES) // (2 * LANES)) * (2 * LANES)

    w_u = w_in[:, :s5_w].astype(BF16)
    w_lat = jnp.pad(w_in[:, s5_w:], ((0, 0), (0, lat_pad - lat_w))).astype(BF16)
    u = matmul(h, w_u, F32)
    lat = matmul(h, w_lat, F32)

    y = s5_scan(u.reshape(b, s, s5_w), lam_re, lam_im, log_step, b_re, b_im, c_re, c_im, d_skip)
    s5_out = glu_matmul(y.reshape(t, s5_w), glu_a.astype(BF16), glu_b.astype(BF16), BF16)

    qn, kvn, kr2 = even_lat_prep(lat, q_norm_g, kv_norm_g, cos_m, sin_m)
    wq = w_q_up.reshape(q_rank, heads, MLA_NOPE + MLA_ROPE)
    wq = jnp.concatenate([wq[:, :, :MLA_NOPE].reshape(q_rank, heads * MLA_NOPE),
                          wq[:, :, MLA_NOPE:].reshape(q_rank, heads * MLA_ROPE)], axis=1).astype(BF16)
    qf, kf, vf = mla_projections(qn, kvn, kr2, wq, w_kv_up.astype(BF16), cos_m, sin_m, b, s, heads)
    tq = 2 * _tile(s, FLASH_TKB)
    mla_out = flash_attention(qf.reshape(b, heads, s // tq, tq, 2 * LANES), kf, vf, None,
                              tq=tq, rep=1, out_dtype=BF16)
    mixed = jnp.concatenate([s5_out, mla_out.reshape(t, -1)], axis=-1)
    return matmul(mixed, w_out.astype(BF16), F32, kind="residual", res=x2)


def _odd_lat_kernel(z_ref, gq_ref, lng_ref, lnb_ref, cos_ref, sin_ref,
                    qn_ref, k_ref, v_ref, kidx_ref, w_ref, *, q_rank, w_scale):
    cos_t, sin_t = cos_ref[...], sin_ref[...]
    half = ROT_DIM // 2
    z = z_ref[0]
    q_lat = z[:, 0:q_rank]
    rq = lax.rsqrt(jnp.mean(q_lat * q_lat, axis=-1, keepdims=True) + EPS)
    qn_ref[0] = (q_lat * rq * gq_ref[...]).astype(qn_ref.dtype)
    kvh = DSA_KV_HEADS
    d = DSA_HEAD_DIM
    for g in range(kvh):
        kh = z[:, q_rank + g * d:q_rank + (g + 1) * d]
        k_ref[0, g] = _rope_tile(kh, cos_t, sin_t, half).astype(k_ref.dtype)
        v_ref[0, g, 0:d, :] = z[:, q_rank + (kvh + g) * d:q_rank + (kvh + g + 1) * d].T.astype(v_ref.dtype)
        v_ref[0, g, d:d + V_ONES_ROWS, :] = jnp.ones((V_ONES_ROWS, z.shape[0]), v_ref.dtype)
    off = q_rank + 2 * kvh * d
    ki = z[:, off:off + IDX_DIM]
    kc = ki - jnp.mean(ki, axis=-1, keepdims=True)
    var = jnp.mean(kc * kc, axis=-1, keepdims=True)
    ki = kc * lax.rsqrt(var + EPS) * lng_ref[...] + lnb_ref[...]
    kidx_ref[0] = _rope_tile(ki, cos_t, sin_t, half).astype(kidx_ref.dtype)
    w_ref[0] = (z[:, off + IDX_DIM:off + IDX_DIM + LANES] * w_scale).T


def odd_lat_prep(z, gq, ln_g, ln_b, cos_t, sin_t, idx_heads):
    b, s, zw = z.shape
    q_rank = gq.shape[0]
    tm = _tile(s, 512, SUBLANES)
    nt = s // tm
    w_scale = idx_heads ** -0.5 * IDX_DIM ** -0.5
    kvh, d = DSA_KV_HEADS, DSA_HEAD_DIM
    return pl.pallas_call(
        functools.partial(_odd_lat_kernel, q_rank=q_rank, w_scale=w_scale),
        grid=(b, nt),
        in_specs=[pl.BlockSpec((1, tm, zw), lambda bb, i: (bb, i, 0)),
                  pl.BlockSpec((1, q_rank), lambda bb, i: (0, 0)),
                  pl.BlockSpec((1, IDX_DIM), lambda bb, i: (0, 0)),
                  pl.BlockSpec((1, IDX_DIM), lambda bb, i: (0, 0)),
                  pl.BlockSpec((tm, LANES), lambda bb, i: (bb * nt + i, 0)),
                  pl.BlockSpec((tm, LANES), lambda bb, i: (bb * nt + i, 0))],
        out_specs=[pl.BlockSpec((1, tm, q_rank), lambda bb, i: (bb, i, 0)),
                   pl.BlockSpec((1, kvh, tm, d), lambda bb, i: (bb, 0, i, 0)),
                   pl.BlockSpec((1, kvh, d + V_ONES_ROWS, tm), lambda bb, i: (bb, 0, 0, i)),
                   pl.BlockSpec((1, tm, IDX_DIM), lambda bb, i: (bb, i, 0)),
                   pl.BlockSpec((1, LANES, tm), lambda bb, i: (bb, 0, i))],
        out_shape=[jax.ShapeDtypeStruct((b, s, q_rank), BF16),
                   jax.ShapeDtypeStruct((b, kvh, s, d), BF16),
                   jax.ShapeDtypeStruct((b, kvh, d + V_ONES_ROWS, s), BF16),
                   jax.ShapeDtypeStruct((b, s, IDX_DIM), BF16),
                   jax.ShapeDtypeStruct((b, LANES, s), F32)],
        compiler_params=_params(("parallel", "parallel")),
        name="odd_lat_prep",
    )(z, gq.reshape(1, -1).astype(F32), ln_g.reshape(1, -1).astype(F32),
      ln_b.reshape(1, -1).astype(F32), cos_t, sin_t)


def _mm_rope_kernel(a_ref, w_ref, cos_ref, sin_ref, o_ref, *, n_heads, scale, stack_rows):
    acc = jnp.dot(a_ref[...], w_ref[...], preferred_element_type=F32)
    cos_t, sin_t = cos_ref[...], sin_ref[...]
    d = DSA_HEAD_DIM
    for r in range(n_heads):
        xh = _rope_tile(acc[:, r * d:(r + 1) * d], cos_t, sin_t, ROT_DIM // 2) * scale
        if stack_rows:
            o_ref[0, 0, 0, r * stack_rows:(r + 1) * stack_rows, :] = xh.astype(o_ref.dtype)
        else:
            o_ref[:, r * d:(r + 1) * d] = xh.astype(o_ref.dtype)


def dsa_q_proj(qn, w_q, cos_t, sin_t, b, s, tq):
    t, kq = qn.shape
    d = DSA_HEAD_DIM
    heads = w_q.shape[1] // d
    rep = heads // DSA_KV_HEADS
    nt = s // tq
    return pl.pallas_call(
        functools.partial(_mm_rope_kernel, n_heads=rep, scale=d ** -0.5 * LOG2E, stack_rows=tq),
        grid=(t // tq, DSA_KV_HEADS),
        in_specs=[pl.BlockSpec((tq, kq), lambda i, g: (i, 0)),
                  pl.BlockSpec((kq, rep * d), lambda i, g: (0, g)),
                  pl.BlockSpec((tq, LANES), lambda i, g: (i, 0)),
                  pl.BlockSpec((tq, LANES), lambda i, g: (i, 0))],
        out_specs=pl.BlockSpec((1, 1, 1, rep * tq, d), lambda i, g: (i // nt, g, i % nt, 0, 0)),
        out_shape=jax.ShapeDtypeStruct((b, DSA_KV_HEADS, nt, rep * tq, d), BF16),
        compiler_params=_params(("parallel", "parallel")),
        name="dsa_q_proj",
    )(qn, w_q, cos_t, sin_t)


def idx_q_proj(qn, w_qi, cos_t, sin_t):
    t, kq = qn.shape
    n = w_qi.shape[1]
    tm = _tile(t, 1024, SUBLANES)
    tn = _tile(n, 512)
    return pl.pallas_call(
        functools.partial(_mm_rope_kernel, n_heads=tn // IDX_DIM, scale=1.0, stack_rows=0),
        grid=(t // tm, n // tn),
        in_specs=[pl.BlockSpec((tm, kq), lambda i, j: (i, 0)),
                  pl.BlockSpec((kq, tn), lambda i, j: (0, j)),
                  pl.BlockSpec((tm, LANES), lambda i, j: (i, 0)),
                  pl.BlockSpec((tm, LANES), lambda i, j: (i, 0))],
        out_specs=pl.BlockSpec((tm, tn), lambda i, j: (i, j)),
        out_shape=jax.ShapeDtypeStruct((t, n), BF16),
        compiler_params=_params(("parallel", "parallel")),
        name="idx_q_proj",
    )(qn, w_qi, cos_t, sin_t)


def _ukey_to_float(u):
    bits = jnp.where(u < 0, u & jnp.int32(0x7FFFFFFF), ~u)
    return lax.bitcast_convert_type(bits, F32)


def _threshold_of_key(u):
    thr = _ukey_to_float(u)
    return jnp.where((thr != thr) & (u >= 0), jnp.float32(-jnp.inf), thr)


def _indexer_kernel(qi_ref, wt_ref, kidx_ref, bias_ref, score_ref, lim_ref, *, tq, tkb, idx_heads, top_k):
    i = pl.program_id(1)
    s_len = kidx_ref.shape[1]
    n_blocks = ((i + 1) * tq + tkb - 1) // tkb
    neg_inf = jnp.float32(-jnp.inf)
    qpos = i * tq + lax.broadcasted_iota(jnp.int32, (1, tq), 1)

    def causal(start):
        kpos = start + lax.broadcasted_iota(jnp.int32, (tkb, tq), 0)
        return kpos <= qpos

    def score_block(j, c):
        start = pl.multiple_of(j * tkb, tkb)
        kb = kidx_ref[0, pl.ds(start, tkb), :]
        acc = jnp.zeros((tkb, tq), F32)
        for h in range(idx_heads):
            logits = lax.dot_general(kb, qi_ref[0, :, h * IDX_DIM:(h + 1) * IDX_DIM],
                                     (((1,), (1,)), ((), ())), preferred_element_type=F32)
            acc = acc + jnp.maximum(logits, 0.0) * wt_ref[0, h:h + 1, :]
        score_ref[pl.ds(start, tkb), :] = jnp.where(causal(start), acc, neg_inf)
        return c

    lax.fori_loop(0, n_blocks, score_block, 0)

    def count_where(pred):
        def body(j, cnt):
            start = pl.multiple_of(j * tkb, tkb)
            ind = jnp.where(pred(score_ref[pl.ds(start, tkb), :], start), 1.0, 0.0)
            part = jnp.sum(ind.reshape(tkb // (8 * SUBLANES), 8, SUBLANES, tq), axis=1)
            return cnt + jnp.sum(part, axis=0)
        cnt = lax.fori_loop(0, n_blocks, body, jnp.zeros((SUBLANES, tq), F32))
        return jnp.sum(cnt, axis=0, keepdims=True)

    def count_ge(thr):
        return count_where(lambda blk, start: blk >= thr)

    k_f = jnp.float32(top_k)
    few = qpos < top_k

    def search_cond(st):
        bi, _, cnt_u = st
        pending = jnp.where(few | (cnt_u == k_f), 0.0, 1.0)
        return (bi < 32) & (jnp.max(pending) > 0.0)

    def search_step(st):
        bi, u, cnt_u = st
        cand = u | (jnp.int32(1) << (31 - bi))
        cnt = count_ge(_threshold_of_key(cand))
        take = cnt >= k_f
        return bi + 1, jnp.where(take, cand, u), jnp.where(take, cnt, cnt_u)

    total = (n_blocks * tkb).astype(F32)
    _, u, cnt_u = lax.while_loop(search_cond, search_step,
                                 (jnp.int32(0), jnp.zeros((1, tq), jnp.int32), jnp.full((1, tq), total, F32)))
    thr = jnp.where(few, neg_inf, _threshold_of_key(u))

    def key_pos(start):
        return start + lax.broadcasted_iota(jnp.int32, (tkb, tq), 0)

    excess = jnp.logical_not(few) & (cnt_u > k_f)
    lim_ref[...] = jnp.full((1, tq), s_len, jnp.int32)

    @pl.when(jnp.max(jnp.where(excess, 1.0, 0.0)) > 0.0)
    def _():
        need = k_f - count_where(lambda blk, start: blk > thr)
        nbits = s_len.bit_length()

        def tie_step(bi, p):
            cand = p | (jnp.int32(1) << (nbits - 1 - bi))
            cnt = count_where(lambda blk, start: (blk == thr) & (key_pos(start) < cand))
            return jnp.where(cnt <= need, cand, p)

        p = lax.fori_loop(0, nbits, tie_step, jnp.zeros((1, tq), jnp.int32))
        lim_ref[...] = jnp.where(excess, p, s_len)

    lim = lim_ref[...]

    def write_block(j, c):
        start = pl.multiple_of(j * tkb, tkb)
        blk = score_ref[pl.ds(start, tkb), :]
        keep = ((blk > thr) | ((blk == thr) & (key_pos(start) < lim))) & causal(start)
        bias_ref[0, pl.ds(start, tkb), :] = jnp.where(keep, 0.0, MASK_VALUE).astype(bias_ref.dtype)
        return c

    lax.fori_loop(0, n_blocks, write_block, 0)

    def fill_block(j, c):
        start = pl.multiple_of(j * tkb, tkb)
        bias_ref[0, pl.ds(start, tkb), :] = jnp.full((tkb, tq), MASK_VALUE, bias_ref.dtype)
        return c

    lax.fori_loop(n_blocks, s_len // tkb, fill_block, 0)


IDX_TQ = 512


def dsa_indexer(qi, wt, kidx, idx_heads, top_k):
    b, s, _ = qi.shape
    tq = _tile(s, IDX_TQ)
    tkb = _tile(s, 512)
    assert tkb >= top_k or tkb == s
    return pl.pallas_call(
        functools.partial(_indexer_kernel, tq=tq, tkb=tkb, idx_heads=idx_heads, top_k=top_k),
        grid=(b, s // tq),
        in_specs=[pl.BlockSpec((1, tq, idx_heads * IDX_DIM), lambda bb, i: (bb, i, 0)),
                  pl.BlockSpec((1, LANES, tq), lambda bb, i: (bb, 0, i)),
                  pl.BlockSpec((1, s, IDX_DIM), lambda bb, i: (bb, 0, 0))],
        out_specs=pl.BlockSpec((1, s, tq), lambda bb, i: (bb, 0, i)),
        out_shape=jax.ShapeDtypeStruct((b, s, s), BF16),
        scratch_shapes=[pltpu.VMEM((s, tq), F32), pltpu.VMEM((1, tq), jnp.int32)],
        compiler_params=_params(("parallel", "parallel")),
        name="dsa_indexer",
    )(qi, wt, kidx)


def odd_mixer(x2, h, b, s, cos_p, sin_p, w_in, q_norm_g, w_q_up, w_idx_q, k_ln_g, k_ln_b, w_out):
    t, d = h.shape
    q_rank = q_norm_g.shape[0]
    heads = w_q_up.shape[1] // DSA_HEAD_DIM
    idx_heads = w_idx_q.shape[1] // IDX_DIM
    in_w = w_in.shape[1]
    z_w = -(-(in_w - idx_heads + LANES) // (2 * LANES)) * (2 * LANES)
    w_z = jnp.pad(w_in, ((0, 0), (0, z_w - in_w))).astype(BF16)
    z = matmul(h, w_z, F32)
    qn, k, vt, kidx, wt = odd_lat_prep(z.reshape(b, s, z_w), q_norm_g, k_ln_g, k_ln_b, cos_p, sin_p, idx_heads)
    tq = _tile(s, 256)
    qn2 = qn.reshape(t, q_rank)
    q = dsa_q_proj(qn2, w_q_up.astype(BF16), cos_p, sin_p, b, s, tq)
    qi = idx_q_proj(qn2, w_idx_q.astype(BF16), cos_p, sin_p).reshape(b, s, -1)
    top_k = min(IDX_TOPK_MAX, s // 4)
    bias = dsa_indexer(qi, wt, kidx, idx_heads, top_k)
    o = flash_attention(q, k, vt, bias, tq=tq, rep=heads // DSA_KV_HEADS, out_dtype=BF16)
    return matmul(o.reshape(t, -1), w_out.astype(BF16), F32, kind="residual", res=x2)


def sq_relu_mlp(x2, h, w_up, w_down, layer):
    a = matmul(h, w_up, BF16, kind="relu2", layer=layer)
    return matmul(a, w_down, F32, kind="residual", res=x2, layer=layer)


def kernel(x, positions, norm_mix_g, norm_mlp_g, final_norm_g, even_w_in, s5_lam_re, s5_lam_im, s5_log_step, s5_b_re, s5_b_im, s5_c_re, s5_c_im, s5_d, s5_glu_a, s5_glu_b, mla_q_norm_g, mla_w_q_up, mla_kv_norm_g, mla_w_kv_up, even_w_out, odd_w_in, dsa_q_norm_g, dsa_w_q_up, idx_w_q, idx_k_ln_g, idx_k_ln_b, odd_w_out, mlp_w_up, mlp_w_down):
    b, s, d = x.shape
    depth = norm_mix_g.shape[0]
    cos_m, sin_m = _rope_tables(positions, MLA_ROPE, LANES)
    cos_p, sin_p = _rope_tables(positions, ROT_DIM, ROT_DIM)
    x2 = x.reshape(b * s, d)
    for layer in range(depth):
        i = layer // 2
        h = rmsnorm(x2, norm_mix_g[layer], BF16)
        if layer % 2 == 0:
            x2 = even_mixer(x2, h, b, s, cos_m, sin_m, even_w_in[i], s5_lam_re[i], s5_lam_im[i],
                            s5_log_step[i], s5_b_re[i], s5_b_im[i], s5_c_re[i], s5_c_im[i], s5_d[i],
                            s5_glu_a[i], s5_glu_b[i], mla_q_norm_g[i], mla_w_q_up[i],
                            mla_kv_norm_g[i], mla_w_kv_up[i], even_w_out[i])
        else:
            x2 = odd_mixer(x2, h, b, s, cos_p, sin_p, odd_w_in[i], dsa_q_norm_g[i], dsa_w_q_up[i],
                           idx_w_q[i], idx_k_ln_g[i], idx_k_ln_b[i], odd_w_out[i])
        h = rmsnorm(x2, norm_mlp_g[layer], BF16)
        x2 = sq_relu_mlp(x2, h, mlp_w_up, mlp_w_down, layer)
    return rmsnorm(x2, final_norm_g, x.dtype).reshape(b, s, d)
```
